```python
import jax, jax.numpy as jnp
from jax import lax
import numpy as np

D_MODEL = 2048
BATCH = 2
SEQ = 4096
DEPTH = 2
DEC_BATCH = 128
DEC_SEQ = 1
PAST_LEN = 8192
PAGE_SIZE = 128

GLA_HEADS = 4
GLA_DK = 256
GLA_DV = 512
GLA_KEY = GLA_HEADS * GLA_DK
GLA_VAL = GLA_HEADS * GLA_DV
GLA_RANK = 16
GLA_GATE_NORM = 16.0
GLA_CHUNK = 16
GM_WIDTH = 1024
GM_GROUPS = 4
GM_GW = GM_WIDTH // GM_GROUPS
GM_CHUNK = 128
SWA_HQ = 16
SWA_HKV = 4
SWA_G = SWA_HQ // SWA_HKV
SWA_HD = 64
WINDOW = 128
FFN_HIDDEN = ((8 * D_MODEL // 3 + 255) // 256) * 256
N_BRANCH = 3
IN_WIDTHS = (GLA_KEY, GLA_KEY, GLA_VAL, GLA_VAL, GLA_RANK,
             GM_WIDTH, GM_WIDTH,
             SWA_HQ * SWA_HD, SWA_HKV * SWA_HD, SWA_HKV * SWA_HD,
             N_BRANCH * D_MODEL)
IN_WIDTH = sum(IN_WIDTHS)
EPS = 1e-6
NEG_BIG = -1e30

kernel_name = "hybrid_gla_gmlp_swa_adaln_step"


def _split_points():
    return [int(v) for v in np.cumsum(np.array(IN_WIDTHS))[:-1]]


def _rmsnorm(x, w):
    xf = x.astype(jnp.float32)
    y = xf * lax.rsqrt(jnp.mean(xf * xf, axis=-1, keepdims=True) + EPS)
    return (y * w.astype(jnp.float32)).astype(x.dtype)


def _layernorm(x, w, b):
    xf = x.astype(jnp.float32)
    mu = jnp.mean(xf, axis=-1, keepdims=True)
    var = jnp.mean(jnp.square(xf - mu), axis=-1, keepdims=True)
    y = (xf - mu) * lax.rsqrt(var + EPS)
    return (y * w.astype(jnp.float32) + b.astype(jnp.float32)).astype(x.dtype)


def _alibi_slopes():
    return jnp.exp2(-8.0 * jnp.arange(1, SWA_HQ + 1, dtype=jnp.float32) / SWA_HQ)


def _gla_chunked(q, k, v, gk, s0):
    B, T, H, _ = q.shape
    C = min(GLA_CHUNK, T)
    N = T // C

    def blocks(a):
        return a.astype(jnp.float32).reshape(B, N, C, H, a.shape[-1]).transpose(1, 0, 3, 2, 4)

    causal = jnp.tril(jnp.ones((C, C), dtype=bool))

    def step(S, inp):
        qc, kc, vc, gc = inp
        b = jnp.cumsum(gc, axis=-2)
        bl = b[:, :, -1:, :]
        q_in = qc * jnp.exp(b)
        a = jnp.einsum('bhtd,bhsd->bhts', q_in, kc * jnp.exp(-b))
        a = jnp.where(causal, a, 0.0)
        o = jnp.einsum('bhts,bhsv->bhtv', a, vc) + jnp.einsum('bhtd,bhdv->bhtv', q_in, S)
        S = jnp.exp(bl[:, :, 0, :])[..., None] * S + jnp.einsum('bhsd,bhsv->bhdv', kc * jnp.exp(bl - b), vc)
        return S, o

    S, o = lax.scan(step, s0.astype(jnp.float32), (blocks(q), blocks(k), blocks(v), blocks(gk)))
    o = o.transpose(1, 0, 3, 2, 4).reshape(B, T, H, GLA_DV)
    return o.astype(v.dtype), S


def _gla_recurrent(q, k, v, gk, s0):
    def step(S, inp):
        qt, kt, vt, gt = inp
        S = jnp.exp(gt)[..., None] * S + kt[..., None] * vt[..., None, :]
        return S, jnp.einsum('bhd,bhdv->bhv', qt, S)

    xs = tuple(a.astype(jnp.float32).transpose(1, 0, 2, 3) for a in (q, k, v, gk))
    S, o = lax.scan(step, s0.astype(jnp.float32), xs)
    return o.transpose(1, 0, 2, 3).astype(v.dtype), S


def _spatial_gate(u, v, ws, bs, nw, nb):
    B, T, _ = u.shape
    C = min(GM_CHUNK, T)
    N = T // C
    u = jax.nn.gelu(u)
    v = _layernorm(jax.nn.gelu(v), nw, nb)
    vb = v.reshape(B, N, C, GM_GROUPS, GM_GW)
    w_m = jnp.where(jnp.tril(jnp.ones((C, C), dtype=bool)), ws[:, :C, :C], 0.0)
    mixed = jnp.einsum('gts,bnsgc->bntgc', w_m, vb) + bs[:, :C].T[None, None, :, :, None]
    return u * mixed.reshape(B, T, GM_WIDTH), v


def _sink_attention(q, k, v, dist, valid, sinks):
    slopes = _alibi_slopes().reshape(SWA_HKV, SWA_G)[None, None, :, :, None, None]
    s = jnp.einsum('bnqkgd,bnskd->bnkgqs', q, k).astype(jnp.float32) * (SWA_HD ** -0.5)
    s = s - slopes * dist.astype(jnp.float32)[None, :, None, None]
    s = jnp.where(valid[None, :, None, None], s, NEG_BIG)
    sink = sinks.astype(jnp.float32).reshape(SWA_HKV, SWA_G)[None, None, :, :, None, None]
    m = jnp.maximum(jnp.max(s, axis=-1, keepdims=True), sink)
    p = jnp.exp(s - m)
    probs = p / (jnp.sum(p, axis=-1, keepdims=True) + jnp.exp(sink - m))
    return jnp.einsum('bnkgqs,bnskd->bnqkgd', probs.astype(v.dtype), v)


def _swa_prompt(q, k, v, sinks):
    B, T = q.shape[:2]
    W = WINDOW
    N = T // W
    qb = q.reshape(B, N, W, SWA_HKV, SWA_G, SWA_HD)
    kb = k.reshape(B, N, W, SWA_HKV, SWA_HD)
    vb = v.reshape(B, N, W, SWA_HKV, SWA_HD)
    kcat = jnp.concatenate([jnp.concatenate([jnp.zeros_like(kb[:, :1]), kb[:, :-1]], axis=1), kb], axis=2)
    vcat = jnp.concatenate([jnp.concatenate([jnp.zeros_like(vb[:, :1]), vb[:, :-1]], axis=1), vb], axis=2)
    i = jnp.arange(W)[:, None]
    j = jnp.arange(2 * W)[None, :]
    dist = (W + i - j)[None]
    keypos = jnp.arange(N)[:, None, None] * W - W + j[None]
    valid = (dist >= 0) & (dist < WINDOW) & (keypos >= 0)
    o = _sink_attention(qb, kcat, vcat, dist, valid, sinks)
    return o.reshape(B, T, SWA_HQ * SWA_HD)


def _swa_sample(q, k, v, kbuf, vbuf, sinks):
    B, S = q.shape[:2]
    Wb = kbuf.shape[1]
    kcat = jnp.concatenate([kbuf.astype(k.dtype), k], axis=1)[:, None]
    vcat = jnp.concatenate([vbuf.astype(v.dtype), v], axis=1)[:, None]
    qb = q.reshape(B, 1, S, SWA_HKV, SWA_G, SWA_HD)
    dist = (jnp.arange(S)[:, None] + Wb - jnp.arange(Wb + S)[None, :])[None]
    valid = (dist >= 0) & (dist < WINDOW)
    o = _sink_attention(qb, kcat, vcat, dist, valid, sinks)
    return o.reshape(B, S, SWA_HQ * SWA_HD)


def _mixer_block(h, p, gla_s0, kbuf, vbuf):
    B, T, _ = h.shape
    z = h @ p['w_in']
    qa, ka, va, ga, lra, ub, vb, qc, kc, vc, gates = jnp.split(z, _split_points(), axis=-1)
    gk = jax.nn.log_sigmoid(lra @ p['w_gk2'] + p['b_gk']) / GLA_GATE_NORM
    qa = qa.reshape(B, T, GLA_HEADS, GLA_DK) * (GLA_DK ** -0.5)
    ka = ka.reshape(B, T, GLA_HEADS, GLA_DK)
    va = va.reshape(B, T, GLA_HEADS, GLA_DV)
    gk = gk.reshape(B, T, GLA_HEADS, GLA_DK)
    if gla_s0 is None:
        s0 = jnp.zeros((B, GLA_HEADS, GLA_DK, GLA_DV), jnp.float32)
        oa, s_new = _gla_chunked(qa, ka, va, gk, s0)
    else:
        oa, s_new = _gla_recurrent(qa, ka, va, gk, gla_s0)
    oa = _rmsnorm(oa, p['gla_norm_w']) * jax.nn.silu(ga.reshape(B, T, GLA_HEADS, GLA_DV))
    oa = oa.reshape(B, T, GLA_VAL)
    ob, v_gm = _spatial_gate(ub, vb, p['gm_ws'], p['gm_bs'], p['gm_norm_w'], p['gm_norm_b'])
    qc = qc.reshape(B, T, SWA_HQ, SWA_HD)
    kc = kc.reshape(B, T, SWA_HKV, SWA_HD)
    vc = vc.reshape(B, T, SWA_HKV, SWA_HD)
    if kbuf is None:
        oc = _swa_prompt(qc, kc, vc, p['swa_sinks'])
        keep = min(WINDOW, T)
        k_rows, v_rows = kc[:, T - keep:], vc[:, T - keep:]
    else:
        oc = _swa_sample(qc, kc, vc, kbuf, vbuf, p['swa_sinks'])
        k_rows, v_rows = kc, vc
    g_a, g_b, g_c = jnp.split(jax.nn.sigmoid(gates), N_BRANCH, axis=-1)
    merged = g_a * (oa @ p['w_pa']) + g_b * (ob @ p['w_pb']) + g_c * (oc @ p['w_pc'])
    return merged @ p['w_o'], s_new, k_rows, v_rows, v_gm


def _layer(x, c, p, gla_s0, kbuf, vbuf):
    mod = (jax.nn.silu(c) @ p['w_ada'] + p['b_ada'])[:, None, :]
    sh1, sc1, g1, sh2, sc2, g2 = jnp.split(mod, 6, axis=-1)
    h = _rmsnorm(x, p['norm1_w']) * (1.0 + sc1) + sh1
    mix, s_new, k_rows, v_rows, v_gm = _mixer_block(h, p, gla_s0, kbuf, vbuf)
    x = x + g1 * mix
    h2 = _rmsnorm(x, p['norm2_w']) * (1.0 + sc2) + sh2
    a, b = jnp.split(h2 @ p['w_ffn_in'], 2, axis=-1)
    x = x + g2 * ((jax.nn.silu(a) * b) @ p['w_ffn_out'])
    return x, s_new, k_rows, v_rows, v_gm


def setup_inputs(seed: int = 0) -> dict:
    key = jax.random.key(seed)
    ks = jax.random.split(key, 32)

    def nrm(k, shape, scale):
        return jax.random.normal(k, shape, jnp.float32) * scale

    D = D_MODEL
    wb = min(WINDOW, PAST_LEN)
    return {
        'x_prompt': nrm(ks[0], (BATCH, SEQ, D), 1.0),
        'x_sample': nrm(ks[1], (DEC_BATCH, DEC_SEQ, D), 1.0),
        'c_prompt': nrm(ks[2], (BATCH, D), 1.0),
        'c_sample': nrm(ks[3], (DEC_BATCH, D), 1.0),
        'state_gla': nrm(ks[4], (DEPTH, DEC_BATCH, GLA_HEADS, GLA_DK, GLA_DV), 0.5),
        'cache_swa_k': nrm(ks[5], (DEPTH, DEC_BATCH, wb, SWA_HKV, SWA_HD), 1.0),
        'cache_swa_v': nrm(ks[6], (DEPTH, DEC_BATCH, wb, SWA_HKV, SWA_HD), 1.0),
        'w_ada': nrm(ks[7], (DEPTH, D, 6 * D), 0.5 * D ** -0.5),
        'b_ada': nrm(ks[8], (DEPTH, 6 * D), 0.01),
        'norm1_w': 1.0 + nrm(ks[9], (DEPTH, D), 0.02),
        'norm2_w': 1.0 + nrm(ks[10], (DEPTH, D), 0.02),
        'w_in': nrm(ks[11], (DEPTH, D, IN_WIDTH), D ** -0.5),
        'w_gk2': nrm(ks[12], (DEPTH, GLA_RANK, GLA_KEY), GLA_RANK ** -0.5),
        'b_gk': nrm(ks[13], (DEPTH, GLA_KEY), 0.01),
        'gla_norm_w': 1.0 + nrm(ks[14], (DEPTH, GLA_DV), 0.02),
        'gm_norm_w': 1.0 + nrm(ks[15], (DEPTH, GM_WIDTH), 0.02),
        'gm_norm_b': nrm(ks[16], (DEPTH, GM_WIDTH), 0.01),
        'gm_ws': nrm(ks[17], (DEPTH, GM_GROUPS, GM_CHUNK, GM_CHUNK), 0.5 * GM_CHUNK ** -0.5),
        'gm_bs': 1.0 + nrm(ks[18], (DEPTH, GM_GROUPS, GM_CHUNK), 0.02),
        'swa_sinks': nrm(ks[19], (DEPTH, SWA_HQ), 1.0),
        'w_pa': nrm(ks[20], (DEPTH, GLA_VAL, D), GLA_VAL ** -0.5),
        'w_pb': nrm(ks[21], (DEPTH, GM_WIDTH, D), GM_WIDTH ** -0.5),
        'w_pc': nrm(ks[22], (DEPTH, SWA_HQ * SWA_HD, D), (SWA_HQ * SWA_HD) ** -0.5),
        'w_o': nrm(ks[23], (DEPTH, D, D), D ** -0.5),
        'w_ffn_in': nrm(ks[24], (DEPTH, D, 2 * FFN_HIDDEN), D ** -0.5),
        'w_ffn_out': nrm(ks[25], (DEPTH, FFN_HIDDEN, D), FFN_HIDDEN ** -0.5),
        'final_norm_w': 1.0 + nrm(ks[26], (D,), 0.02),
    }


def reference(x_prompt, x_sample, c_prompt, c_sample, state_gla, cache_swa_k, cache_swa_v,
              w_ada, b_ada, norm1_w, norm2_w, w_in, w_gk2, b_gk, gla_norm_w, gm_norm_w, gm_norm_b,
              gm_ws, gm_bs, swa_sinks, w_pa, w_pb, w_pc, w_o, w_ffn_in, w_ffn_out, final_norm_w):
    xp, xs = x_prompt, x_sample
    gla_p, gla_s, kp, vp, ksm, vsm, gmv_s = [], [], [], [], [], [], []
    for l in range(DEPTH):
        p = {'w_ada': w_ada[l], 'b_ada': b_ada[l], 'norm1_w': norm1_w[l], 'norm2_w': norm2_w[l],
             'w_in': w_in[l], 'w_gk2': w_gk2[l], 'b_gk': b_gk[l], 'gla_norm_w': gla_norm_w[l],
             'gm_norm_w': gm_norm_w[l], 'gm_norm_b': gm_norm_b[l], 'gm_ws': gm_ws[l], 'gm_bs': gm_bs[l],
             'swa_sinks': swa_sinks[l], 'w_pa': w_pa[l], 'w_pb': w_pb[l], 'w_pc': w_pc[l], 'w_o': w_o[l],
             'w_ffn_in': w_ffn_in[l], 'w_ffn_out': w_ffn_out[l]}
        xp, s_p, k_p, v_p, _ = _layer(xp, c_prompt, p, None, None, None)
        xs, s_s, k_s, v_s, gv_s = _layer(xs, c_sample, p, state_gla[l], cache_swa_k[l], cache_swa_v[l])
        gla_p.append(s_p)
        gla_s.append(s_s)
        kp.append(k_p)
        vp.append(v_p)
        ksm.append(k_s)
        vsm.append(v_s)
        gmv_s.append(gv_s)
    y_prompt = _rmsnorm(xp, final_norm_w)
    y_sample = _rmsnorm(xs, final_norm_w)
    state_gla_prompt = jnp.stack(gla_p, axis=0)
    state_gla_sample = jnp.stack(gla_s, axis=0)
    cache_swa_k_prompt = jnp.stack(kp, axis=0)
    cache_swa_v_prompt = jnp.stack(vp, axis=0)
    cache_swa_k_sample = jnp.stack(ksm, axis=0)
    cache_swa_v_sample = jnp.stack(vsm, axis=0)
    state_gmlp_v_sample = jnp.stack(gmv_s, axis=0)
    return (y_prompt, y_sample, state_gla_prompt, state_gla_sample, cache_swa_k_prompt, cache_swa_v_prompt,
            cache_swa_k_sample, cache_swa_v_sample, state_gmlp_v_sample)
```

```python
import functools

import jax
import jax.numpy as jnp
import numpy as np
from jax import lax
from jax.experimental import pallas as pl
from jax.experimental.pallas import tpu as pltpu

F32 = jnp.float32
BF16 = jnp.bfloat16

D = 2048
BATCH, SEQ = 2, 4096
DEPTH = 2
DEC_BATCH = 128
GLA_H, GLA_DK, GLA_DV = 4, 256, 512
GLA_KEY, GLA_VAL = GLA_H * GLA_DK, GLA_H * GLA_DV
GLA_RANK = 16
GLA_CHUNK = 16
GM_WIDTH, GM_GROUPS, GM_CHUNK = 1024, 4, 128
GM_GW = GM_WIDTH // GM_GROUPS
SWA_HQ, SWA_HKV, SWA_HD, WINDOW = 16, 4, 64, 128
SWA_G = SWA_HQ // SWA_HKV
SWA_Q, SWA_KV = SWA_HQ * SWA_HD, SWA_HKV * SWA_HD
FFN_HIDDEN = 5632
EPS = 1e-6
NEG_BIG = -1e30

Z_QA, Z_KA, Z_VA, Z_GA = 0, 1024, 2048, 4096
Z_UB, Z_VB = 6144, 7168
Z_QC, Z_KC, Z_VC = 8192, 9216, 9472
Z_GATES = 9728
Z_WIDTH = 15872
LR_COL = 6144
LANE = 128
LR_PAD = LANE

MOD_SH1, MOD_SC1, MOD_G1, MOD_SH2, MOD_SC2, MOD_G2 = range(6)
MOD_ROWS = DEC_BATCH + 8

VMEM_LIMIT = 56 * 1024 * 1024


def _cparams(n_axes):
    return pltpu.CompilerParams(dimension_semantics=("arbitrary",) * n_axes,
                                vmem_limit_bytes=VMEM_LIMIT)


def _bf(x):
    return x if x.dtype == BF16 else x.astype(BF16)


def _dot(a, b):
    return jnp.dot(_bf(a), _bf(b), preferred_element_type=F32)


def _dot_nt(a, b):
    return lax.dot_general(_bf(a), _bf(b), (((1,), (1,)), ((), ())), preferred_element_type=F32)


def _silu(x):
    return x * (1.0 / (1.0 + jnp.exp(-x)))


def _sigmoid(x):
    return 1.0 / (1.0 + jnp.exp(-x))


def _gelu(x):
    return 0.5 * x * (1.0 + jnp.tanh(np.sqrt(2.0 / np.pi).astype(np.float32) * (x + 0.044715 * (x * x * x))))


class _Group:
    def __init__(self, rows, tm, rows_per_batch, mod_row0):
        self.rows, self.tm = rows, tm
        self.per_row = rows_per_batch is None
        self.blocks_per_batch = None if self.per_row else rows_per_batch // tm
        self.mod_row0 = mod_row0

    def mod_operand(self, mod):
        return mod if self.per_row else mod.reshape(DEPTH, MOD_ROWS, 1, 6 * D)

    def mod_spec(self, layer, chunk, tn):
        cb = chunk * D // tn
        if self.per_row:
            return pl.BlockSpec((None, self.tm, tn), lambda i, j: (layer, 0, j + cb))
        bpb, r0 = self.blocks_per_batch, self.mod_row0
        return pl.BlockSpec((None, None, 1, tn), lambda i, j: (layer, r0 + i // bpb, 0, j + cb))


PROMPT = _Group(BATCH * SEQ, 1024, SEQ, DEC_BATCH)
SAMPLE = _Group(DEC_BATCH, DEC_BATCH, None, 0)


def _ada_kernel(c_ref, w_ref, b_ref, o_ref):
    o_ref[...] = _dot(_silu(c_ref[...]), w_ref[...]) + b_ref[...]


def _ada(c_all, w_ada, b_ada):
    tn = 1024
    return pl.pallas_call(
        _ada_kernel,
        grid=(DEPTH, 6 * D // tn),
        in_specs=[pl.BlockSpec((MOD_ROWS, D), lambda l, j: (0, 0)),
                  pl.BlockSpec((None, D, tn), lambda l, j: (l, 0, j)),
                  pl.BlockSpec((None, 1, tn), lambda l, j: (l, 0, j))],
        out_specs=pl.BlockSpec((None, MOD_ROWS, tn), lambda l, j: (l, 0, j)),
        out_shape=jax.ShapeDtypeStruct((DEPTH, MOD_ROWS, 6 * D), F32),
        compiler_params=_cparams(2),
        name="ada",
    )(c_all, w_ada, b_ada.reshape(DEPTH, 1, 6 * D))


def _prep_kernel(x_ref, nw_ref, sc_ref, sh_ref, o_ref):
    x = x_ref[...]
    y = x * lax.rsqrt(jnp.mean(x * x, axis=-1, keepdims=True) + EPS) * nw_ref[...]
    o_ref[...] = (y * (1.0 + sc_ref[...]) + sh_ref[...]).astype(o_ref.dtype)


def _prep(grp, layer, x, norm_w, mod, sc_chunk, sh_chunk):
    tm = min(grp.tm, 512)
    g = _Group(grp.rows, tm, None if grp.per_row else SEQ, grp.mod_row0)
    modop = g.mod_operand(mod)
    return pl.pallas_call(
        _prep_kernel,
        grid=(grp.rows // tm, 1),
        in_specs=[pl.BlockSpec((tm, D), lambda i, j: (i, 0)),
                  pl.BlockSpec((None, 1, D), lambda i, j: (layer, 0, 0)),
                  g.mod_spec(layer, sc_chunk, D),
                  g.mod_spec(layer, sh_chunk, D)],
        out_specs=pl.BlockSpec((tm, D), lambda i, j: (i, 0)),
        out_shape=jax.ShapeDtypeStruct((grp.rows, D), BF16),
        compiler_params=_cparams(2),
        name="prep",
    )(x, norm_w.reshape(DEPTH, 1, D), modop, modop)


def _final_norm_kernel(x_ref, nw_ref, o_ref):
    x = x_ref[...]
    o_ref[...] = x * lax.rsqrt(jnp.mean(x * x, axis=-1, keepdims=True) + EPS) * nw_ref[...]


def _final_norm(x, w):
    rows = x.shape[0]
    tm = min(rows, 512)
    return pl.pallas_call(
        _final_norm_kernel,
        grid=(rows // tm,),
        in_specs=[pl.BlockSpec((tm, D), lambda i: (i, 0)), pl.BlockSpec((1, D), lambda i: (0, 0))],
        out_specs=pl.BlockSpec((tm, D), lambda i: (i, 0)),
        out_shape=jax.ShapeDtypeStruct((rows, D), F32),
        compiler_params=_cparams(1),
        name="final_norm",
    )(x, w.reshape(1, D))


def _mm_kernel(*refs, n_a, term_a, n_extra, epilogue):
    a_refs = refs[:n_a]
    w_refs = refs[n_a:n_a + len(term_a)]
    e_refs = refs[n_a + len(term_a):n_a + len(term_a) + n_extra]
    o_refs = refs[n_a + len(term_a) + n_extra:]
    a_vals = [_bf(a[...]) for a in a_refs]
    dots = [jnp.dot(a_vals[ai], _bf(w[...]), preferred_element_type=F32) for ai, w in zip(term_a, w_refs)]
    outs = epilogue(dots, [e[...] for e in e_refs])
    for o_ref, o in zip(o_refs, outs):
        o_ref[...] = o.astype(o_ref.dtype)


def _fused_matmul(name, grp, layer, a_list, terms, extras, epilogue, outs, n_cols, tn, tm=None):
    tm = tm or grp.tm
    g = grp if tm == grp.tm else _Group(grp.rows, tm, None if grp.per_row else SEQ, grp.mod_row0)
    grid = (grp.rows // tm, n_cols // tn)
    args, in_specs = [], []
    for a in a_list:
        args.append(a)
        in_specs.append(pl.BlockSpec((tm, a.shape[1]), lambda i, j: (i, 0)))
    for ai, w, col0 in terms:
        assert col0 % tn == 0 and w.shape[-2] == a_list[ai].shape[1]
        args.append(w)
        in_specs.append(pl.BlockSpec((None, w.shape[-2], tn), lambda i, j, cb=col0 // tn: (layer, 0, j + cb)))
    for ex in extras:
        if ex[0] == "tile":
            _, arr, col0 = ex
            assert col0 % tn == 0
            args.append(arr)
            in_specs.append(pl.BlockSpec((tm, tn), lambda i, j, cb=col0 // tn: (i, j + cb)))
        else:
            _, mod, chunk = ex
            args.append(g.mod_operand(mod))
            in_specs.append(g.mod_spec(layer, chunk, tn))
    kern = functools.partial(_mm_kernel, n_a=len(a_list), term_a=tuple(t[0] for t in terms),
                             n_extra=len(extras), epilogue=epilogue)
    res = pl.pallas_call(
        kern,
        grid=grid,
        in_specs=in_specs,
        out_specs=[pl.BlockSpec((tm, tn), lambda i, j: (i, j)) for _ in outs],
        out_shape=[jax.ShapeDtypeStruct((grp.rows, n_cols), dt) for dt in outs],
        compiler_params=_cparams(2),
        name=name,
    )(*args)
    return res


def _epi_plain(dots, ex):
    return [dots[0]]


def _epi_merge(dots, ex):
    return [_sigmoid(ex[0]) * dots[0] + _sigmoid(ex[1]) * dots[1] + _sigmoid(ex[2]) * dots[2]]


def _epi_residual(dots, ex):
    return [ex[0] + ex[1] * dots[0]]


def _epi_swiglu(dots, ex):
    return [_silu(dots[0]) * dots[1]]


def _log_sigmoid(u):
    return -(jnp.maximum(-u, 0.0) + jnp.log1p(jnp.exp(-jnp.abs(u))))


def _gla_prompt_kernel(q_ref, k_ref, v_ref, ga_ref, lr_ref, wgk_ref, bgk_ref, nw_ref,
                       oa_ref, sfin_ref, st_s, b_s, kd_s, vt_s, o_s, *, tb):
    t = pl.program_id(1)
    nchunk = tb // GLA_CHUNK

    @pl.when(t == 0)
    def _():
        st_s[...] = jnp.zeros_like(st_s)

    u = _dot(lr_ref[...], wgk_ref[...]) + bgk_ref[...]
    gk = _log_sigmoid(u) * (1.0 / 16.0)
    pos = lax.broadcasted_iota(jnp.int32, gk.shape, 0) % GLA_CHUNK
    b = gk
    for s in (1, 2, 4, 8):
        b = b + jnp.where(pos >= s, pltpu.roll(b, s, 0), 0.0)
    bl = jnp.where(pos == GLA_CHUNK - 1, b, 0.0)
    for s in (1, 2, 4, 8):
        bl = bl + pltpu.roll(bl, tb - s, 0)
    b_s[...] = b
    q = q_ref[...]
    k = k_ref[...]
    qin = q * jnp.exp(b) * (GLA_DK ** -0.5)
    kout = k * jnp.exp(-b)
    kd_s[...] = k * jnp.exp(bl - b)
    vt_s[...] = v_ref[...].T

    ri = lax.broadcasted_iota(jnp.int32, (tb, tb), 0)
    ci = lax.broadcasted_iota(jnp.int32, (tb, tb), 1)
    intra = (ri // GLA_CHUNK == ci // GLA_CHUNK) & (ci <= ri)
    for h in range(GLA_H):
        ks = slice(h * GLA_DK, (h + 1) * GLA_DK)
        vs = slice(h * GLA_DV, (h + 1) * GLA_DV)
        a = jnp.where(intra, _dot_nt(qin[:, ks], kout[:, ks]), 0.0)
        o_s[:, vs] = _dot(a, v_ref[:, vs])

    rowid = lax.broadcasted_iota(jnp.int32, (tb, GLA_DK), 0) // GLA_CHUNK

    def chunk_step(c, carry):
        r0 = pl.multiple_of(c * GLA_CHUNK, GLA_CHUNK)
        b_c = b_s[pl.ds(r0, GLA_CHUNK), :]
        qin_c = q_ref[pl.ds(r0, GLA_CHUNK), :] * jnp.exp(b_c) * (GLA_DK ** -0.5)
        ebl = jnp.exp(b_c[GLA_CHUNK - 1:GLA_CHUNK, :])
        for h in range(GLA_H):
            ks = slice(h * GLA_DK, (h + 1) * GLA_DK)
            vs = slice(h * GLA_DV, (h + 1) * GLA_DV)
            st = st_s[h]
            o_s[pl.ds(r0, GLA_CHUNK), vs] += _dot_nt(qin_c[:, ks], st)
            kd_c = jnp.where(rowid == c, kd_s[:, ks], 0.0)
            st_s[h] = st * ebl[:, ks] + _dot(vt_s[vs, :], kd_c)
        return carry

    lax.fori_loop(0, nchunk, chunk_step, 0)

    nw = nw_ref[...]
    for h in range(GLA_H):
        vs = slice(h * GLA_DV, (h + 1) * GLA_DV)
        o = o_s[:, vs]
        y = o * lax.rsqrt(jnp.mean(o * o, axis=-1, keepdims=True) + EPS) * nw
        oa_ref[:, vs] = (y * _silu(ga_ref[:, vs])).astype(oa_ref.dtype)

    @pl.when(t == pl.num_programs(1) - 1)
    def _():
        for h in range(GLA_H):
            sfin_ref[h] = st_s[h].T


def _gla_prompt(layer, z, lr, wgk_pad, b_gk, gla_norm_w):
    tb = 256
    nt = SEQ // tb
    row = lambda b, t: b * nt + t
    kern = functools.partial(_gla_prompt_kernel, tb=tb)
    return pl.pallas_call(
        kern,
        grid=(BATCH, nt),
        in_specs=[pl.BlockSpec((tb, GLA_KEY), lambda b, t: (row(b, t), Z_QA // GLA_KEY)),
                  pl.BlockSpec((tb, GLA_KEY), lambda b, t: (row(b, t), Z_KA // GLA_KEY)),
                  pl.BlockSpec((tb, GLA_VAL), lambda b, t: (row(b, t), Z_VA // GLA_VAL)),
                  pl.BlockSpec((tb, GLA_VAL), lambda b, t: (row(b, t), Z_GA // GLA_VAL)),
                  pl.BlockSpec((tb, LR_PAD), lambda b, t: (row(b, t), 0)),
                  pl.BlockSpec((None, LR_PAD, GLA_KEY), lambda b, t: (layer, 0, 0)),
                  pl.BlockSpec((None, 1, GLA_KEY), lambda b, t: (layer, 0, 0)),
                  pl.BlockSpec((None, 1, GLA_DV), lambda b, t: (layer, 0, 0))],
        out_specs=[pl.BlockSpec((tb, GLA_VAL), lambda b, t: (row(b, t), 0)),
                   pl.BlockSpec((None, GLA_H, GLA_DK, GLA_DV), lambda b, t: (b, 0, 0, 0))],
        out_shape=[jax.ShapeDtypeStruct((BATCH * SEQ, GLA_VAL), BF16),
                   jax.ShapeDtypeStruct((BATCH, GLA_H, GLA_DK, GLA_DV), F32)],
        scratch_shapes=[pltpu.VMEM((GLA_H, GLA_DV, GLA_DK), F32),
                        pltpu.VMEM((tb, GLA_KEY), F32),
                        pltpu.VMEM((tb, GLA_KEY), F32),
                        pltpu.VMEM((GLA_VAL, tb), F32),
                        pltpu.VMEM((tb, GLA_VAL), F32)],
        compiler_params=_cparams(2),
        name="gla_prompt",
    )(z, z, z, z, lr, wgk_pad, b_gk.reshape(DEPTH, 1, GLA_KEY), gla_norm_w.reshape(DEPTH, 1, GLA_DV))


def _gla_sample_kernel(q_ref, k_ref, v_ref, ga_ref, lr_ref, wgk_ref, bgk_ref, nw_ref, s_ref,
                       oa_ref, snew_ref, x_s, *, rb):
    u = _dot(lr_ref[...], wgk_ref[...]) + bgk_ref[...]
    eg = jnp.exp(_log_sigmoid(u) * (1.0 / 16.0))
    x_s[...] = jnp.zeros_like(x_s)
    x_s[0:rb, :] = q_ref[...] * (GLA_DK ** -0.5)
    x_s[rb:2 * rb, :] = k_ref[...]
    x_s[2 * rb:3 * rb, :] = eg
    xt = x_s[...].T
    nw = nw_ref[...]
    for r in range(rb):
        qc = xt[:, r:r + 1]
        kc = xt[:, rb + r:rb + r + 1]
        gc = xt[:, 2 * rb + r:2 * rb + r + 1]
        s_new = gc * s_ref[r, 0] + kc * v_ref[r:r + 1, :]
        snew_ref[r, 0] = s_new
        o = jnp.sum(qc * s_new, axis=0, keepdims=True)
        y = o * lax.rsqrt(jnp.mean(o * o, axis=-1, keepdims=True) + EPS) * nw
        oa_ref[r:r + 1, :] = y * _silu(ga_ref[r:r + 1, :])


def _gla_sample(layer, z, lr, wgk_pad, b_gk, gla_norm_w, state_gla, state_out):
    rb = 8
    kern = functools.partial(_gla_sample_kernel, rb=rb)
    in_specs = [pl.BlockSpec((rb, GLA_DK), lambda i, h: (i, Z_QA // GLA_DK + h)),
                pl.BlockSpec((rb, GLA_DK), lambda i, h: (i, Z_KA // GLA_DK + h)),
                pl.BlockSpec((rb, GLA_DV), lambda i, h: (i, Z_VA // GLA_DV + h)),
                pl.BlockSpec((rb, GLA_DV), lambda i, h: (i, Z_GA // GLA_DV + h)),
                pl.BlockSpec((rb, LR_PAD), lambda i, h: (i, 0)),
                pl.BlockSpec((None, LR_PAD, GLA_DK), lambda i, h: (layer, 0, h)),
                pl.BlockSpec((None, 1, GLA_DK), lambda i, h: (layer, 0, h)),
                pl.BlockSpec((None, 1, GLA_DV), lambda i, h: (layer, 0, 0)),
                pl.BlockSpec((None, rb, 1, GLA_DK, GLA_DV), lambda i, h: (layer, i, h, 0, 0))]
    args = [z, z, z, z, lr, wgk_pad, b_gk.reshape(DEPTH, 1, GLA_KEY), gla_norm_w.reshape(DEPTH, 1, GLA_DV),
            state_gla]
    aliases = {}
    if state_out is not None:
        in_specs.append(pl.BlockSpec(memory_space=pl.ANY))
        args.append(state_out)
        aliases = {len(args) - 1: 1}
        kern = functools.partial(_drop_last_input, kern, n_in=len(args))
    return pl.pallas_call(
        kern,
        grid=(DEC_BATCH // rb, GLA_H),
        in_specs=in_specs,
        out_specs=[pl.BlockSpec((rb, GLA_DV), lambda i, h: (i, h)),
                   pl.BlockSpec((None, rb, 1, GLA_DK, GLA_DV), lambda i, h: (layer, i, h, 0, 0))],
        out_shape=[jax.ShapeDtypeStruct((DEC_BATCH, GLA_VAL), F32),
                   jax.ShapeDtypeStruct((DEPTH, DEC_BATCH, GLA_H, GLA_DK, GLA_DV), F32)],
        scratch_shapes=[pltpu.VMEM((LANE, GLA_DK), F32)],
        input_output_aliases=aliases,
        compiler_params=_cparams(2),
        name="gla_sample",
    )(*args)


def _drop_last_input(kern, *refs, n_in):
    return kern(*refs[:n_in - 1], *refs[n_in:])


def _layernorm(x, w, b):
    mu = jnp.mean(x, axis=-1, keepdims=True)
    xc = x - mu
    var = jnp.mean(xc * xc, axis=-1, keepdims=True)
    return xc * lax.rsqrt(var + EPS) * w + b


def _gmlp_prompt_kernel(u_ref, v_ref, ws_ref, bst_ref, nw_ref, nb_ref, ob_ref, *, nsub):
    ri = lax.broadcasted_iota(jnp.int32, (GM_CHUNK, GM_CHUNK), 0)
    ci = lax.broadcasted_iota(jnp.int32, (GM_CHUNK, GM_CHUNK), 1)
    tril = ci <= ri
    for s in range(nsub):
        rs = slice(s * GM_CHUNK, (s + 1) * GM_CHUNK)
        u = _gelu(u_ref[rs, :])
        v = _layernorm(_gelu(v_ref[rs, :]), nw_ref[...], nb_ref[...])
        for g in range(GM_GROUPS):
            cs = slice(g * GM_GW, (g + 1) * GM_GW)
            wm = jnp.where(tril, ws_ref[g], 0.0)
            mixed = _dot(wm, v[:, cs]) + bst_ref[:, g:g + 1]
            ob_ref[rs, cs] = (u[:, cs] * mixed).astype(ob_ref.dtype)


def _gmlp_prompt(layer, z, gm_ws, gm_bs_t, gm_norm_w, gm_norm_b):
    nsub = 4
    tb = nsub * GM_CHUNK
    kern = functools.partial(_gmlp_prompt_kernel, nsub=nsub)
    return pl.pallas_call(
        kern,
        grid=(BATCH * SEQ // tb,),
        in_specs=[pl.BlockSpec((tb, GM_WIDTH), lambda i: (i, Z_UB // GM_WIDTH)),
                  pl.BlockSpec((tb, GM_WIDTH), lambda i: (i, Z_VB // GM_WIDTH)),
                  pl.BlockSpec((None, GM_GROUPS, GM_CHUNK, GM_CHUNK), lambda i: (layer, 0, 0, 0)),
                  pl.BlockSpec((None, GM_CHUNK, GM_GROUPS), lambda i: (layer, 0, 0)),
                  pl.BlockSpec((None, 1, GM_WIDTH), lambda i: (layer, 0, 0)),
                  pl.BlockSpec((None, 1, GM_WIDTH), lambda i: (layer, 0, 0))],
        out_specs=pl.BlockSpec((tb, GM_WIDTH), lambda i: (i, 0)),
        out_shape=jax.ShapeDtypeStruct((BATCH * SEQ, GM_WIDTH), BF16),
        compiler_params=_cparams(1),
        name="gmlp_prompt",
    )(z, z, gm_ws, gm_bs_t, gm_norm_w.reshape(DEPTH, 1, GM_WIDTH), gm_norm_b.reshape(DEPTH, 1, GM_WIDTH))


def _gmlp_sample_kernel(u_ref, v_ref, w0_ref, b0_ref, nw_ref, nb_ref, ob_ref, vn_ref):
    u = _gelu(u_ref[...])
    v = _layernorm(_gelu(v_ref[...]), nw_ref[...], nb_ref[...])
    vn_ref[...] = v
    ob_ref[...] = u * (w0_ref[...] * v + b0_ref[...])


def _gmlp_sample(layer, z, w0_row, b0_row, gm_norm_w, gm_norm_b):
    full = lambda i: (0, 0)
    lrow = lambda i: (layer, 0, 0)
    return pl.pallas_call(
        _gmlp_sample_kernel,
        grid=(1,),
        in_specs=[pl.BlockSpec((DEC_BATCH, GM_WIDTH), lambda i: (0, Z_UB // GM_WIDTH)),
                  pl.BlockSpec((DEC_BATCH, GM_WIDTH), lambda i: (0, Z_VB // GM_WIDTH)),
                  pl.BlockSpec((None, 1, GM_WIDTH), lrow),
                  pl.BlockSpec((None, 1, GM_WIDTH), lrow),
                  pl.BlockSpec((None, 1, GM_WIDTH), lrow),
                  pl.BlockSpec((None, 1, GM_WIDTH), lrow)],
        out_specs=[pl.BlockSpec((DEC_BATCH, GM_WIDTH), full), pl.BlockSpec((DEC_BATCH, GM_WIDTH), full)],
        out_shape=[jax.ShapeDtypeStruct((DEC_BATCH, GM_WIDTH), F32),
                   jax.ShapeDtypeStruct((DEC_BATCH, GM_WIDTH), F32)],
        compiler_params=_cparams(1),
        name="gmlp_sample",
    )(z, z, w0_row, b0_row, gm_norm_w.reshape(DEPTH, 1, GM_WIDTH), gm_norm_b.reshape(DEPTH, 1, GM_WIDTH))


def _alibi_slope(h):
    return float(2.0 ** (-8.0 * (h + 1) / SWA_HQ))


def _swa_prompt_kernel(q_ref, kc_ref, kp_ref, vc_ref, vp_ref, sink_ref, oc_ref):
    n = pl.program_id(1)
    w = WINDOW
    ri = lax.broadcasted_iota(jnp.int32, (w, w), 0)
    ci = lax.broadcasted_iota(jnp.int32, (w, w), 1)
    lane = lax.broadcasted_iota(jnp.int32, (w, LANE), 1)
    dist_cur = (ri - ci).astype(F32)
    dist_prev = (w + ri - ci).astype(F32)
    ok_cur = ci <= ri
    ok_prev = (ci > ri) & (n > 0)
    for t in range(SWA_KV // LANE):
        kc_t, kp_t = kc_ref[:, t * LANE:(t + 1) * LANE], kp_ref[:, t * LANE:(t + 1) * LANE]
        vc_t, vp_t = vc_ref[:, t * LANE:(t + 1) * LANE], vp_ref[:, t * LANE:(t + 1) * LANE]
        for half in range(2):
            kv = 2 * t + half
            in_half = (lane >= half * SWA_HD) & (lane < (half + 1) * SWA_HD)
            kca = jnp.where(in_half, kc_t, 0.0)
            kpa = jnp.where(in_half, kp_t, 0.0)
            kcb = pltpu.roll(kca, SWA_HD, 1)
            kpb = pltpu.roll(kpa, SWA_HD, 1)
            for g in range(SWA_G):
                h = kv * SWA_G + g
                qt = q_ref[:, (h // 2) * LANE:(h // 2 + 1) * LANE]
                same = (h % 2) == half
                kc_h, kp_h = (kca, kpa) if same else (kcb, kpb)
                slope = _alibi_slope(h)
                s_cur = _dot_nt(qt, kc_h) * (SWA_HD ** -0.5) - slope * dist_cur
                s_prev = _dot_nt(qt, kp_h) * (SWA_HD ** -0.5) - slope * dist_prev
                s_cur = jnp.where(ok_cur, s_cur, NEG_BIG)
                s_prev = jnp.where(ok_prev, s_prev, NEG_BIG)
                sink = sink_ref[h:h + 1, 0:1]
                m = jnp.maximum(jnp.maximum(jnp.max(s_cur, axis=-1, keepdims=True),
                                            jnp.max(s_prev, axis=-1, keepdims=True)), sink)
                p_cur = jnp.exp(s_cur - m)
                p_prev = jnp.exp(s_prev - m)
                den = (jnp.sum(p_cur, axis=-1, keepdims=True) + jnp.sum(p_prev, axis=-1, keepdims=True)
                       + jnp.exp(sink - m))
                inv = 1.0 / den
                o = _dot(p_cur * inv, vc_t) + _dot(p_prev * inv, vp_t)
                if not same:
                    o = pltpu.roll(o, SWA_HD, 1)
                q_half = (lane >= (h % 2) * SWA_HD) & (lane < (h % 2 + 1) * SWA_HD)
                if h % 2 == 0:
                    acc = jnp.where(q_half, o, 0.0)
                else:
                    oc_ref[:, (h // 2) * LANE:(h // 2 + 1) * LANE] = jnp.where(q_half, o, acc).astype(oc_ref.dtype)


def _swa_prompt(layer, z, sinks_b):
    nb = SEQ // WINDOW
    row = lambda b, n: b * nb + n
    prev = lambda b, n: b * nb + jnp.maximum(n - 1, 0)
    return pl.pallas_call(
        _swa_prompt_kernel,
        grid=(BATCH, nb),
        in_specs=[pl.BlockSpec((WINDOW, SWA_Q), lambda b, n: (row(b, n), Z_QC // SWA_Q)),
                  pl.BlockSpec((WINDOW, SWA_KV), lambda b, n: (row(b, n), Z_KC // SWA_KV)),
                  pl.BlockSpec((WINDOW, SWA_KV), lambda b, n: (prev(b, n), Z_KC // SWA_KV)),
                  pl.BlockSpec((WINDOW, SWA_KV), lambda b, n: (row(b, n), Z_VC // SWA_KV)),
                  pl.BlockSpec((WINDOW, SWA_KV), lambda b, n: (prev(b, n), Z_VC // SWA_KV)),
                  pl.BlockSpec((None, SWA_HQ, LANE), lambda b, n: (layer, 0, 0))],
        out_specs=pl.BlockSpec((WINDOW, SWA_Q), lambda b, n: (row(b, n), 0)),
        out_shape=jax.ShapeDtypeStruct((BATCH * SEQ, SWA_Q), BF16),
        compiler_params=_cparams(2),
        name="swa_prompt",
    )(z, z, z, z, z, sinks_b)


def _swa_sample_kernel(qm_ref, kn_ref, vn_ref, kb_ref, vb_ref, sink_ref, slope_ref, om_ref, *, rb):
    wb = WINDOW
    j = lax.broadcasted_iota(jnp.int32, (SWA_HQ, wb), 1)
    dist = (wb - j).astype(F32)
    ok = j >= 1
    slope = slope_ref[:, 0:1]
    sink = sink_ref[:, 0:1]
    for r in range(rb):
        qm = qm_ref[r]
        s = _dot_nt(qm, kb_ref[r]) * (SWA_HD ** -0.5) - slope * dist
        s = jnp.where(ok, s, NEG_BIG)
        s_self = jnp.sum(_bf(qm).astype(F32) * _bf(kn_ref[r:r + 1, :]).astype(F32), axis=-1,
                         keepdims=True) * (SWA_HD ** -0.5)
        m = jnp.maximum(jnp.maximum(jnp.max(s, axis=-1, keepdims=True), s_self), sink)
        p = jnp.exp(s - m)
        p_self = jnp.exp(s_self - m)
        inv = 1.0 / (jnp.sum(p, axis=-1, keepdims=True) + p_self + jnp.exp(sink - m))
        o = _dot(p * inv, vb_ref[r]) + _bf(p_self * inv).astype(F32) * _bf(vn_ref[r:r + 1, :]).astype(F32)
        om_ref[r] = o


def _swa_sample(layer, qm, z, cache_k, cache_v, sinks_b, slopes_b):
    rb = 8
    kern = functools.partial(_swa_sample_kernel, rb=rb)
    return pl.pallas_call(
        kern,
        grid=(DEC_BATCH // rb,),
        in_specs=[pl.BlockSpec((rb, SWA_HQ, SWA_KV), lambda i: (i, 0, 0)),
                  pl.BlockSpec((rb, SWA_KV), lambda i: (i, Z_KC // SWA_KV)),
                  pl.BlockSpec((rb, SWA_KV), lambda i: (i, Z_VC // SWA_KV)),
                  pl.BlockSpec((None, rb, WINDOW, SWA_KV), lambda i: (layer, i, 0, 0)),
                  pl.BlockSpec((None, rb, WINDOW, SWA_KV), lambda i: (layer, i, 0, 0)),
                  pl.BlockSpec((None, SWA_HQ, LANE), lambda i: (layer, 0, 0)),
                  pl.BlockSpec((SWA_HQ, LANE), lambda i: (0, 0))],
        out_specs=pl.BlockSpec((rb, SWA_HQ, SWA_KV), lambda i: (i, 0, 0)),
        out_shape=jax.ShapeDtypeStruct((DEC_BATCH, SWA_HQ, SWA_KV), F32),
        compiler_params=_cparams(1),
        name="swa_sample",
    )(qm, z, z, cache_k, cache_v, sinks_b, slopes_b)


def _layer(grp, layer, x, mod, p, state_gla=None, cache_k=None, cache_v=None, state_out=None):
    sample = grp.per_row
    h = _prep(grp, layer, x, p["norm1_w"], mod, MOD_SC1, MOD_SH1)
    tn_in = 512
    (z,) = _fused_matmul("w_in", grp, layer, [h], [(0, p["w_in_r"], 0)], [], _epi_plain, [F32], Z_WIDTH, tn_in)
    (lr,) = _fused_matmul("w_lr", grp, layer, [h], [(0, p["w_lr"], 0)], [], _epi_plain, [F32], LR_PAD, LR_PAD)
    if not sample:
        oa, s_new = _gla_prompt(layer, z, lr, p["wgk_pad"], p["b_gk"], p["gla_norm_w"])
        ob = _gmlp_prompt(layer, z, p["gm_ws"], p["gm_bs_t"], p["gm_norm_w"], p["gm_norm_b"])
        v_gm = None
        oc = _swa_prompt(layer, z, p["sinks_b"])
        z4 = z.reshape(BATCH, SEQ, Z_WIDTH)
        k_rows = z4[:, SEQ - WINDOW:, Z_KC:Z_KC + SWA_KV].reshape(BATCH, WINDOW, SWA_HKV, SWA_HD)
        v_rows = z4[:, SEQ - WINDOW:, Z_VC:Z_VC + SWA_KV].reshape(BATCH, WINDOW, SWA_HKV, SWA_HD)
    else:
        oa, s_new = _gla_sample(layer, z, lr, p["wgk_pad"], p["b_gk"], p["gla_norm_w"], state_gla, state_out)
        ob, v_gm = _gmlp_sample(layer, z, p["gm_w0"], p["gm_b0"], p["gm_norm_w"], p["gm_norm_b"])
        q4 = z[:, Z_QC:Z_QC + SWA_Q].reshape(DEC_BATCH, SWA_HKV, SWA_G, 1, SWA_HD)
        eye = jnp.eye(SWA_HKV, dtype=F32).reshape(SWA_HKV, 1, SWA_HKV, 1)
        qm = (q4 * eye[None]).reshape(DEC_BATCH, SWA_HQ, SWA_KV)
        om = _swa_sample(layer, qm, z, cache_k, cache_v, p["sinks_b"], p["slopes_b"])
        om5 = om.reshape(DEC_BATCH, SWA_HKV, SWA_G, SWA_HKV, SWA_HD)
        oc = jnp.sum(om5 * eye[None], axis=3).reshape(DEC_BATCH, SWA_Q)
        k_rows = z[:, Z_KC:Z_KC + SWA_KV].reshape(DEC_BATCH, 1, SWA_HKV, SWA_HD)
        v_rows = z[:, Z_VC:Z_VC + SWA_KV].reshape(DEC_BATCH, 1, SWA_HKV, SWA_HD)
    (merged,) = _fused_matmul(
        "merge", grp, layer, [oa, ob, oc],
        [(0, p["w_pa"], 0), (1, p["w_pb"], 0), (2, p["w_pc"], 0)],
        [("tile", z, Z_GATES), ("tile", z, Z_GATES + D), ("tile", z, Z_GATES + 2 * D)],
        _epi_merge, [BF16], D, 256)
    (x1,) = _fused_matmul("w_o", grp, layer, [merged], [(0, p["w_o"], 0)],
                          [("tile", x, 0), ("mod", mod, MOD_G1)], _epi_residual, [F32], D, 512)
    h2 = _prep(grp, layer, x1, p["norm2_w"], mod, MOD_SC2, MOD_SH2)
    (hid,) = _fused_matmul("ffn_in", grp, layer, [h2], [(0, p["w_ffn_in"], 0), (0, p["w_ffn_in"], FFN_HIDDEN)],
                           [], _epi_swiglu, [BF16], FFN_HIDDEN, 512)
    (x2,) = _fused_matmul("ffn_out", grp, layer, [hid], [(0, p["w_ffn_out"], 0)],
                          [("tile", x1, 0), ("mod", mod, MOD_G2)], _epi_residual, [F32], D, 256)
    return x2, s_new, k_rows, v_rows, v_gm


def kernel(x_prompt, x_sample, c_prompt, c_sample, state_gla, cache_swa_k, cache_swa_v, w_ada, b_ada, norm1_w,
           norm2_w, w_in, w_gk2, b_gk, gla_norm_w, gm_norm_w, gm_norm_b, gm_ws, gm_bs, swa_sinks, w_pa, w_pb,
           w_pc, w_o, w_ffn_in, w_ffn_out, final_norm_w):
    w_in_r = jnp.concatenate([w_in[:, :, :LR_COL], w_in[:, :, LR_COL + GLA_RANK:]], axis=2).astype(BF16)
    w_lr = jnp.pad(w_in[:, :, LR_COL:LR_COL + GLA_RANK], ((0, 0), (0, 0), (0, LR_PAD - GLA_RANK))).astype(BF16)
    p = {
        "norm1_w": norm1_w, "norm2_w": norm2_w, "w_in_r": w_in_r, "w_lr": w_lr,
        "wgk_pad": jnp.pad(w_gk2, ((0, 0), (0, LR_PAD - GLA_RANK), (0, 0))),
        "b_gk": b_gk, "gla_norm_w": gla_norm_w, "gm_norm_w": gm_norm_w, "gm_norm_b": gm_norm_b,
        "gm_ws": gm_ws, "gm_bs_t": jnp.swapaxes(gm_bs, 1, 2),
        "gm_w0": jnp.repeat(gm_ws[:, :, 0, 0], GM_GW, axis=1).reshape(DEPTH, 1, GM_WIDTH),
        "gm_b0": jnp.repeat(gm_bs[:, :, 0], GM_GW, axis=1).reshape(DEPTH, 1, GM_WIDTH),
        "sinks_b": jnp.broadcast_to(swa_sinks[:, :, None], (DEPTH, SWA_HQ, LANE)),
        "slopes_b": jnp.broadcast_to(
            jnp.asarray([_alibi_slope(h) for h in range(SWA_HQ)], F32)[:, None], (SWA_HQ, LANE)),
        "w_pa": w_pa, "w_pb": w_pb, "w_pc": w_pc, "w_o": w_o, "w_ffn_in": w_ffn_in, "w_ffn_out": w_ffn_out,
    }
    c_all = jnp.concatenate([c_sample, c_prompt, jnp.zeros((MOD_ROWS - DEC_BATCH - BATCH, D), F32)], axis=0)
    mod = _ada(c_all, w_ada, b_ada)

    xp = x_prompt.reshape(BATCH * SEQ, D)
    xs = x_sample.reshape(DEC_BATCH, D)
    cache_k = cache_swa_k.reshape(DEPTH, DEC_BATCH, WINDOW, SWA_KV)
    cache_v = cache_swa_v.reshape(DEPTH, DEC_BATCH, WINDOW, SWA_KV)
    gla_p, kp, vp, ksm, vsm, gmv = [], [], [], [], [], []
    state_out = None
    for l in range(DEPTH):
        xp, s_p, k_p, v_p, _ = _layer(PROMPT, l, xp, mod, p)
        xs, state_out, k_s, v_s, gv = _layer(SAMPLE, l, xs, mod, p, state_gla, cache_k, cache_v, state_out)
        gla_p.append(s_p)
        kp.append(k_p)
        vp.append(v_p)
        ksm.append(k_s)
        vsm.append(v_s)
        gmv.append(gv.reshape(DEC_BATCH, 1, GM_WIDTH))
    y_prompt = _final_norm(xp, final_norm_w).reshape(BATCH, SEQ, D)
    y_sample = _final_norm(xs, final_norm_w).reshape(DEC_BATCH, 1, D)
    return (y_prompt, y_sample, jnp.stack(gla_p), state_out, jnp.stack(kp), jnp.stack(vp),
            jnp.stack(ksm), jnp.stack(vsm), jnp.stack(gmv))
```

```python
import functools

import jax
import jax.numpy as jnp
import numpy as np
from jax import lax
from jax.experimental import pallas as pl
from jax.experimental.pallas import tpu as pltpu

F32 = jnp.float32
BF16 = jnp.bfloat16

D = 2048
BATCH, SEQ = 2, 4096
DEPTH = 2
DEC_BATCH = 128
GLA_H, GLA_DK, GLA_DV = 4, 256, 512
GLA_KEY, GLA_VAL = GLA_H * GLA_DK, GLA_H * GLA_DV
GLA_RANK = 16
GLA_CHUNK = 16
GLA_SC = 128
GM_WIDTH, GM_GROUPS, GM_CHUNK = 1024, 4, 128
GM_GW = GM_WIDTH // GM_GROUPS
SWA_HQ, SWA_HKV, SWA_HD, WINDOW = 16, 4, 64, 128
SWA_G = SWA_HQ // SWA_HKV
SWA_Q, SWA_KV = SWA_HQ * SWA_HD, SWA_HKV * SWA_HD
FFN_HIDDEN = 5632
EPS = 1e-6
NEG_BIG = -1e30

Z_QA, Z_KA, Z_VA, Z_GA = 0, 1024, 2048, 4096
Z_UB, Z_VB = 6144, 7168
Z_QC, Z_KC, Z_VC = 8192, 9216, 9472
Z_GATES = 9728
Z_WIDTH = 15872
LR_COL = 6144
LANE = 128
LR_PAD = LANE

MOD_SH1, MOD_SC1, MOD_G1, MOD_SH2, MOD_SC2, MOD_G2 = range(6)
MOD_ROWS = DEC_BATCH + 8

VMEM_LIMIT = 56 * 1024 * 1024


def _cparams(n_axes):
    return pltpu.CompilerParams(dimension_semantics=("arbitrary",) * n_axes,
                                vmem_limit_bytes=VMEM_LIMIT)


def _bf(x):
    return x if x.dtype == BF16 else x.astype(BF16)


def _dot(a, b):
    return jnp.dot(_bf(a), _bf(b), preferred_element_type=F32)


def _dot_nt(a, b):
    return lax.dot_general(_bf(a), _bf(b), (((1,), (1,)), ((), ())), preferred_element_type=F32)


def _silu(x):
    return x * (1.0 / (1.0 + jnp.exp(-x)))


def _sigmoid(x):
    return 1.0 / (1.0 + jnp.exp(-x))


def _gelu(x):
    return 0.5 * x * (1.0 + jnp.tanh(np.sqrt(2.0 / np.pi).astype(np.float32) * (x + 0.044715 * (x * x * x))))


class _Group:
    def __init__(self, rows, tm, rows_per_batch, mod_row0):
        self.rows, self.tm = rows, tm
        self.per_row = rows_per_batch is None
        self.blocks_per_batch = None if self.per_row else rows_per_batch // tm
        self.mod_row0 = mod_row0

    def mod_operand(self, mod):
        return mod if self.per_row else mod.reshape(DEPTH, MOD_ROWS, 1, 6 * D)

    def mod_spec(self, layer, chunk, tn):
        cb = chunk * D // tn
        if self.per_row:
            return pl.BlockSpec((None, self.tm, tn), lambda i, j: (layer, 0, j + cb))
        bpb, r0 = self.blocks_per_batch, self.mod_row0
        return pl.BlockSpec((None, None, 1, tn), lambda i, j: (layer, r0 + i // bpb, 0, j + cb))


PROMPT = _Group(BATCH * SEQ, 1024, SEQ, DEC_BATCH)
SAMPLE = _Group(DEC_BATCH, DEC_BATCH, None, 0)


def _ada_kernel(c_ref, w_ref, b_ref, o_ref):
    o_ref[...] = _dot(_silu(c_ref[...]), w_ref[...]) + b_ref[...]


def _ada(c_all, w_ada, b_ada):
    tn = 1024
    return pl.pallas_call(
        _ada_kernel,
        grid=(DEPTH, 6 * D // tn),
        in_specs=[pl.BlockSpec((MOD_ROWS, D), lambda l, j: (0, 0)),
                  pl.BlockSpec((None, D, tn), lambda l, j: (l, 0, j)),
                  pl.BlockSpec((None, 1, tn), lambda l, j: (l, 0, j))],
        out_specs=pl.BlockSpec((None, MOD_ROWS, tn), lambda l, j: (l, 0, j)),
        out_shape=jax.ShapeDtypeStruct((DEPTH, MOD_ROWS, 6 * D), F32),
        compiler_params=_cparams(2),
        name="ada",
    )(c_all, w_ada, b_ada.reshape(DEPTH, 1, 6 * D))


def _prep_kernel(x_ref, nw_ref, sc_ref, sh_ref, o_ref):
    x = x_ref[...]
    y = x * lax.rsqrt(jnp.mean(x * x, axis=-1, keepdims=True) + EPS) * nw_ref[...]
    o_ref[...] = (y * (1.0 + sc_ref[...]) + sh_ref[...]).astype(o_ref.dtype)


def _prep(grp, layer, x, norm_w, mod, sc_chunk, sh_chunk):
    tm = min(grp.tm, 512)
    g = _Group(grp.rows, tm, None if grp.per_row else SEQ, grp.mod_row0)
    modop = g.mod_operand(mod)
    return pl.pallas_call(
        _prep_kernel,
        grid=(grp.rows // tm, 1),
        in_specs=[pl.BlockSpec((tm, D), lambda i, j: (i, 0)),
                  pl.BlockSpec((None, 1, D), lambda i, j: (layer, 0, 0)),
                  g.mod_spec(layer, sc_chunk, D),
                  g.mod_spec(layer, sh_chunk, D)],
        out_specs=pl.BlockSpec((tm, D), lambda i, j: (i, 0)),
        out_shape=jax.ShapeDtypeStruct((grp.rows, D), BF16),
        compiler_params=_cparams(2),
        name="prep",
    )(x, norm_w.reshape(DEPTH, 1, D), modop, modop)


def _final_norm_kernel(x_ref, nw_ref, o_ref):
    x = x_ref[...]
    o_ref[...] = x * lax.rsqrt(jnp.mean(x * x, axis=-1, keepdims=True) + EPS) * nw_ref[...]


def _final_norm(x, w):
    rows = x.shape[0]
    tm = min(rows, 512)
    return pl.pallas_call(
        _final_norm_kernel,
        grid=(rows // tm,),
        in_specs=[pl.BlockSpec((tm, D), lambda i: (i, 0)), pl.BlockSpec((1, D), lambda i: (0, 0))],
        out_specs=pl.BlockSpec((tm, D), lambda i: (i, 0)),
        out_shape=jax.ShapeDtypeStruct((rows, D), F32),
        compiler_params=_cparams(1),
        name="final_norm",
    )(x, w.reshape(1, D))


def _mm_kernel(*refs, n_a, term_a, n_extra, epilogue):
    a_refs = refs[:n_a]
    w_refs = refs[n_a:n_a + len(term_a)]
    e_refs = refs[n_a + len(term_a):n_a + len(term_a) + n_extra]
    o_refs = refs[n_a + len(term_a) + n_extra:]
    a_vals = [_bf(a[...]) for a in a_refs]
    dots = [jnp.dot(a_vals[ai], _bf(w[...]), preferred_element_type=F32) for ai, w in zip(term_a, w_refs)]
    outs = epilogue(dots, [e[...] for e in e_refs])
    for o_ref, o in zip(o_refs, outs):
        o_ref[...] = o.astype(o_ref.dtype)


def _fused_matmul(name, grp, layer, a_list, terms, extras, epilogue, outs, n_cols, tn, tm=None):
    tm = tm or grp.tm
    g = grp if tm == grp.tm else _Group(grp.rows, tm, None if grp.per_row else SEQ, grp.mod_row0)
    grid = (grp.rows // tm, n_cols // tn)
    args, in_specs = [], []
    for a in a_list:
        args.append(a)
        in_specs.append(pl.BlockSpec((tm, a.shape[1]), lambda i, j: (i, 0)))
    for ai, w, col0 in terms:
        assert col0 % tn == 0 and w.shape[-2] == a_list[ai].shape[1]
        args.append(w)
        in_specs.append(pl.BlockSpec((None, w.shape[-2], tn), lambda i, j, cb=col0 // tn: (layer, 0, j + cb)))
    for ex in extras:
        if ex[0] == "tile":
            _, arr, col0 = ex
            assert col0 % tn == 0
            args.append(arr)
            in_specs.append(pl.BlockSpec((tm, tn), lambda i, j, cb=col0 // tn: (i, j + cb)))
        else:
            _, mod, chunk = ex
            args.append(g.mod_operand(mod))
            in_specs.append(g.mod_spec(layer, chunk, tn))
    kern = functools.partial(_mm_kernel, n_a=len(a_list), term_a=tuple(t[0] for t in terms),
                             n_extra=len(extras), epilogue=epilogue)
    res = pl.pallas_call(
        kern,
        grid=grid,
        in_specs=in_specs,
        out_specs=[pl.BlockSpec((tm, tn), lambda i, j: (i, j)) for _ in outs],
        out_shape=[jax.ShapeDtypeStruct((grp.rows, n_cols), dt) for dt in outs],
        compiler_params=_cparams(2),
        name=name,
    )(*args)
    return res


def _epi_plain(dots, ex):
    return [dots[0]]


def _epi_merge(dots, ex):
    return [_sigmoid(ex[0]) * dots[0] + _sigmoid(ex[1]) * dots[1] + _sigmoid(ex[2]) * dots[2]]


def _epi_residual(dots, ex):
    return [ex[0] + ex[1] * dots[0]]


def _epi_swiglu(dots, ex):
    return [_silu(dots[0]) * dots[1]]


def _log_sigmoid(u):
    return -(jnp.maximum(-u, 0.0) + jnp.log1p(jnp.exp(-jnp.abs(u))))


def _gla_prompt_kernel(q_ref, k_ref, v_ref, ga_ref, lr_ref, wgk_ref, bgk_ref, nw_ref,
                       oa_ref, sfin_ref, st_s, vt_s, o_s, *, tb):
    t = pl.program_id(1)
    nsc = tb // GLA_SC
    nd = GLA_SC // GLA_CHUNK - 1

    @pl.when(t == 0)
    def _():
        st_s[...] = jnp.zeros_like(st_s)

    u = _dot(lr_ref[...], wgk_ref[...]) + bgk_ref[...]
    gk = _log_sigmoid(u) * (1.0 / 16.0)
    row = lax.broadcasted_iota(jnp.int32, gk.shape, 0)
    pos_c = row % GLA_CHUNK
    pos_s = row % GLA_SC
    b = gk
    for s in (1, 2, 4, 8):
        b = b + jnp.where(pos_c >= s, pltpu.roll(b, s, 0), 0.0)
    blb = jnp.where(pos_c == GLA_CHUNK - 1, b, 0.0)
    for s in (1, 2, 4, 8):
        blb = blb + pltpu.roll(blb, tb - s, 0)
    acc = jnp.where(pos_s >= GLA_CHUNK, pltpu.roll(blb, GLA_CHUNK, 0), 0.0)
    for s in (16, 32, 64):
        acc = acc + jnp.where(pos_s >= s, pltpu.roll(acc, s, 0), 0.0)
    bs = b + acc

    q = q_ref[...] * (GLA_DK ** -0.5)
    k = k_ref[...]
    qin = _bf(q * jnp.exp(b))
    kout = _bf(k * jnp.exp(-b))
    kd = k * jnp.exp(blb - b)
    qsc = _bf(q * jnp.exp(bs))
    vt_s[...] = v_ref[...].T

    ri = lax.broadcasted_iota(jnp.int32, (tb, tb), 0)
    ci = lax.broadcasted_iota(jnp.int32, (tb, tb), 1)
    delta = jnp.where(ri // GLA_SC == ci // GLA_SC, ri // GLA_CHUNK - ci // GLA_CHUNK, -1)
    m_intra = (delta == 0) & (ci <= ri)
    kds = [_bf(kd)]
    for d in range(1, nd):
        kd = kd * jnp.exp(pltpu.roll(blb, tb - GLA_CHUNK * d, 0))
        kds.append(_bf(kd))
    for h in range(GLA_H):
        ks = slice(h * GLA_DK, (h + 1) * GLA_DK)
        vs = slice(h * GLA_DV, (h + 1) * GLA_DV)
        a = jnp.where(m_intra, _dot_nt(qin[:, ks], kout[:, ks]), 0.0)
        for d in range(nd):
            a = jnp.where(delta == d + 1, _dot_nt(qin[:, ks], kds[d][:, ks]), a)
        o_s[:, vs] = _dot(a, v_ref[:, vs])

    for sc in range(nsc):
        rows = slice(sc * GLA_SC, (sc + 1) * GLA_SC)
        last = bs[(sc + 1) * GLA_SC - 1:(sc + 1) * GLA_SC, :]
        k2 = _bf(k[rows, :] * jnp.exp(last - bs[rows, :]))
        elast = jnp.exp(last)
        for h in range(GLA_H):
            ks = slice(h * GLA_DK, (h + 1) * GLA_DK)
            vs = slice(h * GLA_DV, (h + 1) * GLA_DV)
            st = st_s[h]
            o_s[rows, vs] += _dot_nt(qsc[rows, ks], st)
            st_s[h] = st * elast[:, ks] + _dot(vt_s[vs, rows], k2[:, ks])

    nw = nw_ref[...]
    for h in range(GLA_H):
        vs = slice(h * GLA_DV, (h + 1) * GLA_DV)
        o = o_s[:, vs]
        y = o * lax.rsqrt(jnp.mean(o * o, axis=-1, keepdims=True) + EPS) * nw
        oa_ref[:, vs] = (y * _silu(ga_ref[:, vs])).astype(oa_ref.dtype)

    @pl.when(t == pl.num_programs(1) - 1)
    def _():
        for h in range(GLA_H):
            sfin_ref[h] = st_s[h].T


def _gla_prompt(layer, z, lr, wgk_pad, b_gk, gla_norm_w):
    tb = 256
    nt = SEQ // tb
    row = lambda b, t: b * nt + t
    kern = functools.partial(_gla_prompt_kernel, tb=tb)
    return pl.pallas_call(
        kern,
        grid=(BATCH, nt),
        in_specs=[pl.BlockSpec((tb, GLA_KEY), lambda b, t: (row(b, t), Z_QA // GLA_KEY)),
                  pl.BlockSpec((tb, GLA_KEY), lambda b, t: (row(b, t), Z_KA // GLA_KEY)),
                  pl.BlockSpec((tb, GLA_VAL), lambda b, t: (row(b, t), Z_VA // GLA_VAL)),
                  pl.BlockSpec((tb, GLA_VAL), lambda b, t: (row(b, t), Z_GA // GLA_VAL)),
                  pl.BlockSpec((tb, LR_PAD), lambda b, t: (row(b, t), 0)),
                  pl.BlockSpec((None, LR_PAD, GLA_KEY), lambda b, t: (layer, 0, 0)),
                  pl.BlockSpec((None, 1, GLA_KEY), lambda b, t: (layer, 0, 0)),
                  pl.BlockSpec((None, 1, GLA_DV), lambda b, t: (layer, 0, 0))],
        out_specs=[pl.BlockSpec((tb, GLA_VAL), lambda b, t: (row(b, t), 0)),
                   pl.BlockSpec((None, GLA_H, GLA_DK, GLA_DV), lambda b, t: (b, 0, 0, 0))],
        out_shape=[jax.ShapeDtypeStruct((BATCH * SEQ, GLA_VAL), BF16),
                   jax.ShapeDtypeStruct((BATCH, GLA_H, GLA_DK, GLA_DV), F32)],
        scratch_shapes=[pltpu.VMEM((GLA_H, GLA_DV, GLA_DK), F32),
                        pltpu.VMEM((GLA_VAL, tb), F32),
                        pltpu.VMEM((tb, GLA_VAL), F32)],
        compiler_params=_cparams(2),
        name="gla_prompt",
    )(z, z, z, z, lr, wgk_pad, b_gk.reshape(DEPTH, 1, GLA_KEY), gla_norm_w.reshape(DEPTH, 1, GLA_DV))


def _gla_sample_kernel(q_ref, k_ref, v_ref, ga_ref, lr_ref, wgk_ref, bgk_ref, nw_ref, s_ref,
                       oa_ref, snew_ref, x_s, *, rb):
    u = _dot(lr_ref[...], wgk_ref[...]) + bgk_ref[...]
    eg = jnp.exp(_log_sigmoid(u) * (1.0 / 16.0))
    x_s[...] = jnp.zeros_like(x_s)
    x_s[0:rb, :] = q_ref[...] * (GLA_DK ** -0.5)
    x_s[rb:2 * rb, :] = k_ref[...]
    x_s[2 * rb:3 * rb, :] = eg
    xt = x_s[...].T
    nw = nw_ref[...]
    for r in range(rb):
        qc = xt[:, r:r + 1]
        kc = xt[:, rb + r:rb + r + 1]
        gc = xt[:, 2 * rb + r:2 * rb + r + 1]
        s_new = gc * s_ref[r, 0] + kc * v_ref[r:r + 1, :]
        snew_ref[r, 0] = s_new
        o = jnp.sum(qc * s_new, axis=0, keepdims=True)
        y = o * lax.rsqrt(jnp.mean(o * o, axis=-1, keepdims=True) + EPS) * nw
        oa_ref[r:r + 1, :] = y * _silu(ga_ref[r:r + 1, :])


def _gla_sample(layer, z, lr, wgk_pad, b_gk, gla_norm_w, state_gla, state_out):
    rb = 8
    kern = functools.partial(_gla_sample_kernel, rb=rb)
    in_specs = [pl.BlockSpec((rb, GLA_DK), lambda i, h: (i, Z_QA // GLA_DK + h)),
                pl.BlockSpec((rb, GLA_DK), lambda i, h: (i, Z_KA // GLA_DK + h)),
                pl.BlockSpec((rb, GLA_DV), lambda i, h: (i, Z_VA // GLA_DV + h)),
                pl.BlockSpec((rb, GLA_DV), lambda i, h: (i, Z_GA // GLA_DV + h)),
                pl.BlockSpec((rb, LR_PAD), lambda i, h: (i, 0)),
                pl.BlockSpec((None, LR_PAD, GLA_DK), lambda i, h: (layer, 0, h)),
                pl.BlockSpec((None, 1, GLA_DK), lambda i, h: (layer, 0, h)),
                pl.BlockSpec((None, 1, GLA_DV), lambda i, h: (layer, 0, 0)),
                pl.BlockSpec((None, rb, 1, GLA_DK, GLA_DV), lambda i, h: (layer, i, h, 0, 0))]
    args = [z, z, z, z, lr, wgk_pad, b_gk.reshape(DEPTH, 1, GLA_KEY), gla_norm_w.reshape(DEPTH, 1, GLA_DV),
            state_gla]
    aliases = {}
    if state_out is not None:
        in_specs.append(pl.BlockSpec(memory_space=pl.ANY))
        args.append(state_out)
        aliases = {len(args) - 1: 1}
        kern = functools.partial(_drop_last_input, kern, n_in=len(args))
    return pl.pallas_call(
        kern,
        grid=(DEC_BATCH // rb, GLA_H),
        in_specs=in_specs,
        out_specs=[pl.BlockSpec((rb, GLA_DV), lambda i, h: (i, h)),
                   pl.BlockSpec((None, rb, 1, GLA_DK, GLA_DV), lambda i, h: (layer, i, h, 0, 0))],
        out_shape=[jax.ShapeDtypeStruct((DEC_BATCH, GLA_VAL), F32),
                   jax.ShapeDtypeStruct((DEPTH, DEC_BATCH, GLA_H, GLA_DK, GLA_DV), F32)],
        scratch_shapes=[pltpu.VMEM((LANE, GLA_DK), F32)],
        input_output_aliases=aliases,
        compiler_params=_cparams(2),
        name="gla_sample",
    )(*args)


def _drop_last_input(kern, *refs, n_in):
    return kern(*refs[:n_in - 1], *refs[n_in:])


def _layernorm(x, w, b):
    mu = jnp.mean(x, axis=-1, keepdims=True)
    xc = x - mu
    var = jnp.mean(xc * xc, axis=-1, keepdims=True)
    return xc * lax.rsqrt(var + EPS) * w + b


def _gmlp_prompt_kernel(u_ref, v_ref, ws_ref, bst_ref, nw_ref, nb_ref, ob_ref, *, nsub):
    ri = lax.broadcasted_iota(jnp.int32, (GM_CHUNK, GM_CHUNK), 0)
    ci = lax.broadcasted_iota(jnp.int32, (GM_CHUNK, GM_CHUNK), 1)
    tril = ci <= ri
    for s in range(nsub):
        rs = slice(s * GM_CHUNK, (s + 1) * GM_CHUNK)
        u = _gelu(u_ref[rs, :])
        v = _layernorm(_gelu(v_ref[rs, :]), nw_ref[...], nb_ref[...])
        for g in range(GM_GROUPS):
            cs = slice(g * GM_GW, (g + 1) * GM_GW)
            wm = jnp.where(tril, ws_ref[g], 0.0)
            mixed = _dot(wm, v[:, cs]) + bst_ref[:, g:g + 1]
            ob_ref[rs, cs] = (u[:, cs] * mixed).astype(ob_ref.dtype)


def _gmlp_prompt(layer, z, gm_ws, gm_bs_t, gm_norm_w, gm_norm_b):
    nsub = 4
    tb = nsub * GM_CHUNK
    kern = functools.partial(_gmlp_prompt_kernel, nsub=nsub)
    return pl.pallas_call(
        kern,
        grid=(BATCH * SEQ // tb,),
        in_specs=[pl.BlockSpec((tb, GM_WIDTH), lambda i: (i, Z_UB // GM_WIDTH)),
                  pl.BlockSpec((tb, GM_WIDTH), lambda i: (i, Z_VB // GM_WIDTH)),
                  pl.BlockSpec((None, GM_GROUPS, GM_CHUNK, GM_CHUNK), lambda i: (layer, 0, 0, 0)),
                  pl.BlockSpec((None, GM_CHUNK, GM_GROUPS), lambda i: (layer, 0, 0)),
                  pl.BlockSpec((None, 1, GM_WIDTH), lambda i: (layer, 0, 0)),
                  pl.BlockSpec((None, 1, GM_WIDTH), lambda i: (layer, 0, 0))],
        out_specs=pl.BlockSpec((tb, GM_WIDTH), lambda i: (i, 0)),
        out_shape=jax.ShapeDtypeStruct((BATCH * SEQ, GM_WIDTH), BF16),
        compiler_params=_cparams(1),
        name="gmlp_prompt",
    )(z, z, gm_ws, gm_bs_t, gm_norm_w.reshape(DEPTH, 1, GM_WIDTH), gm_norm_b.reshape(DEPTH, 1, GM_WIDTH))


def _gmlp_sample_kernel(u_ref, v_ref, w0_ref, b0_ref, nw_ref, nb_ref, ob_ref, vn_ref):
    u = _gelu(u_ref[...])
    v = _layernorm(_gelu(v_ref[...]), nw_ref[...], nb_ref[...])
    vn_ref[...] = v
    ob_ref[...] = u * (w0_ref[...] * v + b0_ref[...])


def _gmlp_sample(layer, z, w0_row, b0_row, gm_norm_w, gm_norm_b):
    full = lambda i: (0, 0)
    lrow = lambda i: (layer, 0, 0)
    return pl.pallas_call(
        _gmlp_sample_kernel,
        grid=(1,),
        in_specs=[pl.BlockSpec((DEC_BATCH, GM_WIDTH), lambda i: (0, Z_UB // GM_WIDTH)),
                  pl.BlockSpec((DEC_BATCH, GM_WIDTH), lambda i: (0, Z_VB // GM_WIDTH)),
                  pl.BlockSpec((None, 1, GM_WIDTH), lrow),
                  pl.BlockSpec((None, 1, GM_WIDTH), lrow),
                  pl.BlockSpec((None, 1, GM_WIDTH), lrow),
                  pl.BlockSpec((None, 1, GM_WIDTH), lrow)],
        out_specs=[pl.BlockSpec((DEC_BATCH, GM_WIDTH), full), pl.BlockSpec((DEC_BATCH, GM_WIDTH), full)],
        out_shape=[jax.ShapeDtypeStruct((DEC_BATCH, GM_WIDTH), F32),
                   jax.ShapeDtypeStruct((DEC_BATCH, GM_WIDTH), F32)],
        compiler_params=_cparams(1),
        name="gmlp_sample",
    )(z, z, w0_row, b0_row, gm_norm_w.reshape(DEPTH, 1, GM_WIDTH), gm_norm_b.reshape(DEPTH, 1, GM_WIDTH))


def _alibi_slope(h):
    return float(2.0 ** (-8.0 * (h + 1) / SWA_HQ))


def _swa_lane_halves(x, half):
    lane = lax.broadcasted_iota(jnp.int32, x.shape, 1)
    own = jnp.where((lane >= half * SWA_HD) & (lane < (half + 1) * SWA_HD), x, 0.0)
    other = pltpu.roll(own, SWA_HD, 1)
    return (own, other) if half == 0 else (other, own)


def _swa_prompt_kernel(q_ref, kc_ref, kp_ref, vc_ref, vp_ref, sink_ref, oc_ref, s_s, p_s):
    n = pl.program_id(1)
    w = WINDOW
    ri = lax.broadcasted_iota(jnp.int32, (w, 2 * w), 0)
    ci = lax.broadcasted_iota(jnp.int32, (w, 2 * w), 1)
    dist_i = w + ri - ci
    valid = (dist_i >= 0) & (dist_i < w) & ((ci >= w) | (n > 0))
    dist = dist_i.astype(F32)
    kcat = jnp.concatenate([kp_ref[...], kc_ref[...]], axis=0)
    vcat = jnp.concatenate([vp_ref[...], vc_ref[...]], axis=0)
    heads = []
    for kv in range(SWA_HKV):
        t, half = kv // 2, kv % 2
        k_lo, k_hi = _swa_lane_halves(kcat[:, t * LANE:(t + 1) * LANE], half)
        q2 = jnp.concatenate([q_ref[:, 2 * kv * LANE:(2 * kv + 1) * LANE],
                              q_ref[:, (2 * kv + 1) * LANE:(2 * kv + 2) * LANE]], axis=0)
        for par, kk in ((0, k_lo), (1, k_hi)):
            s = _dot_nt(q2, kk) * (SWA_HD ** -0.5)
            for e in range(2):
                h = SWA_G * kv + 2 * e + par
                seg = len(heads)
                heads.append(h)
                s_s[seg * w:(seg + 1) * w, :] = jnp.where(
                    valid, s[e * w:(e + 1) * w, :] - _alibi_slope(h) * dist, NEG_BIG)
    s = s_s[...]
    sink = jnp.concatenate([jnp.broadcast_to(sink_ref[h:h + 1, 0:1], (w, 1)) for h in heads], axis=0)
    m = jnp.maximum(jnp.max(s, axis=-1, keepdims=True), sink)
    p = jnp.exp(s - m)
    inv = 1.0 / (jnp.sum(p, axis=-1, keepdims=True) + jnp.exp(sink - m))
    p_s[...] = (p * inv).astype(p_s.dtype)
    for kv in range(SWA_HKV):
        t, half = kv // 2, kv % 2
        v_lo, v_hi = _swa_lane_halves(vcat[:, t * LANE:(t + 1) * LANE], half)
        r0 = SWA_G * kv * w
        o = _dot(p_s[r0:r0 + 2 * w, :], v_lo) + _dot(p_s[r0 + 2 * w:r0 + 4 * w, :], v_hi)
        oc_ref[:, 2 * kv * LANE:(2 * kv + 1) * LANE] = o[:w].astype(oc_ref.dtype)
        oc_ref[:, (2 * kv + 1) * LANE:(2 * kv + 2) * LANE] = o[w:].astype(oc_ref.dtype)


def _swa_prompt(layer, z, sinks_b):
    nb = SEQ // WINDOW
    row = lambda b, n: b * nb + n
    prev = lambda b, n: b * nb + jnp.maximum(n - 1, 0)
    return pl.pallas_call(
        _swa_prompt_kernel,
        grid=(BATCH, nb),
        in_specs=[pl.BlockSpec((WINDOW, SWA_Q), lambda b, n: (row(b, n), Z_QC // SWA_Q)),
                  pl.BlockSpec((WINDOW, SWA_KV), lambda b, n: (row(b, n), Z_KC // SWA_KV)),
                  pl.BlockSpec((WINDOW, SWA_KV), lambda b, n: (prev(b, n), Z_KC // SWA_KV)),
                  pl.BlockSpec((WINDOW, SWA_KV), lambda b, n: (row(b, n), Z_VC // SWA_KV)),
                  pl.BlockSpec((WINDOW, SWA_KV), lambda b, n: (prev(b, n), Z_VC // SWA_KV)),
                  pl.BlockSpec((None, SWA_HQ, LANE), lambda b, n: (layer, 0, 0))],
        out_specs=pl.BlockSpec((WINDOW, SWA_Q), lambda b, n: (row(b, n), 0)),
        out_shape=jax.ShapeDtypeStruct((BATCH * SEQ, SWA_Q), BF16),
        scratch_shapes=[pltpu.VMEM((SWA_HQ * WINDOW, 2 * WINDOW), F32),
                        pltpu.VMEM((SWA_HQ * WINDOW, 2 * WINDOW), BF16)],
        compiler_params=_cparams(2),
        name="swa_prompt",
    )(z, z, z, z, z, sinks_b)


def _swa_sample_kernel(qm_ref, kn_ref, vn_ref, kb_ref, vb_ref, sink_ref, slope_ref, om_ref, *, rb):
    wb = WINDOW
    j = lax.broadcasted_iota(jnp.int32, (SWA_HQ, wb), 1)
    dist = (wb - j).astype(F32)
    ok = j >= 1
    slope = slope_ref[:, 0:1]
    sink = sink_ref[:, 0:1]
    for r in range(rb):
        qm = qm_ref[r]
        s = _dot_nt(qm, kb_ref[r]) * (SWA_HD ** -0.5) - slope * dist
        s = jnp.where(ok, s, NEG_BIG)
        s_self = jnp.sum(_bf(qm).astype(F32) * _bf(kn_ref[r:r + 1, :]).astype(F32), axis=-1,
                         keepdims=True) * (SWA_HD ** -0.5)
        m = jnp.maximum(jnp.maximum(jnp.max(s, axis=-1, keepdims=True), s_self), sink)
        p = jnp.exp(s - m)
        p_self = jnp.exp(s_self - m)
        inv = 1.0 / (jnp.sum(p, axis=-1, keepdims=True) + p_self + jnp.exp(sink - m))
        o = _dot(p * inv, vb_ref[r]) + _bf(p_self * inv).astype(F32) * _bf(vn_ref[r:r + 1, :]).astype(F32)
        om_ref[r] = o


def _swa_sample(layer, qm, z, cache_k, cache_v, sinks_b, slopes_b):
    rb = 8
    kern = functools.partial(_swa_sample_kernel, rb=rb)
    return pl.pallas_call(
        kern,
        grid=(DEC_BATCH // rb,),
        in_specs=[pl.BlockSpec((rb, SWA_HQ, SWA_KV), lambda i: (i, 0, 0)),
                  pl.BlockSpec((rb, SWA_KV), lambda i: (i, Z_KC // SWA_KV)),
                  pl.BlockSpec((rb, SWA_KV), lambda i: (i, Z_VC // SWA_KV)),
                  pl.BlockSpec((None, rb, WINDOW, SWA_KV), lambda i: (layer, i, 0, 0)),
                  pl.BlockSpec((None, rb, WINDOW, SWA_KV), lambda i: (layer, i, 0, 0)),
                  pl.BlockSpec((None, SWA_HQ, LANE), lambda i: (layer, 0, 0)),
                  pl.BlockSpec((SWA_HQ, LANE), lambda i: (0, 0))],
        out_specs=pl.BlockSpec((rb, SWA_HQ, SWA_KV), lambda i: (i, 0, 0)),
        out_shape=jax.ShapeDtypeStruct((DEC_BATCH, SWA_HQ, SWA_KV), F32),
        compiler_params=_cparams(1),
        name="swa_sample",
    )(qm, z, z, cache_k, cache_v, sinks_b, slopes_b)


def _layer(grp, layer, x, mod, p, state_gla=None, cache_k=None, cache_v=None, state_out=None):
    sample = grp.per_row
    h = _prep(grp, layer, x, p["norm1_w"], mod, MOD_SC1, MOD_SH1)
    tn_in = 512
    (z,) = _fused_matmul("w_in", grp, layer, [h], [(0, p["w_in_r"], 0)], [], _epi_plain, [F32], Z_WIDTH, tn_in)
    (lr,) = _fused_matmul("w_lr", grp, layer, [h], [(0, p["w_lr"], 0)], [], _epi_plain, [F32], LR_PAD, LR_PAD)
    if not sample:
        oa, s_new = _gla_prompt(layer, z, lr, p["wgk_pad"], p["b_gk"], p["gla_norm_w"])
        ob = _gmlp_prompt(layer, z, p["gm_ws"], p["gm_bs_t"], p["gm_norm_w"], p["gm_norm_b"])
        v_gm = None
        oc = _swa_prompt(layer, z, p["sinks_b"])
        z4 = z.reshape(BATCH, SEQ, Z_WIDTH)
        k_rows = z4[:, SEQ - WINDOW:, Z_KC:Z_KC + SWA_KV].reshape(BATCH, WINDOW, SWA_HKV, SWA_HD)
        v_rows = z4[:, SEQ - WINDOW:, Z_VC:Z_VC + SWA_KV].reshape(BATCH, WINDOW, SWA_HKV, SWA_HD)
    else:
        oa, s_new = _gla_sample(layer, z, lr, p["wgk_pad"], p["b_gk"], p["gla_norm_w"], state_gla, state_out)
        ob, v_gm = _gmlp_sample(layer, z, p["gm_w0"], p["gm_b0"], p["gm_norm_w"], p["gm_norm_b"])
        q4 = z[:, Z_QC:Z_QC + SWA_Q].reshape(DEC_BATCH, SWA_HKV, SWA_G, 1, SWA_HD)
        eye = jnp.eye(SWA_HKV, dtype=F32).reshape(SWA_HKV, 1, SWA_HKV, 1)
        qm = (q4 * eye[None]).reshape(DEC_BATCH, SWA_HQ, SWA_KV)
        om = _swa_sample(layer, qm, z, cache_k, cache_v, p["sinks_b"], p["slopes_b"])
        om5 = om.reshape(DEC_BATCH, SWA_HKV, SWA_G, SWA_HKV, SWA_HD)
        oc = jnp.sum(om5 * eye[None], axis=3).reshape(DEC_BATCH, SWA_Q)
        k_rows = z[:, Z_KC:Z_KC + SWA_KV].reshape(DEC_BATCH, 1, SWA_HKV, SWA_HD)
        v_rows = z[:, Z_VC:Z_VC + SWA_KV].reshape(DEC_BATCH, 1, SWA_HKV, SWA_HD)
    (merged,) = _fused_matmul(
        "merge", grp, layer, [oa, ob, oc],
        [(0, p["w_pa"], 0), (1, p["w_pb"], 0), (2, p["w_pc"], 0)],
        [("tile", z, Z_GATES), ("tile", z, Z_GATES + D), ("tile", z, Z_GATES + 2 * D)],
        _epi_merge, [BF16], D, 256)
    (x1,) = _fused_matmul("w_o", grp, layer, [merged], [(0, p["w_o"], 0)],
                          [("tile", x, 0), ("mod", mod, MOD_G1)], _epi_residual, [F32], D, 512)
    h2 = _prep(grp, layer, x1, p["norm2_w"], mod, MOD_SC2, MOD_SH2)
    (hid,) = _fused_matmul("ffn_in", grp, layer, [h2], [(0, p["w_ffn_in"], 0), (0, p["w_ffn_in"], FFN_HIDDEN)],
                           [], _epi_swiglu, [BF16], FFN_HIDDEN, 512)
    (x2,) = _fused_matmul("ffn_out", grp, layer, [hid], [(0, p["w_ffn_out"], 0)],
                          [("tile", x1, 0), ("mod", mod, MOD_G2)], _epi_residual, [F32], D, 256)
    return x2, s_new, k_rows, v_rows, v_gm


def kernel(x_prompt, x_sample, c_prompt, c_sample, state_gla, cache_swa_k, cache_swa_v, w_ada, b_ada, norm1_w,
           norm2_w, w_in, w_gk2, b_gk, gla_norm_w, gm_norm_w, gm_norm_b, gm_ws, gm_bs, swa_sinks, w_pa, w_pb,
           w_pc, w_o, w_ffn_in, w_ffn_out, final_norm_w):
    w_in_r = jnp.concatenate([w_in[:, :, :LR_COL], w_in[:, :, LR_COL + GLA_RANK:]], axis=2).astype(BF16)
    w_lr = jnp.pad(w_in[:, :, LR_COL:LR_COL + GLA_RANK], ((0, 0), (0, 0), (0, LR_PAD - GLA_RANK))).astype(BF16)
    p = {
        "norm1_w": norm1_w, "norm2_w": norm2_w, "w_in_r": w_in_r, "w_lr": w_lr,
        "wgk_pad": jnp.pad(w_gk2, ((0, 0), (0, LR_PAD - GLA_RANK), (0, 0))),
        "b_gk": b_gk, "gla_norm_w": gla_norm_w, "gm_norm_w": gm_norm_w, "gm_norm_b": gm_norm_b,
        "gm_ws": gm_ws, "gm_bs_t": jnp.swapaxes(gm_bs, 1, 2),
        "gm_w0": jnp.repeat(gm_ws[:, :, 0, 0], GM_GW, axis=1).reshape(DEPTH, 1, GM_WIDTH),
        "gm_b0": jnp.repeat(gm_bs[:, :, 0], GM_GW, axis=1).reshape(DEPTH, 1, GM_WIDTH),
        "sinks_b": jnp.broadcast_to(swa_sinks[:, :, None], (DEPTH, SWA_HQ, LANE)),
        "slopes_b": jnp.broadcast_to(
            jnp.asarray([_alibi_slope(h) for h in range(SWA_HQ)], F32)[:, None], (SWA_HQ, LANE)),
        "w_pa": w_pa, "w_pb": w_pb, "w_pc": w_pc, "w_o": w_o, "w_ffn_in": w_ffn_in, "w_ffn_out": w_ffn_out,
    }
    c_all = jnp.concatenate([c_sample, c_prompt, jnp.zeros((MOD_ROWS - DEC_BATCH - BATCH, D), F32)], axis=0)
    mod = _ada(c_all, w_ada, b_ada)

    xp = x_prompt.reshape(BATCH * SEQ, D)
    xs = x_sample.reshape(DEC_BATCH, D)
    cache_k = cache_swa_k.reshape(DEPTH, DEC_BATCH, WINDOW, SWA_KV)
    cache_v = cache_swa_v.reshape(DEPTH, DEC_BATCH, WINDOW, SWA_KV)
    gla_p, kp, vp, ksm, vsm, gmv = [], [], [], [], [], []
    state_out = None
    for l in range(DEPTH):
        xp, s_p, k_p, v_p, _ = _layer(PROMPT, l, xp, mod, p)
        xs, state_out, k_s, v_s, gv = _layer(SAMPLE, l, xs, mod, p, state_gla, cache_k, cache_v, state_out)
        gla_p.append(s_p)
        kp.append(k_p)
        vp.append(v_p)
        ksm.append(k_s)
        vsm.append(v_s)
        gmv.append(gv.reshape(DEC_BATCH, 1, GM_WIDTH))
    y_prompt = _final_norm(xp, final_norm_w).reshape(BATCH, SEQ, D)
    y_sample = _final_norm(xs, final_norm_w).reshape(DEC_BATCH, 1, D)
    return (y_prompt, y_sample, jnp.stack(gla_p), state_out, jnp.stack(kp), jnp.stack(vp),
            jnp.stack(ksm), jnp.stack(vsm), jnp.stack(gmv))
```

```python
import functools

import jax
import jax.numpy as jnp
import numpy as np
from jax import lax
from jax.experimental import pallas as pl
from jax.experimental.pallas import tpu as pltpu

F32 = jnp.float32
BF16 = jnp.bfloat16

D = 2048
BATCH, SEQ = 2, 4096
DEPTH = 2
DEC_BATCH = 128
GLA_H, GLA_DK, GLA_DV = 4, 256, 512
GLA_KEY, GLA_VAL = GLA_H * GLA_DK, GLA_H * GLA_DV
GLA_RANK = 16
GLA_CHUNK = 16
GLA_SC = 128
GM_WIDTH, GM_GROUPS, GM_CHUNK = 1024, 4, 128
GM_GW = GM_WIDTH // GM_GROUPS
SWA_HQ, SWA_HKV, SWA_HD, WINDOW = 16, 4, 64, 128
SWA_G = SWA_HQ // SWA_HKV
SWA_Q, SWA_KV = SWA_HQ * SWA_HD, SWA_HKV * SWA_HD
FFN_HIDDEN = 5632
EPS = 1e-6
NEG_BIG = -1e30

Z_QA, Z_KA, Z_VA, Z_GA = 0, 1024, 2048, 4096
Z_UB, Z_VB = 6144, 7168
Z_QC, Z_KC, Z_VC = 8192, 9216, 9472
Z_GATES = 9728
Z_WIDTH = 15872
LR_COL = 6144
LANE = 128
LR_PAD = LANE

MOD_SH1, MOD_SC1, MOD_G1, MOD_SH2, MOD_SC2, MOD_G2 = range(6)
MOD_ROWS = DEC_BATCH + 8

VMEM_LIMIT = 56 * 1024 * 1024


def _cparams(n_axes):
    return pltpu.CompilerParams(dimension_semantics=("arbitrary",) * n_axes,
                                vmem_limit_bytes=VMEM_LIMIT)


def _bf(x):
    return x if x.dtype == BF16 else x.astype(BF16)


def _dot(a, b):
    return jnp.dot(_bf(a), _bf(b), preferred_element_type=F32)


def _dot_nt(a, b):
    return lax.dot_general(_bf(a), _bf(b), (((1,), (1,)), ((), ())), preferred_element_type=F32)


def _silu(x):
    return x * (1.0 / (1.0 + jnp.exp(-x)))


def _sigmoid(x):
    return 1.0 / (1.0 + jnp.exp(-x))


def _gelu(x):
    return 0.5 * x * (1.0 + jnp.tanh(np.sqrt(2.0 / np.pi).astype(np.float32) * (x + 0.044715 * (x * x * x))))


def _mod_spec_prompt(layer, chunk, tm, tn):
    cb, bpb = chunk * D // tn, SEQ // tm
    return pl.BlockSpec((None, None, 1, tn), lambda i, j: (layer, DEC_BATCH + i // bpb, 0, j + cb))


def _mod_spec_sample(layer, chunk, tn, jmap):
    cb = chunk * D // tn
    return pl.BlockSpec((None, DEC_BATCH, tn), lambda i, j: (layer, 0, jmap(i, j) + cb))


def _ada_kernel(c_ref, w_ref, b_ref, o_ref):
    o_ref[...] = _dot(_silu(c_ref[...]), w_ref[...]) + b_ref[...]


def _ada(c_all, w_ada, b_ada):
    tn = 1024
    return pl.pallas_call(
        _ada_kernel,
        grid=(DEPTH, 6 * D // tn),
        in_specs=[pl.BlockSpec((MOD_ROWS, D), lambda l, j: (0, 0)),
                  pl.BlockSpec((None, D, tn), lambda l, j: (l, 0, j)),
                  pl.BlockSpec((None, 1, tn), lambda l, j: (l, 0, j))],
        out_specs=pl.BlockSpec((None, MOD_ROWS, tn), lambda l, j: (l, 0, j)),
        out_shape=jax.ShapeDtypeStruct((DEPTH, MOD_ROWS, 6 * D), F32),
        compiler_params=_cparams(2),
        name="ada",
    )(c_all, w_ada, b_ada.reshape(DEPTH, 1, 6 * D))


def _prep_kernel(x_ref, nw_ref, sc_ref, sh_ref, o_ref):
    x = x_ref[...]
    y = x * lax.rsqrt(jnp.mean(x * x, axis=-1, keepdims=True) + EPS) * nw_ref[...]
    o_ref[...] = (y * (1.0 + sc_ref[...]) + sh_ref[...]).astype(o_ref.dtype)


def _prep(layer, x, norm_w, mod, sc_chunk, sh_chunk, sample):
    rows = x.shape[0]
    if sample:
        tm, modop = rows, mod
        mod_specs = [_mod_spec_sample(layer, c, D, lambda i, j: j) for c in (sc_chunk, sh_chunk)]
    else:
        tm, modop = 512, mod.reshape(DEPTH, MOD_ROWS, 1, 6 * D)
        mod_specs = [_mod_spec_prompt(layer, c, tm, D) for c in (sc_chunk, sh_chunk)]
    return pl.pallas_call(
        _prep_kernel,
        grid=(rows // tm, 1),
        in_specs=[pl.BlockSpec((tm, D), lambda i, j: (i, 0)),
                  pl.BlockSpec((None, 1, D), lambda i, j: (layer, 0, 0))] + mod_specs,
        out_specs=pl.BlockSpec((tm, D), lambda i, j: (i, 0)),
        out_shape=jax.ShapeDtypeStruct((rows, D), BF16),
        compiler_params=_cparams(2),
        name="prep",
    )(x, norm_w.reshape(DEPTH, 1, D), modop, modop)


def _final_norm_kernel(x_ref, nw_ref, o_ref):
    x = x_ref[...]
    o_ref[...] = x * lax.rsqrt(jnp.mean(x * x, axis=-1, keepdims=True) + EPS) * nw_ref[...]


def _final_norm(x, w):
    rows = x.shape[0]
    tm = min(rows, 512)
    return pl.pallas_call(
        _final_norm_kernel,
        grid=(rows // tm,),
        in_specs=[pl.BlockSpec((tm, D), lambda i: (i, 0)), pl.BlockSpec((1, D), lambda i: (0, 0))],
        out_specs=pl.BlockSpec((tm, D), lambda i: (i, 0)),
        out_shape=jax.ShapeDtypeStruct((rows, D), F32),
        compiler_params=_cparams(1),
        name="final_norm",
    )(x, w.reshape(1, D))


def _mm_kernel(*refs, n_a, term_a, n_extra, n_out, epilogue):
    sizes = (n_a, n_a, len(term_a), n_extra, n_extra, n_out, n_out)
    groups, pos = [], 0
    for n in sizes:
        groups.append(refs[pos:pos + n])
        pos += n
    a_p, a_s, w_refs, e_p, e_s, o_p, o_s = groups
    w_vals = [_bf(w[...]) for w in w_refs]

    def run(a_refs, e_refs, o_refs):
        a_vals = [_bf(a[...]) for a in a_refs]
        dots = [jnp.dot(a_vals[ai], w, preferred_element_type=F32) for ai, w in zip(term_a, w_vals)]
        outs = epilogue(dots, [e[...] for e in e_refs])
        for o_ref, o in zip(o_refs, outs):
            o_ref[...] = o.astype(o_ref.dtype)

    run(a_p, e_p, o_p)

    @pl.when(pl.program_id(0) == 0)
    def _():
        run(a_s, e_s, o_s)


def _fused_matmul(name, layer, a_p, a_s, terms, extras, epilogue, out_dtypes, n_cols, tm, tn):
    rows_p, rows_s = a_p[0].shape[0], a_s[0].shape[0]
    nj = n_cols // tn
    grid = (rows_p // tm, nj)
    sj = lambda i, j: jnp.where(i == 0, j, nj - 1)
    args, in_specs = [], []
    for a in a_p:
        args.append(a)
        in_specs.append(pl.BlockSpec((tm, a.shape[1]), lambda i, j: (i, 0), pipeline_mode=pl.Buffered(1)))
    for a in a_s:
        args.append(a)
        in_specs.append(pl.BlockSpec((rows_s, a.shape[1]), lambda i, j: (0, 0)))
    for ai, w, col0 in terms:
        assert col0 % tn == 0 and w.shape[-2] == a_p[ai].shape[1]
        args.append(w)
        in_specs.append(pl.BlockSpec((None, w.shape[-2], tn), lambda i, j, cb=col0 // tn: (layer, 0, j + cb)))
    s_args, s_specs = [], []
    for ex in extras:
        if ex[0] == "tile":
            _, arr_p, arr_s, col0 = ex
            assert col0 % tn == 0
            args.append(arr_p)
            in_specs.append(pl.BlockSpec((tm, tn), lambda i, j, cb=col0 // tn: (i, j + cb)))
            s_args.append(arr_s)
            s_specs.append(pl.BlockSpec((rows_s, tn), lambda i, j, cb=col0 // tn: (0, sj(i, j) + cb)))
        else:
            _, mod, chunk = ex
            args.append(mod.reshape(DEPTH, MOD_ROWS, 1, 6 * D))
            in_specs.append(_mod_spec_prompt(layer, chunk, tm, tn))
            s_args.append(mod)
            s_specs.append(_mod_spec_sample(layer, chunk, tn, sj))
    kern = functools.partial(_mm_kernel, n_a=len(a_p), term_a=tuple(t[0] for t in terms),
                             n_extra=len(extras), n_out=len(out_dtypes), epilogue=epilogue)
    res = pl.pallas_call(
        kern,
        grid=grid,
        in_specs=in_specs + s_specs,
        out_specs=([pl.BlockSpec((tm, tn), lambda i, j: (i, j)) for _ in out_dtypes]
                   + [pl.BlockSpec((rows_s, tn), lambda i, j: (0, sj(i, j))) for _ in out_dtypes]),
        out_shape=([jax.ShapeDtypeStruct((rows_p, n_cols), dt) for dt in out_dtypes]
                   + [jax.ShapeDtypeStruct((rows_s, n_cols), dt) for dt in out_dtypes]),
        compiler_params=_cparams(2),
        name=name,
    )(*args, *s_args)
    return res[:len(out_dtypes)], res[len(out_dtypes):]


def _epi_plain(dots, ex):
    return [dots[0]]


def _epi_merge(dots, ex):
    return [_sigmoid(ex[0]) * dots[0] + _sigmoid(ex[1]) * dots[1] + _sigmoid(ex[2]) * dots[2]]


def _epi_residual(dots, ex):
    return [ex[0] + ex[1] * dots[0]]


def _epi_swiglu(dots, ex):
    return [_silu(dots[0]) * dots[1]]


def _log_sigmoid(u):
    return -(jnp.maximum(-u, 0.0) + jnp.log1p(jnp.exp(-jnp.abs(u))))


def _gla_prompt_kernel(q_ref, k_ref, v_ref, ga_ref, lr_ref, wgk_ref, bgk_ref, nw_ref,
                       oa_ref, sfin_ref, st_s, vt_s, o_s, *, tb):
    t = pl.program_id(1)
    nsc = tb // GLA_SC
    nd = GLA_SC // GLA_CHUNK - 1

    @pl.when(t == 0)
    def _():
        st_s[...] = jnp.zeros_like(st_s)

    u = _dot(lr_ref[...], wgk_ref[...]) + bgk_ref[...]
    gk = _log_sigmoid(u) * (1.0 / 16.0)
    row = lax.broadcasted_iota(jnp.int32, gk.shape, 0)
    pos_c = row % GLA_CHUNK
    pos_s = row % GLA_SC
    b = gk
    for s in (1, 2, 4, 8):
        b = b + jnp.where(pos_c >= s, pltpu.roll(b, s, 0), 0.0)
    blb = jnp.where(pos_c == GLA_CHUNK - 1, b, 0.0)
    for s in (1, 2, 4, 8):
        blb = blb + pltpu.roll(blb, tb - s, 0)
    acc = jnp.where(pos_s >= GLA_CHUNK, pltpu.roll(blb, GLA_CHUNK, 0), 0.0)
    for s in (16, 32, 64):
        acc = acc + jnp.where(pos_s >= s, pltpu.roll(acc, s, 0), 0.0)
    bs = b + acc

    q = q_ref[...] * (GLA_DK ** -0.5)
    k = k_ref[...]
    qin = _bf(q * jnp.exp(b))
    kout = _bf(k * jnp.exp(-b))
    kd = k * jnp.exp(blb - b)
    qsc = _bf(q * jnp.exp(bs))
    vt_s[...] = v_ref[...].T

    ri = lax.broadcasted_iota(jnp.int32, (tb, tb), 0)
    ci = lax.broadcasted_iota(jnp.int32, (tb, tb), 1)
    delta = jnp.where(ri // GLA_SC == ci // GLA_SC, ri // GLA_CHUNK - ci // GLA_CHUNK, -1)
    m_intra = (delta == 0) & (ci <= ri)
    kds = [_bf(kd)]
    for d in range(1, nd):
        kd = kd * jnp.exp(pltpu.roll(blb, tb - GLA_CHUNK * d, 0))
        kds.append(_bf(kd))
    for h in range(GLA_H):
        ks = slice(h * GLA_DK, (h + 1) * GLA_DK)
        vs = slice(h * GLA_DV, (h + 1) * GLA_DV)
        a = jnp.where(m_intra, _dot_nt(qin[:, ks], kout[:, ks]), 0.0)
        for d in range(nd):
            a = jnp.where(delta == d + 1, _dot_nt(qin[:, ks], kds[d][:, ks]), a)
        o_s[:, vs] = _dot(a, v_ref[:, vs])

    for sc in range(nsc):
        rows = slice(sc * GLA_SC, (sc + 1) * GLA_SC)
        last = bs[(sc + 1) * GLA_SC - 1:(sc + 1) * GLA_SC, :]
        k2 = _bf(k[rows, :] * jnp.exp(last - bs[rows, :]))
        elast = jnp.exp(last)
        for h in range(GLA_H):
            ks = slice(h * GLA_DK, (h + 1) * GLA_DK)
            vs = slice(h * GLA_DV, (h + 1) * GLA_DV)
            st = st_s[h]
            o_s[rows, vs] += _dot_nt(qsc[rows, ks], st)
            st_s[h] = st * elast[:, ks] + _dot(vt_s[vs, rows], k2[:, ks])

    nw = nw_ref[...]
    for h in range(GLA_H):
        vs = slice(h * GLA_DV, (h + 1) * GLA_DV)
        o = o_s[:, vs]
        y = o * lax.rsqrt(jnp.mean(o * o, axis=-1, keepdims=True) + EPS) * nw
        oa_ref[:, vs] = (y * _silu(ga_ref[:, vs])).astype(oa_ref.dtype)

    @pl.when(t == pl.num_programs(1) - 1)
    def _():
        for h in range(GLA_H):
            sfin_ref[h] = st_s[h].T


def _gla_prompt(layer, z, lr, wgk_pad, b_gk, gla_norm_w):
    tb = 256
    nt = SEQ // tb
    row = lambda b, t: b * nt + t
    kern = functools.partial(_gla_prompt_kernel, tb=tb)
    return pl.pallas_call(
        kern,
        grid=(BATCH, nt),
        in_specs=[pl.BlockSpec((tb, GLA_KEY), lambda b, t: (row(b, t), Z_QA // GLA_KEY)),
                  pl.BlockSpec((tb, GLA_KEY), lambda b, t: (row(b, t), Z_KA // GLA_KEY)),
                  pl.BlockSpec((tb, GLA_VAL), lambda b, t: (row(b, t), Z_VA // GLA_VAL)),
                  pl.BlockSpec((tb, GLA_VAL), lambda b, t: (row(b, t), Z_GA // GLA_VAL)),
                  pl.BlockSpec((tb, LR_PAD), lambda b, t: (row(b, t), 0)),
                  pl.BlockSpec((None, LR_PAD, GLA_KEY), lambda b, t: (layer, 0, 0)),
                  pl.BlockSpec((None, 1, GLA_KEY), lambda b, t: (layer, 0, 0)),
                  pl.BlockSpec((None, 1, GLA_DV), lambda b, t: (layer, 0, 0))],
        out_specs=[pl.BlockSpec((tb, GLA_VAL), lambda b, t: (row(b, t), 0)),
                   pl.BlockSpec((None, GLA_H, GLA_DK, GLA_DV), lambda b, t: (b, 0, 0, 0))],
        out_shape=[jax.ShapeDtypeStruct((BATCH * SEQ, GLA_VAL), BF16),
                   jax.ShapeDtypeStruct((BATCH, GLA_H, GLA_DK, GLA_DV), F32)],
        scratch_shapes=[pltpu.VMEM((GLA_H, GLA_DV, GLA_DK), F32),
                        pltpu.VMEM((GLA_VAL, tb), F32),
                        pltpu.VMEM((tb, GLA_VAL), F32)],
        compiler_params=_cparams(2),
        name="gla_prompt",
    )(z, z, z, z, lr, wgk_pad, b_gk.reshape(DEPTH, 1, GLA_KEY), gla_norm_w.reshape(DEPTH, 1, GLA_DV))


def _gla_sample_kernel(q_ref, k_ref, v_ref, ga_ref, lr_ref, wgk_ref, bgk_ref, nw_ref, s_ref,
                       oa_ref, snew_ref, x_s, *, rb):
    u = _dot(lr_ref[...], wgk_ref[...]) + bgk_ref[...]
    eg = jnp.exp(_log_sigmoid(u) * (1.0 / 16.0))
    x_s[...] = jnp.zeros_like(x_s)
    x_s[0:rb, :] = q_ref[...] * (GLA_DK ** -0.5)
    x_s[rb:2 * rb, :] = k_ref[...]
    x_s[2 * rb:3 * rb, :] = eg
    xt = x_s[...].T
    nw = nw_ref[...]
    for r in range(rb):
        qc = xt[:, r:r + 1]
        kc = xt[:, rb + r:rb + r + 1]
        gc = xt[:, 2 * rb + r:2 * rb + r + 1]
        s_new = gc * s_ref[r, 0] + kc * v_ref[r:r + 1, :]
        snew_ref[r, 0] = s_new
        o = jnp.sum(qc * s_new, axis=0, keepdims=True)
        y = o * lax.rsqrt(jnp.mean(o * o, axis=-1, keepdims=True) + EPS) * nw
        oa_ref[r:r + 1, :] = y * _silu(ga_ref[r:r + 1, :])


def _gla_sample(layer, z, lr, wgk_pad, b_gk, gla_norm_w, state_gla, state_out):
    rb = 8
    kern = functools.partial(_gla_sample_kernel, rb=rb)
    in_specs = [pl.BlockSpec((rb, GLA_DK), lambda i, h: (i, Z_QA // GLA_DK + h)),
                pl.BlockSpec((rb, GLA_DK), lambda i, h: (i, Z_KA // GLA_DK + h)),
                pl.BlockSpec((rb, GLA_DV), lambda i, h: (i, Z_VA // GLA_DV + h)),
                pl.BlockSpec((rb, GLA_DV), lambda i, h: (i, Z_GA // GLA_DV + h)),
                pl.BlockSpec((rb, LR_PAD), lambda i, h: (i, 0)),
                pl.BlockSpec((None, LR_PAD, GLA_DK), lambda i, h: (layer, 0, h)),
                pl.BlockSpec((None, 1, GLA_DK), lambda i, h: (layer, 0, h)),
                pl.BlockSpec((None, 1, GLA_DV), lambda i, h: (layer, 0, 0)),
                pl.BlockSpec((None, rb, 1, GLA_DK, GLA_DV), lambda i, h: (layer, i, h, 0, 0))]
    args = [z, z, z, z, lr, wgk_pad, b_gk.reshape(DEPTH, 1, GLA_KEY), gla_norm_w.reshape(DEPTH, 1, GLA_DV),
            state_gla]
    aliases = {}
    if state_out is not None:
        in_specs.append(pl.BlockSpec(memory_space=pl.ANY))
        args.append(state_out)
        aliases = {len(args) - 1: 1}
        kern = functools.partial(_drop_last_input, kern, n_in=len(args))
    return pl.pallas_call(
        kern,
        grid=(DEC_BATCH // rb, GLA_H),
        in_specs=in_specs,
        out_specs=[pl.BlockSpec((rb, GLA_DV), lambda i, h: (i, h)),
                   pl.BlockSpec((None, rb, 1, GLA_DK, GLA_DV), lambda i, h: (layer, i, h, 0, 0))],
        out_shape=[jax.ShapeDtypeStruct((DEC_BATCH, GLA_VAL), F32),
                   jax.ShapeDtypeStruct((DEPTH, DEC_BATCH, GLA_H, GLA_DK, GLA_DV), F32)],
        scratch_shapes=[pltpu.VMEM((LANE, GLA_DK), F32)],
        input_output_aliases=aliases,
        compiler_params=_cparams(2),
        name="gla_sample",
    )(*args)


def _drop_last_input(kern, *refs, n_in):
    return kern(*refs[:n_in - 1], *refs[n_in:])


def _layernorm(x, w, b):
    mu = jnp.mean(x, axis=-1, keepdims=True)
    xc = x - mu
    var = jnp.mean(xc * xc, axis=-1, keepdims=True)
    return xc * lax.rsqrt(var + EPS) * w + b


def _gmlp_prompt_kernel(u_ref, v_ref, ws_ref, bst_ref, nw_ref, nb_ref, ob_ref, *, nsub):
    ri = lax.broadcasted_iota(jnp.int32, (GM_CHUNK, GM_CHUNK), 0)
    ci = lax.broadcasted_iota(jnp.int32, (GM_CHUNK, GM_CHUNK), 1)
    tril = ci <= ri
    for s in range(nsub):
        rs = slice(s * GM_CHUNK, (s + 1) * GM_CHUNK)
        u = _gelu(u_ref[rs, :])
        v = _layernorm(_gelu(v_ref[rs, :]), nw_ref[...], nb_ref[...])
        for g in range(GM_GROUPS):
            cs = slice(g * GM_GW, (g + 1) * GM_GW)
            wm = jnp.where(tril, ws_ref[g], 0.0)
            mixed = _dot(wm, v[:, cs]) + bst_ref[:, g:g + 1]
            ob_ref[rs, cs] = (u[:, cs] * mixed).astype(ob_ref.dtype)


def _gmlp_prompt(layer, z, gm_ws, gm_bs_t, gm_norm_w, gm_norm_b):
    nsub = 4
    tb = nsub * GM_CHUNK
    kern = functools.partial(_gmlp_prompt_kernel, nsub=nsub)
    return pl.pallas_call(
        kern,
        grid=(BATCH * SEQ // tb,),
        in_specs=[pl.BlockSpec((tb, GM_WIDTH), lambda i: (i, Z_UB // GM_WIDTH)),
                  pl.BlockSpec((tb, GM_WIDTH), lambda i: (i, Z_VB // GM_WIDTH)),
                  pl.BlockSpec((None, GM_GROUPS, GM_CHUNK, GM_CHUNK), lambda i: (layer, 0, 0, 0)),
                  pl.BlockSpec((None, GM_CHUNK, GM_GROUPS), lambda i: (layer, 0, 0)),
                  pl.BlockSpec((None, 1, GM_WIDTH), lambda i: (layer, 0, 0)),
                  pl.BlockSpec((None, 1, GM_WIDTH), lambda i: (layer, 0, 0))],
        out_specs=pl.BlockSpec((tb, GM_WIDTH), lambda i: (i, 0)),
        out_shape=jax.ShapeDtypeStruct((BATCH * SEQ, GM_WIDTH), BF16),
        compiler_params=_cparams(1),
        name="gmlp_prompt",
    )(z, z, gm_ws, gm_bs_t, gm_norm_w.reshape(DEPTH, 1, GM_WIDTH), gm_norm_b.reshape(DEPTH, 1, GM_WIDTH))


def _gmlp_sample_kernel(u_ref, v_ref, w0_ref, b0_ref, nw_ref, nb_ref, ob_ref, vn_ref):
    u = _gelu(u_ref[...])
    v = _layernorm(_gelu(v_ref[...]), nw_ref[...], nb_ref[...])
    vn_ref[...] = v
    ob_ref[...] = u * (w0_ref[...] * v + b0_ref[...])


def _gmlp_sample(layer, z, w0_row, b0_row, gm_norm_w, gm_norm_b):
    full = lambda i: (0, 0)
    lrow = lambda i: (layer, 0, 0)
    return pl.pallas_call(
        _gmlp_sample_kernel,
        grid=(1,),
        in_specs=[pl.BlockSpec((DEC_BATCH, GM_WIDTH), lambda i: (0, Z_UB // GM_WIDTH)),
                  pl.BlockSpec((DEC_BATCH, GM_WIDTH), lambda i: (0, Z_VB // GM_WIDTH)),
                  pl.BlockSpec((None, 1, GM_WIDTH), lrow),
                  pl.BlockSpec((None, 1, GM_WIDTH), lrow),
                  pl.BlockSpec((None, 1, GM_WIDTH), lrow),
                  pl.BlockSpec((None, 1, GM_WIDTH), lrow)],
        out_specs=[pl.BlockSpec((DEC_BATCH, GM_WIDTH), full), pl.BlockSpec((DEC_BATCH, GM_WIDTH), full)],
        out_shape=[jax.ShapeDtypeStruct((DEC_BATCH, GM_WIDTH), F32),
                   jax.ShapeDtypeStruct((DEC_BATCH, GM_WIDTH), F32)],
        compiler_params=_cparams(1),
        name="gmlp_sample",
    )(z, z, w0_row, b0_row, gm_norm_w.reshape(DEPTH, 1, GM_WIDTH), gm_norm_b.reshape(DEPTH, 1, GM_WIDTH))


def _alibi_slope(h):
    return float(2.0 ** (-8.0 * (h + 1) / SWA_HQ))


def _swa_lane_halves(x, half):
    lane = lax.broadcasted_iota(jnp.int32, x.shape, 1)
    own = jnp.where((lane >= half * SWA_HD) & (lane < (half + 1) * SWA_HD), x, 0.0)
    other = pltpu.roll(own, SWA_HD, 1)
    return (own, other) if half == 0 else (other, own)


def _swa_prompt_kernel(q_ref, kc_ref, kp_ref, vc_ref, vp_ref, sink_ref, oc_ref, s_s, p_s):
    n = pl.program_id(1)
    w = WINDOW
    ri = lax.broadcasted_iota(jnp.int32, (w, 2 * w), 0)
    ci = lax.broadcasted_iota(jnp.int32, (w, 2 * w), 1)
    dist_i = w + ri - ci
    valid = (dist_i >= 0) & (dist_i < w) & ((ci >= w) | (n > 0))
    dist = dist_i.astype(F32)
    kcat = jnp.concatenate([kp_ref[...], kc_ref[...]], axis=0)
    vcat = jnp.concatenate([vp_ref[...], vc_ref[...]], axis=0)
    heads = []
    for kv in range(SWA_HKV):
        t, half = kv // 2, kv % 2
        k_lo, k_hi = _swa_lane_halves(kcat[:, t * LANE:(t + 1) * LANE], half)
        q2 = jnp.concatenate([q_ref[:, 2 * kv * LANE:(2 * kv + 1) * LANE],
                              q_ref[:, (2 * kv + 1) * LANE:(2 * kv + 2) * LANE]], axis=0)
        for par, kk in ((0, k_lo), (1, k_hi)):
            s = _dot_nt(q2, kk) * (SWA_HD ** -0.5)
            for e in range(2):
                h = SWA_G * kv + 2 * e + par
                seg = len(heads)
                heads.append(h)
                s_s[seg * w:(seg + 1) * w, :] = jnp.where(
                    valid, s[e * w:(e + 1) * w, :] - _alibi_slope(h) * dist, NEG_BIG)
    s = s_s[...]
    sink = jnp.concatenate([jnp.broadcast_to(sink_ref[h:h + 1, 0:1], (w, 1)) for h in heads], axis=0)
    m = jnp.maximum(jnp.max(s, axis=-1, keepdims=True), sink)
    p = jnp.exp(s - m)
    inv = 1.0 / (jnp.sum(p, axis=-1, keepdims=True) + jnp.exp(sink - m))
    p_s[...] = (p * inv).astype(p_s.dtype)
    for kv in range(SWA_HKV):
        t, half = kv // 2, kv % 2
        v_lo, v_hi = _swa_lane_halves(vcat[:, t * LANE:(t + 1) * LANE], half)
        r0 = SWA_G * kv * w
        o = _dot(p_s[r0:r0 + 2 * w, :], v_lo) + _dot(p_s[r0 + 2 * w:r0 + 4 * w, :], v_hi)
        oc_ref[:, 2 * kv * LANE:(2 * kv + 1) * LANE] = o[:w].astype(oc_ref.dtype)
        oc_ref[:, (2 * kv + 1) * LANE:(2 * kv + 2) * LANE] = o[w:].astype(oc_ref.dtype)


def _swa_prompt(layer, z, sinks_b):
    nb = SEQ // WINDOW
    row = lambda b, n: b * nb + n
    prev = lambda b, n: b * nb + jnp.maximum(n - 1, 0)
    return pl.pallas_call(
        _swa_prompt_kernel,
        grid=(BATCH, nb),
        in_specs=[pl.BlockSpec((WINDOW, SWA_Q), lambda b, n: (row(b, n), Z_QC // SWA_Q)),
                  pl.BlockSpec((WINDOW, SWA_KV), lambda b, n: (row(b, n), Z_KC // SWA_KV)),
                  pl.BlockSpec((WINDOW, SWA_KV), lambda b, n: (prev(b, n), Z_KC // SWA_KV)),
                  pl.BlockSpec((WINDOW, SWA_KV), lambda b, n: (row(b, n), Z_VC // SWA_KV)),
                  pl.BlockSpec((WINDOW, SWA_KV), lambda b, n: (prev(b, n), Z_VC // SWA_KV)),
                  pl.BlockSpec((None, SWA_HQ, LANE), lambda b, n: (layer, 0, 0))],
        out_specs=pl.BlockSpec((WINDOW, SWA_Q), lambda b, n: (row(b, n), 0)),
        out_shape=jax.ShapeDtypeStruct((BATCH * SEQ, SWA_Q), BF16),
        scratch_shapes=[pltpu.VMEM((SWA_HQ * WINDOW, 2 * WINDOW), F32),
                        pltpu.VMEM((SWA_HQ * WINDOW, 2 * WINDOW), BF16)],
        compiler_params=_cparams(2),
        name="swa_prompt",
    )(z, z, z, z, z, sinks_b)


def _swa_sample_kernel(qm_ref, kn_ref, vn_ref, kb_ref, vb_ref, sink_ref, slope_ref, om_ref, *, rb):
    wb = WINDOW
    j = lax.broadcasted_iota(jnp.int32, (SWA_HQ, wb), 1)
    dist = (wb - j).astype(F32)
    ok = j >= 1
    slope = slope_ref[:, 0:1]
    sink = sink_ref[:, 0:1]
    for r in range(rb):
        qm = qm_ref[r]
        s = _dot_nt(qm, kb_ref[r]) * (SWA_HD ** -0.5) - slope * dist
        s = jnp.where(ok, s, NEG_BIG)
        s_self = jnp.sum(_bf(qm).astype(F32) * _bf(kn_ref[r:r + 1, :]).astype(F32), axis=-1,
                         keepdims=True) * (SWA_HD ** -0.5)
        m = jnp.maximum(jnp.maximum(jnp.max(s, axis=-1, keepdims=True), s_self), sink)
        p = jnp.exp(s - m)
        p_self = jnp.exp(s_self - m)
        inv = 1.0 / (jnp.sum(p, axis=-1, keepdims=True) + p_self + jnp.exp(sink - m))
        o = _dot(p * inv, vb_ref[r]) + _bf(p_self * inv).astype(F32) * _bf(vn_ref[r:r + 1, :]).astype(F32)
        om_ref[r] = o


def _swa_sample(layer, qm, z, cache_k, cache_v, sinks_b, slopes_b):
    rb = 8
    kern = functools.partial(_swa_sample_kernel, rb=rb)
    return pl.pallas_call(
        kern,
        grid=(DEC_BATCH // rb,),
        in_specs=[pl.BlockSpec((rb, SWA_HQ, SWA_KV), lambda i: (i, 0, 0)),
                  pl.BlockSpec((rb, SWA_KV), lambda i: (i, Z_KC // SWA_KV)),
                  pl.BlockSpec((rb, SWA_KV), lambda i: (i, Z_VC // SWA_KV)),
                  pl.BlockSpec((None, rb, WINDOW, SWA_KV), lambda i: (layer, i, 0, 0)),
                  pl.BlockSpec((None, rb, WINDOW, SWA_KV), lambda i: (layer, i, 0, 0)),
                  pl.BlockSpec((None, SWA_HQ, LANE), lambda i: (layer, 0, 0)),
                  pl.BlockSpec((SWA_HQ, LANE), lambda i: (0, 0))],
        out_specs=pl.BlockSpec((rb, SWA_HQ, SWA_KV), lambda i: (i, 0, 0)),
        out_shape=jax.ShapeDtypeStruct((DEC_BATCH, SWA_HQ, SWA_KV), F32),
        compiler_params=_cparams(1),
        name="swa_sample",
    )(qm, z, z, cache_k, cache_v, sinks_b, slopes_b)


def _repack_kernel(a_ref, b_ref, o_ref, *, n_plain):
    j = pl.program_id(1)

    @pl.when(j < n_plain)
    def _():
        o_ref[...] = a_ref[...].astype(o_ref.dtype)

    @pl.when(j >= n_plain)
    def _():
        o_ref[...] = jnp.concatenate([a_ref[:, GLA_RANK:], b_ref[:, :GLA_RANK]], axis=1).astype(o_ref.dtype)


def _repack_w_in(w_in):
    tn = 512
    assert LR_COL % tn == 0 and Z_WIDTH % tn == 0
    kern = functools.partial(_repack_kernel, n_plain=LR_COL // tn)
    return pl.pallas_call(
        kern,
        grid=(DEPTH, Z_WIDTH // tn),
        in_specs=[pl.BlockSpec((None, D, tn), lambda l, j: (l, 0, j)),
                  pl.BlockSpec((None, D, LANE), lambda l, j: (l, 0, (j + 1) * (tn // LANE)))],
        out_specs=pl.BlockSpec((None, D, tn), lambda l, j: (l, 0, j)),
        out_shape=jax.ShapeDtypeStruct((DEPTH, D, Z_WIDTH), BF16),
        compiler_params=_cparams(2),
        name="repack_w_in",
    )(w_in, w_in)


def _layer(layer, xp, xs, mod, p, state_gla, cache_k, cache_v, state_out):
    hp = _prep(layer, xp, p["norm1_w"], mod, MOD_SC1, MOD_SH1, sample=False)
    hs = _prep(layer, xs, p["norm1_w"], mod, MOD_SC1, MOD_SH1, sample=True)
    (zp,), (zs,) = _fused_matmul("w_in", layer, [hp], [hs], [(0, p["w_in_r"], 0)], [], _epi_plain, [F32],
                                 Z_WIDTH, 2048, 512)
    (lrp,), (lrs,) = _fused_matmul("w_lr", layer, [hp], [hs], [(0, p["w_lr"], 0)], [], _epi_plain, [F32],
                                   LR_PAD, 2048, LR_PAD)
    oa_p, s_p = _gla_prompt(layer, zp, lrp, p["wgk_pad"], p["b_gk"], p["gla_norm_w"])
    ob_p = _gmlp_prompt(layer, zp, p["gm_ws"], p["gm_bs_t"], p["gm_norm_w"], p["gm_norm_b"])
    oc_p = _swa_prompt(layer, zp, p["sinks_b"])
    z4 = zp.reshape(BATCH, SEQ, Z_WIDTH)
    kp_rows = z4[:, SEQ - WINDOW:, Z_KC:Z_KC + SWA_KV].reshape(BATCH, WINDOW, SWA_HKV, SWA_HD)
    vp_rows = z4[:, SEQ - WINDOW:, Z_VC:Z_VC + SWA_KV].reshape(BATCH, WINDOW, SWA_HKV, SWA_HD)
    oa_s, state_out = _gla_sample(layer, zs, lrs, p["wgk_pad"], p["b_gk"], p["gla_norm_w"], state_gla, state_out)
    ob_s, v_gm = _gmlp_sample(layer, zs, p["gm_w0"], p["gm_b0"], p["gm_norm_w"], p["gm_norm_b"])
    q4 = zs[:, Z_QC:Z_QC + SWA_Q].reshape(DEC_BATCH, SWA_HKV, SWA_G, 1, SWA_HD)
    eye = jnp.eye(SWA_HKV, dtype=F32).reshape(SWA_HKV, 1, SWA_HKV, 1)
    qm = (q4 * eye[None]).reshape(DEC_BATCH, SWA_HQ, SWA_KV)
    om = _swa_sample(layer, qm, zs, cache_k, cache_v, p["sinks_b"], p["slopes_b"])
    om5 = om.reshape(DEC_BATCH, SWA_HKV, SWA_G, SWA_HKV, SWA_HD)
    oc_s = jnp.sum(om5 * eye[None], axis=3).reshape(DEC_BATCH, SWA_Q)
    ks_rows = zs[:, Z_KC:Z_KC + SWA_KV].reshape(DEC_BATCH, 1, SWA_HKV, SWA_HD)
    vs_rows = zs[:, Z_VC:Z_VC + SWA_KV].reshape(DEC_BATCH, 1, SWA_HKV, SWA_HD)
    (mp,), (ms,) = _fused_matmul(
        "merge", layer, [oa_p, ob_p, oc_p], [oa_s, ob_s, oc_s],
        [(0, p["w_pa"], 0), (1, p["w_pb"], 0), (2, p["w_pc"], 0)],
        [("tile", zp, zs, Z_GATES), ("tile", zp, zs, Z_GATES + D), ("tile", zp, zs, Z_GATES + 2 * D)],
        _epi_merge, [BF16], D, 2048, 256)
    (x1p,), (x1s,) = _fused_matmul("w_o", layer, [mp], [ms], [(0, p["w_o"], 0)],
                                   [("tile", xp, xs, 0), ("mod", mod, MOD_G1)], _epi_residual, [F32], D, 2048, 512)
    h2p = _prep(layer, x1p, p["norm2_w"], mod, MOD_SC2, MOD_SH2, sample=False)
    h2s = _prep(layer, x1s, p["norm2_w"], mod, MOD_SC2, MOD_SH2, sample=True)
    (hidp,), (hids,) = _fused_matmul(
        "ffn_in", layer, [h2p], [h2s], [(0, p["w_ffn_in"], 0), (0, p["w_ffn_in"], FFN_HIDDEN)],
        [], _epi_swiglu, [BF16], FFN_HIDDEN, 2048, 512)
    (x2p,), (x2s,) = _fused_matmul("ffn_out", layer, [hidp], [hids], [(0, p["w_ffn_out"], 0)],
                                   [("tile", x1p, x1s, 0), ("mod", mod, MOD_G2)], _epi_residual, [F32],
                                   D, 1024, 512)
    return x2p, x2s, s_p, state_out, kp_rows, vp_rows, ks_rows, vs_rows, v_gm


def kernel(x_prompt, x_sample, c_prompt, c_sample, state_gla, cache_swa_k, cache_swa_v, w_ada, b_ada, norm1_w,
           norm2_w, w_in, w_gk2, b_gk, gla_norm_w, gm_norm_w, gm_norm_b, gm_ws, gm_bs, swa_sinks, w_pa, w_pb,
           w_pc, w_o, w_ffn_in, w_ffn_out, final_norm_w):
    w_in_r = _repack_w_in(w_in)
    w_lr =jnp.pad(w_in[:, :, LR_COL:LR_COL + GLA_RANK], ((0, 0), (0, 0), (0, LR_PAD - GLA_RANK))).astype(BF16)
    p = {
        "norm1_w": norm1_w, "norm2_w": norm2_w, "w_in_r": w_in_r, "w_lr": w_lr,
        "wgk_pad": jnp.pad(w_gk2, ((0, 0), (0, LR_PAD - GLA_RANK), (0, 0))),
        "b_gk": b_gk, "gla_norm_w": gla_norm_w, "gm_norm_w": gm_norm_w, "gm_norm_b": gm_norm_b,
        "gm_ws": gm_ws, "gm_bs_t": jnp.swapaxes(gm_bs, 1, 2),
        "gm_w0": jnp.repeat(gm_ws[:, :, 0, 0], GM_GW, axis=1).reshape(DEPTH, 1, GM_WIDTH),
        "gm_b0": jnp.repeat(gm_bs[:, :, 0], GM_GW, axis=1).reshape(DEPTH, 1, GM_WIDTH),
        "sinks_b": jnp.broadcast_to(swa_sinks[:, :, None], (DEPTH, SWA_HQ, LANE)),
        "slopes_b": jnp.broadcast_to(
            jnp.asarray([_alibi_slope(h) for h in range(SWA_HQ)], F32)[:, None], (SWA_HQ, LANE)),
        "w_pa": w_pa, "w_pb": w_pb, "w_pc": w_pc, "w_o": w_o, "w_ffn_in": w_ffn_in, "w_ffn_out": w_ffn_out,
    }
    c_all = jnp.concatenate([c_sample, c_prompt, jnp.zeros((MOD_ROWS - DEC_BATCH - BATCH, D), F32)], axis=0)
    mod = _ada(c_all, w_ada, b_ada)

    xp = x_prompt.reshape(BATCH * SEQ, D)
    xs = x_sample.reshape(DEC_BATCH, D)
    cache_k = cache_swa_k.reshape(DEPTH, DEC_BATCH, WINDOW, SWA_KV)
    cache_v = cache_swa_v.reshape(DEPTH, DEC_BATCH, WINDOW, SWA_KV)
    gla_p, kp, vp, ksm, vsm, gmv = [], [], [], [], [], []
    state_out = None
    for l in range(DEPTH):
        xp, xs, s_p, state_out, k_p, v_p, k_s, v_s, gv = _layer(l, xp, xs, mod, p, state_gla, cache_k, cache_v,
                                                                state_out)
        gla_p.append(s_p)
        kp.append(k_p)
        vp.append(v_p)
        ksm.append(k_s)
        vsm.append(v_s)
        gmv.append(gv.reshape(DEC_BATCH, 1, GM_WIDTH))
    y_prompt = _final_norm(xp, final_norm_w).reshape(BATCH, SEQ, D)
    y_sample = _final_norm(xs, final_norm_w).reshape(DEC_BATCH, 1, D)
    return (y_prompt, y_sample, jnp.stack(gla_p), state_out, jnp.stack(kp), jnp.stack(vp),
            jnp.stack(ksm), jnp.stack(vsm), jnp.stack(gmv))
```

```python
import functools

import jax
import jax.numpy as jnp
import numpy as np
from jax import lax
from jax.experimental import pallas as pl
from jax.experimental.pallas import tpu as pltpu

F32 = jnp.float32
BF16 = jnp.bfloat16

D = 2048
BATCH, SEQ = 2, 4096
DEPTH = 2
DEC_BATCH = 128
GLA_H, GLA_DK, GLA_DV = 4, 256, 512
GLA_KEY, GLA_VAL = GLA_H * GLA_DK, GLA_H * GLA_DV
GLA_RANK = 16
GLA_CHUNK = 16
GLA_SC = 128
GM_WIDTH, GM_GROUPS, GM_CHUNK = 1024, 4, 128
GM_GW = GM_WIDTH // GM_GROUPS
SWA_HQ, SWA_HKV, SWA_HD, WINDOW = 16, 4, 64, 128
SWA_G = SWA_HQ // SWA_HKV
SWA_Q, SWA_KV = SWA_HQ * SWA_HD, SWA_HKV * SWA_HD
FFN_HIDDEN = 5632
EPS = 1e-6
NEG_BIG = -1e30

Z_QA, Z_KA, Z_VA, Z_GA = 0, 1024, 2048, 4096
Z_UB, Z_VB = 6144, 7168
Z_QC, Z_KC, Z_VC = 8192, 9216, 9472
Z_GATES = 9728
Z_WIDTH = 15872
LR_COL = 6144
LANE = 128
LR_PAD = LANE

MOD_SH1, MOD_SC1, MOD_G1, MOD_SH2, MOD_SC2, MOD_G2 = range(6)
MOD_ROWS = DEC_BATCH + 8

VMEM_LIMIT = 56 * 1024 * 1024


def _cparams(n_axes):
    return pltpu.CompilerParams(dimension_semantics=("arbitrary",) * n_axes,
                                vmem_limit_bytes=VMEM_LIMIT)


def _bf(x):
    return x if x.dtype == BF16 else x.astype(BF16)


def _dot(a, b):
    return jnp.dot(_bf(a), _bf(b), preferred_element_type=F32)


def _dot_nt(a, b):
    return lax.dot_general(_bf(a), _bf(b), (((1,), (1,)), ((), ())), preferred_element_type=F32)


def _silu(x):
    return x * (1.0 / (1.0 + jnp.exp(-x)))


def _sigmoid(x):
    return 1.0 / (1.0 + jnp.exp(-x))


def _gelu(x):
    return 0.5 * x * (1.0 + jnp.tanh(np.sqrt(2.0 / np.pi).astype(np.float32) * (x + 0.044715 * (x * x * x))))


def _mod_spec_prompt(layer, chunk, tm, tn):
    cb, bpb = chunk * D // tn, SEQ // tm
    return pl.BlockSpec((None, None, 1, tn), lambda i, j: (layer, DEC_BATCH + i // bpb, 0, j + cb))


def _mod_spec_sample(layer, chunk, tn, jmap):
    cb = chunk * D // tn
    return pl.BlockSpec((None, DEC_BATCH, tn), lambda i, j: (layer, 0, jmap(i, j) + cb))


def _ada_kernel(c_ref, w_ref, b_ref, o_ref):
    o_ref[...] = _dot(_silu(c_ref[...]), w_ref[...]) + b_ref[...]


def _ada(c_all, w_ada, b_ada):
    tn = 1024
    return pl.pallas_call(
        _ada_kernel,
        grid=(DEPTH, 6 * D // tn),
        in_specs=[pl.BlockSpec((MOD_ROWS, D), lambda l, j: (0, 0)),
                  pl.BlockSpec((None, D, tn), lambda l, j: (l, 0, j)),
                  pl.BlockSpec((None, 1, tn), lambda l, j: (l, 0, j))],
        out_specs=pl.BlockSpec((None, MOD_ROWS, tn), lambda l, j: (l, 0, j)),
        out_shape=jax.ShapeDtypeStruct((DEPTH, MOD_ROWS, 6 * D), F32),
        compiler_params=_cparams(2),
        name="ada",
    )(c_all, w_ada, b_ada.reshape(DEPTH, 1, 6 * D))


def _prep_kernel(x_ref, nw_ref, sc_ref, sh_ref, o_ref):
    x = x_ref[...]
    y = x * lax.rsqrt(jnp.mean(x * x, axis=-1, keepdims=True) + EPS) * nw_ref[...]
    o_ref[...] = (y * (1.0 + sc_ref[...]) + sh_ref[...]).astype(o_ref.dtype)


def _prep(layer, x, norm_w, mod, sc_chunk, sh_chunk, sample):
    rows = x.shape[0]
    if sample:
        tm, modop = rows, mod
        mod_specs = [_mod_spec_sample(layer, c, D, lambda i, j: j) for c in (sc_chunk, sh_chunk)]
    else:
        tm, modop = 512, mod.reshape(DEPTH, MOD_ROWS, 1, 6 * D)
        mod_specs = [_mod_spec_prompt(layer, c, tm, D) for c in (sc_chunk, sh_chunk)]
    return pl.pallas_call(
        _prep_kernel,
        grid=(rows // tm, 1),
        in_specs=[pl.BlockSpec((tm, D), lambda i, j: (i, 0)),
                  pl.BlockSpec((None, 1, D), lambda i, j: (layer, 0, 0))] + mod_specs,
        out_specs=pl.BlockSpec((tm, D), lambda i, j: (i, 0)),
        out_shape=jax.ShapeDtypeStruct((rows, D), BF16),
        compiler_params=_cparams(2),
        name="prep",
    )(x, norm_w.reshape(DEPTH, 1, D), modop, modop)


def _final_norm_kernel(x_ref, nw_ref, o_ref):
    x = x_ref[...]
    o_ref[...] = x * lax.rsqrt(jnp.mean(x * x, axis=-1, keepdims=True) + EPS) * nw_ref[...]


def _final_norm(x, w):
    rows = x.shape[0]
    tm = min(rows, 512)
    return pl.pallas_call(
        _final_norm_kernel,
        grid=(rows // tm,),
        in_specs=[pl.BlockSpec((tm, D), lambda i: (i, 0)), pl.BlockSpec((1, D), lambda i: (0, 0))],
        out_specs=pl.BlockSpec((tm, D), lambda i: (i, 0)),
        out_shape=jax.ShapeDtypeStruct((rows, D), F32),
        compiler_params=_cparams(1),
        name="final_norm",
    )(x, w.reshape(1, D))


def _mm_kernel(*refs, n_a, term_a, term_t, n_extra, n_out, epilogue):
    sizes = (n_a, n_a, len(term_a), n_extra, n_extra, n_out, n_out)
    groups, pos = [], 0
    for n in sizes:
        groups.append(refs[pos:pos + n])
        pos += n
    a_p, a_s, w_refs, e_p, e_s, o_p, o_s = groups
    w_vals = [_bf(w[...]) for w in w_refs]

    def run(a_refs, e_refs, o_refs):
        a_vals = [_bf(a[...]) for a in a_refs]
        dots = [(_dot_nt if t else _dot)(a_vals[ai], w) for ai, t, w in zip(term_a, term_t, w_vals)]
        outs = epilogue(dots, [e[...] for e in e_refs])
        for o_ref, o in zip(o_refs, outs):
            o_ref[...] = o.astype(o_ref.dtype)

    run(a_p, e_p, o_p)

    @pl.when(pl.program_id(0) == 0)
    def _():
        run(a_s, e_s, o_s)


def _fused_matmul(name, layer, a_p, a_s, terms, extras, epilogue, out_dtypes, n_cols, tm, tn):
    rows_p, rows_s = a_p[0].shape[0], a_s[0].shape[0]
    nj = n_cols // tn
    grid = (rows_p // tm, nj)
    sj = lambda i, j: jnp.where(i == 0, j, nj - 1)
    args, in_specs = [], []
    for a in a_p:
        args.append(a)
        in_specs.append(pl.BlockSpec((tm, a.shape[1]), lambda i, j: (i, 0), pipeline_mode=pl.Buffered(1)))
    for a in a_s:
        args.append(a)
        in_specs.append(pl.BlockSpec((rows_s, a.shape[1]), lambda i, j: (0, 0)))
    for ai, w, col0, transposed in terms:
        assert col0 % tn == 0 and w.shape[-1 if transposed else -2] == a_p[ai].shape[1]
        args.append(w)
        if transposed:
            in_specs.append(pl.BlockSpec((None, tn, w.shape[-1]), lambda i, j, cb=col0 // tn: (layer, j + cb, 0)))
        else:
            in_specs.append(pl.BlockSpec((None, w.shape[-2], tn), lambda i, j, cb=col0 // tn: (layer, 0, j + cb)))
    s_args, s_specs = [], []
    for ex in extras:
        if ex[0] == "tile":
            _, arr_p, arr_s, col0 = ex
            assert col0 % tn == 0
            args.append(arr_p)
            in_specs.append(pl.BlockSpec((tm, tn), lambda i, j, cb=col0 // tn: (i, j + cb)))
            s_args.append(arr_s)
            s_specs.append(pl.BlockSpec((rows_s, tn), lambda i, j, cb=col0 // tn: (0, sj(i, j) + cb)))
        else:
            _, mod, chunk = ex
            args.append(mod.reshape(DEPTH, MOD_ROWS, 1, 6 * D))
            in_specs.append(_mod_spec_prompt(layer, chunk, tm, tn))
            s_args.append(mod)
            s_specs.append(_mod_spec_sample(layer, chunk, tn, sj))
    kern = functools.partial(_mm_kernel, n_a=len(a_p), term_a=tuple(t[0] for t in terms),
                             term_t=tuple(t[3] for t in terms), n_extra=len(extras), n_out=len(out_dtypes),
                             epilogue=epilogue)
    res = pl.pallas_call(
        kern,
        grid=grid,
        in_specs=in_specs + s_specs,
        out_specs=([pl.BlockSpec((tm, tn), lambda i, j: (i, j)) for _ in out_dtypes]
                   + [pl.BlockSpec((rows_s, tn), lambda i, j: (0, sj(i, j))) for _ in out_dtypes]),
        out_shape=([jax.ShapeDtypeStruct((rows_p, n_cols), dt) for dt in out_dtypes]
                   + [jax.ShapeDtypeStruct((rows_s, n_cols), dt) for dt in out_dtypes]),
        compiler_params=_cparams(2),
        name=name,
    )(*args, *s_args)
    return res[:len(out_dtypes)], res[len(out_dtypes):]


def _epi_plain(dots, ex):
    return [dots[0]]


def _epi_merge(dots, ex):
    return [_sigmoid(ex[0]) * dots[0] + _sigmoid(ex[1]) * dots[1] + _sigmoid(ex[2]) * dots[2]]


def _epi_residual(dots, ex):
    return [ex[0] + ex[1] * dots[0]]


def _epi_swiglu(dots, ex):
    return [_silu(dots[0]) * dots[1]]


def _log_sigmoid(u):
    return -(jnp.maximum(-u, 0.0) + jnp.log1p(jnp.exp(-jnp.abs(u))))


def _gla_prompt_kernel(q_ref, k_ref, v_ref, ga_ref, lr_ref, wgk_ref, bgk_ref, nw_ref,
                       oa_ref, sfin_ref, st_s, vt_s, o_s, *, tb):
    t = pl.program_id(1)
    nsc = tb // GLA_SC
    nd = GLA_SC // GLA_CHUNK - 1

    @pl.when(t == 0)
    def _():
        st_s[...] = jnp.zeros_like(st_s)

    u = _dot(lr_ref[...], wgk_ref[...]) + bgk_ref[...]
    gk = _log_sigmoid(u) * (1.0 / 16.0)
    row = lax.broadcasted_iota(jnp.int32, gk.shape, 0)
    pos_c = row % GLA_CHUNK
    pos_s = row % GLA_SC
    b = gk
    for s in (1, 2, 4, 8):
        b = b + jnp.where(pos_c >= s, pltpu.roll(b, s, 0), 0.0)
    blb = jnp.where(pos_c == GLA_CHUNK - 1, b, 0.0)
    for s in (1, 2, 4, 8):
        blb = blb + pltpu.roll(blb, tb - s, 0)
    acc = jnp.where(pos_s >= GLA_CHUNK, pltpu.roll(blb, GLA_CHUNK, 0), 0.0)
    for s in (16, 32, 64):
        acc = acc + jnp.where(pos_s >= s, pltpu.roll(acc, s, 0), 0.0)
    bs = b + acc

    q = q_ref[...] * (GLA_DK ** -0.5)
    k = k_ref[...]
    qin = _bf(q * jnp.exp(b))
    kout = _bf(k * jnp.exp(-b))
    kd = k * jnp.exp(blb - b)
    qsc = _bf(q * jnp.exp(bs))
    vt_s[...] = v_ref[...].T

    ri = lax.broadcasted_iota(jnp.int32, (tb, tb), 0)
    ci = lax.broadcasted_iota(jnp.int32, (tb, tb), 1)
    delta = jnp.where(ri // GLA_SC == ci // GLA_SC, ri // GLA_CHUNK - ci // GLA_CHUNK, -1)
    m_intra = (delta == 0) & (ci <= ri)
    kds = [_bf(kd)]
    for d in range(1, nd):
        kd = kd * jnp.exp(pltpu.roll(blb, tb - GLA_CHUNK * d, 0))
        kds.append(_bf(kd))
    for h in range(GLA_H):
        ks = slice(h * GLA_DK, (h + 1) * GLA_DK)
        vs = slice(h * GLA_DV, (h + 1) * GLA_DV)
        a = jnp.where(m_intra, _dot_nt(qin[:, ks], kout[:, ks]), 0.0)
        for d in range(nd):
            a = jnp.where(delta == d + 1, _dot_nt(qin[:, ks], kds[d][:, ks]), a)
        o_s[:, vs] = _dot(a, v_ref[:, vs])

    for sc in range(nsc):
        rows = slice(sc * GLA_SC, (sc + 1) * GLA_SC)
        last = bs[(sc + 1) * GLA_SC - 1:(sc + 1) * GLA_SC, :]
        k2 = _bf(k[rows, :] * jnp.exp(last - bs[rows, :]))
        elast = jnp.exp(last)
        for h in range(GLA_H):
            ks = slice(h * GLA_DK, (h + 1) * GLA_DK)
            vs = slice(h * GLA_DV, (h + 1) * GLA_DV)
            st = st_s[h]
            o_s[rows, vs] += _dot_nt(qsc[rows, ks], st)
            st_s[h] = st * elast[:, ks] + _dot(vt_s[vs, rows], k2[:, ks])

    nw = nw_ref[...]
    for h in range(GLA_H):
        vs = slice(h * GLA_DV, (h + 1) * GLA_DV)
        o = o_s[:, vs]
        y = o * lax.rsqrt(jnp.mean(o * o, axis=-1, keepdims=True) + EPS) * nw
        oa_ref[:, vs] = (y * _silu(ga_ref[:, vs])).astype(oa_ref.dtype)

    @pl.when(t == pl.num_programs(1) - 1)
    def _():
        for h in range(GLA_H):
            sfin_ref[h] = st_s[h].T


def _gla_prompt(layer, z, lr, wgk_pad, b_gk, gla_norm_w):
    tb = 256
    nt = SEQ // tb
    row = lambda b, t: b * nt + t
    kern = functools.partial(_gla_prompt_kernel, tb=tb)
    return pl.pallas_call(
        kern,
        grid=(BATCH, nt),
        in_specs=[pl.BlockSpec((tb, GLA_KEY), lambda b, t: (row(b, t), Z_QA // GLA_KEY)),
                  pl.BlockSpec((tb, GLA_KEY), lambda b, t: (row(b, t), Z_KA // GLA_KEY)),
                  pl.BlockSpec((tb, GLA_VAL), lambda b, t: (row(b, t), Z_VA // GLA_VAL)),
                  pl.BlockSpec((tb, GLA_VAL), lambda b, t: (row(b, t), Z_GA // GLA_VAL)),
                  pl.BlockSpec((tb, LR_PAD), lambda b, t: (row(b, t), 0)),
                  pl.BlockSpec((None, LR_PAD, GLA_KEY), lambda b, t: (layer, 0, 0)),
                  pl.BlockSpec((None, 1, GLA_KEY), lambda b, t: (layer, 0, 0)),
                  pl.BlockSpec((None, 1, GLA_DV), lambda b, t: (layer, 0, 0))],
        out_specs=[pl.BlockSpec((tb, GLA_VAL), lambda b, t: (row(b, t), 0)),
                   pl.BlockSpec((None, GLA_H, GLA_DK, GLA_DV), lambda b, t: (b, 0, 0, 0))],
        out_shape=[jax.ShapeDtypeStruct((BATCH * SEQ, GLA_VAL), BF16),
                   jax.ShapeDtypeStruct((BATCH, GLA_H, GLA_DK, GLA_DV), F32)],
        scratch_shapes=[pltpu.VMEM((GLA_H, GLA_DV, GLA_DK), F32),
                        pltpu.VMEM((GLA_VAL, tb), F32),
                        pltpu.VMEM((tb, GLA_VAL), F32)],
        compiler_params=_cparams(2),
        name="gla_prompt",
    )(z, z, z, z, lr, wgk_pad, b_gk.reshape(DEPTH, 1, GLA_KEY), gla_norm_w.reshape(DEPTH, 1, GLA_DV))


def _gla_sample_kernel(q_ref, k_ref, v_ref, ga_ref, lr_ref, wgk_ref, bgk_ref, nw_ref, s_ref,
                       oa_ref, snew_ref, x_s, *, rb):
    u = _dot(lr_ref[...], wgk_ref[...]) + bgk_ref[...]
    eg = jnp.exp(_log_sigmoid(u) * (1.0 / 16.0))
    x_s[...] = jnp.zeros_like(x_s)
    x_s[0:rb, :] = q_ref[...] * (GLA_DK ** -0.5)
    x_s[rb:2 * rb, :] = k_ref[...]
    x_s[2 * rb:3 * rb, :] = eg
    xt = x_s[...].T
    nw = nw_ref[...]
    for r in range(rb):
        qc = xt[:, r:r + 1]
        kc = xt[:, rb + r:rb + r + 1]
        gc = xt[:, 2 * rb + r:2 * rb + r + 1]
        s_new = gc * s_ref[r, 0] + kc * v_ref[r:r + 1, :]
        snew_ref[r, 0] = s_new
        o = jnp.sum(qc * s_new, axis=0, keepdims=True)
        y = o * lax.rsqrt(jnp.mean(o * o, axis=-1, keepdims=True) + EPS) * nw
        oa_ref[r:r + 1, :] = y * _silu(ga_ref[r:r + 1, :])


def _gla_sample(layer, z, lr, wgk_pad, b_gk, gla_norm_w, state_gla, state_out):
    rb = 8
    kern = functools.partial(_gla_sample_kernel, rb=rb)
    in_specs = [pl.BlockSpec((rb, GLA_DK), lambda i, h: (i, Z_QA // GLA_DK + h)),
                pl.BlockSpec((rb, GLA_DK), lambda i, h: (i, Z_KA // GLA_DK + h)),
                pl.BlockSpec((rb, GLA_DV), lambda i, h: (i, Z_VA // GLA_DV + h)),
                pl.BlockSpec((rb, GLA_DV), lambda i, h: (i, Z_GA // GLA_DV + h)),
                pl.BlockSpec((rb, LR_PAD), lambda i, h: (i, 0)),
                pl.BlockSpec((None, LR_PAD, GLA_DK), lambda i, h: (layer, 0, h)),
                pl.BlockSpec((None, 1, GLA_DK), lambda i, h: (layer, 0, h)),
                pl.BlockSpec((None, 1, GLA_DV), lambda i, h: (layer, 0, 0)),
                pl.BlockSpec((None, rb, 1, GLA_DK, GLA_DV), lambda i, h: (layer, i, h, 0, 0))]
    args = [z, z, z, z, lr, wgk_pad, b_gk.reshape(DEPTH, 1, GLA_KEY), gla_norm_w.reshape(DEPTH, 1, GLA_DV),
            state_gla]
    aliases = {}
    if state_out is not None:
        in_specs.append(pl.BlockSpec(memory_space=pl.ANY))
        args.append(state_out)
        aliases = {len(args) - 1: 1}
        kern = functools.partial(_drop_last_input, kern, n_in=len(args))
    return pl.pallas_call(
        kern,
        grid=(DEC_BATCH // rb, GLA_H),
        in_specs=in_specs,
        out_specs=[pl.BlockSpec((rb, GLA_DV), lambda i, h: (i, h)),
                   pl.BlockSpec((None, rb, 1, GLA_DK, GLA_DV), lambda i, h: (layer, i, h, 0, 0))],
        out_shape=[jax.ShapeDtypeStruct((DEC_BATCH, GLA_VAL), F32),
                   jax.ShapeDtypeStruct((DEPTH, DEC_BATCH, GLA_H, GLA_DK, GLA_DV), F32)],
        scratch_shapes=[pltpu.VMEM((LANE, GLA_DK), F32)],
        input_output_aliases=aliases,
        compiler_params=_cparams(2),
        name="gla_sample",
    )(*args)


def _drop_last_input(kern, *refs, n_in):
    return kern(*refs[:n_in - 1], *refs[n_in:])


def _layernorm(x, w, b):
    mu = jnp.mean(x, axis=-1, keepdims=True)
    xc = x - mu
    var = jnp.mean(xc * xc, axis=-1, keepdims=True)
    return xc * lax.rsqrt(var + EPS) * w + b


def _gmlp_prompt_kernel(u_ref, v_ref, ws_ref, bst_ref, nw_ref, nb_ref, ob_ref, *, nsub):
    ri = lax.broadcasted_iota(jnp.int32, (GM_CHUNK, GM_CHUNK), 0)
    ci = lax.broadcasted_iota(jnp.int32, (GM_CHUNK, GM_CHUNK), 1)
    tril = ci <= ri
    for s in range(nsub):
        rs = slice(s * GM_CHUNK, (s + 1) * GM_CHUNK)
        u = _gelu(u_ref[rs, :])
        v = _layernorm(_gelu(v_ref[rs, :]), nw_ref[...], nb_ref[...])
        for g in range(GM_GROUPS):
            cs = slice(g * GM_GW, (g + 1) * GM_GW)
            wm = jnp.where(tril, ws_ref[g], 0.0)
            mixed = _dot(wm, v[:, cs]) + bst_ref[:, g:g + 1]
            ob_ref[rs, cs] = (u[:, cs] * mixed).astype(ob_ref.dtype)


def _gmlp_prompt(layer, z, gm_ws, gm_bs_t, gm_norm_w, gm_norm_b):
    nsub = 4
    tb = nsub * GM_CHUNK
    kern = functools.partial(_gmlp_prompt_kernel, nsub=nsub)
    return pl.pallas_call(
        kern,
        grid=(BATCH * SEQ // tb,),
        in_specs=[pl.BlockSpec((tb, GM_WIDTH), lambda i: (i, Z_UB // GM_WIDTH)),
                  pl.BlockSpec((tb, GM_WIDTH), lambda i: (i, Z_VB // GM_WIDTH)),
                  pl.BlockSpec((None, GM_GROUPS, GM_CHUNK, GM_CHUNK), lambda i: (layer, 0, 0, 0)),
                  pl.BlockSpec((None, GM_CHUNK, GM_GROUPS), lambda i: (layer, 0, 0)),
                  pl.BlockSpec((None, 1, GM_WIDTH), lambda i: (layer, 0, 0)),
                  pl.BlockSpec((None, 1, GM_WIDTH), lambda i: (layer, 0, 0))],
        out_specs=pl.BlockSpec((tb, GM_WIDTH), lambda i: (i, 0)),
        out_shape=jax.ShapeDtypeStruct((BATCH * SEQ, GM_WIDTH), BF16),
        compiler_params=_cparams(1),
        name="gmlp_prompt",
    )(z, z, gm_ws, gm_bs_t, gm_norm_w.reshape(DEPTH, 1, GM_WIDTH), gm_norm_b.reshape(DEPTH, 1, GM_WIDTH))


def _gmlp_sample_kernel(u_ref, v_ref, w0_ref, b0_ref, nw_ref, nb_ref, ob_ref, vn_ref):
    u = _gelu(u_ref[...])
    v = _layernorm(_gelu(v_ref[...]), nw_ref[...], nb_ref[...])
    vn_ref[...] = v
    ob_ref[...] = u * (w0_ref[...] * v + b0_ref[...])


def _gmlp_sample(layer, z, w0_row, b0_row, gm_norm_w, gm_norm_b):
    full = lambda i: (0, 0)
    lrow = lambda i: (layer, 0, 0)
    return pl.pallas_call(
        _gmlp_sample_kernel,
        grid=(1,),
        in_specs=[pl.BlockSpec((DEC_BATCH, GM_WIDTH), lambda i: (0, Z_UB // GM_WIDTH)),
                  pl.BlockSpec((DEC_BATCH, GM_WIDTH), lambda i: (0, Z_VB // GM_WIDTH)),
                  pl.BlockSpec((None, 1, GM_WIDTH), lrow),
                  pl.BlockSpec((None, 1, GM_WIDTH), lrow),
                  pl.BlockSpec((None, 1, GM_WIDTH), lrow),
                  pl.BlockSpec((None, 1, GM_WIDTH), lrow)],
        out_specs=[pl.BlockSpec((DEC_BATCH, GM_WIDTH), full), pl.BlockSpec((DEC_BATCH, GM_WIDTH), full)],
        out_shape=[jax.ShapeDtypeStruct((DEC_BATCH, GM_WIDTH), F32),
                   jax.ShapeDtypeStruct((DEC_BATCH, GM_WIDTH), F32)],
        compiler_params=_cparams(1),
        name="gmlp_sample",
    )(z, z, w0_row, b0_row, gm_norm_w.reshape(DEPTH, 1, GM_WIDTH), gm_norm_b.reshape(DEPTH, 1, GM_WIDTH))


def _alibi_slope(h):
    return float(2.0 ** (-8.0 * (h + 1) / SWA_HQ))


def _swa_lane_halves(x, half):
    lane = lax.broadcasted_iota(jnp.int32, x.shape, 1)
    own = jnp.where((lane >= half * SWA_HD) & (lane < (half + 1) * SWA_HD), x, 0.0)
    other = pltpu.roll(own, SWA_HD, 1)
    return (own, other) if half == 0 else (other, own)


def _swa_prompt_kernel(q_ref, kc_ref, kp_ref, vc_ref, vp_ref, sink_ref, oc_ref, s_s, p_s):
    n = pl.program_id(1)
    w = WINDOW
    ri = lax.broadcasted_iota(jnp.int32, (w, 2 * w), 0)
    ci = lax.broadcasted_iota(jnp.int32, (w, 2 * w), 1)
    dist_i = w + ri - ci
    valid = (dist_i >= 0) & (dist_i < w) & ((ci >= w) | (n > 0))
    dist = dist_i.astype(F32)
    kcat = jnp.concatenate([kp_ref[...], kc_ref[...]], axis=0)
    vcat = jnp.concatenate([vp_ref[...], vc_ref[...]], axis=0)
    heads = []
    for kv in range(SWA_HKV):
        t, half = kv // 2, kv % 2
        k_lo, k_hi = _swa_lane_halves(kcat[:, t * LANE:(t + 1) * LANE], half)
        q2 = jnp.concatenate([q_ref[:, 2 * kv * LANE:(2 * kv + 1) * LANE],
                              q_ref[:, (2 * kv + 1) * LANE:(2 * kv + 2) * LANE]], axis=0)
        for par, kk in ((0, k_lo), (1, k_hi)):
            s = _dot_nt(q2, kk) * (SWA_HD ** -0.5)
            for e in range(2):
                h = SWA_G * kv + 2 * e + par
                seg = len(heads)
                heads.append(h)
                s_s[seg * w:(seg + 1) * w, :] = jnp.where(
                    valid, s[e * w:(e + 1) * w, :] - _alibi_slope(h) * dist, NEG_BIG)
    s = s_s[...]
    sink = jnp.concatenate([jnp.broadcast_to(sink_ref[h:h + 1, 0:1], (w, 1)) for h in heads], axis=0)
    m = jnp.maximum(jnp.max(s, axis=-1, keepdims=True), sink)
    p = jnp.exp(s - m)
    inv = 1.0 / (jnp.sum(p, axis=-1, keepdims=True) + jnp.exp(sink - m))
    p_s[...] = (p * inv).astype(p_s.dtype)
    for kv in range(SWA_HKV):
        t, half = kv // 2, kv % 2
        v_lo, v_hi = _swa_lane_halves(vcat[:, t * LANE:(t + 1) * LANE], half)
        r0 = SWA_G * kv * w
        o = _dot(p_s[r0:r0 + 2 * w, :], v_lo) + _dot(p_s[r0 + 2 * w:r0 + 4 * w, :], v_hi)
        oc_ref[:, 2 * kv * LANE:(2 * kv + 1) * LANE] = o[:w].astype(oc_ref.dtype)
        oc_ref[:, (2 * kv + 1) * LANE:(2 * kv + 2) * LANE] = o[w:].astype(oc_ref.dtype)


def _swa_prompt(layer, z, sinks_b):
    nb = SEQ // WINDOW
    row = lambda b, n: b * nb + n
    prev = lambda b, n: b * nb + jnp.maximum(n - 1, 0)
    return pl.pallas_call(
        _swa_prompt_kernel,
        grid=(BATCH, nb),
        in_specs=[pl.BlockSpec((WINDOW, SWA_Q), lambda b, n: (row(b, n), Z_QC // SWA_Q)),
                  pl.BlockSpec((WINDOW, SWA_KV), lambda b, n: (row(b, n), Z_KC // SWA_KV)),
                  pl.BlockSpec((WINDOW, SWA_KV), lambda b, n: (prev(b, n), Z_KC // SWA_KV)),
                  pl.BlockSpec((WINDOW, SWA_KV), lambda b, n: (row(b, n), Z_VC // SWA_KV)),
                  pl.BlockSpec((WINDOW, SWA_KV), lambda b, n: (prev(b, n), Z_VC // SWA_KV)),
                  pl.BlockSpec((None, SWA_HQ, LANE), lambda b, n: (layer, 0, 0))],
        out_specs=pl.BlockSpec((WINDOW, SWA_Q), lambda b, n: (row(b, n), 0)),
        out_shape=jax.ShapeDtypeStruct((BATCH * SEQ, SWA_Q), BF16),
        scratch_shapes=[pltpu.VMEM((SWA_HQ * WINDOW, 2 * WINDOW), F32),
                        pltpu.VMEM((SWA_HQ * WINDOW, 2 * WINDOW), BF16)],
        compiler_params=_cparams(2),
        name="swa_prompt",
    )(z, z, z, z, z, sinks_b)


def _swa_sample_kernel(qm_ref, kn_ref, vn_ref, kb_ref, vb_ref, sink_ref, slope_ref, om_ref, *, rb):
    wb = WINDOW
    j = lax.broadcasted_iota(jnp.int32, (SWA_HQ, wb), 1)
    dist = (wb - j).astype(F32)
    ok = j >= 1
    slope = slope_ref[:, 0:1]
    sink = sink_ref[:, 0:1]
    for r in range(rb):
        qm = qm_ref[r]
        s = _dot_nt(qm, kb_ref[r]) * (SWA_HD ** -0.5) - slope * dist
        s = jnp.where(ok, s, NEG_BIG)
        s_self = jnp.sum(_bf(qm).astype(F32) * _bf(kn_ref[r:r + 1, :]).astype(F32), axis=-1,
                         keepdims=True) * (SWA_HD ** -0.5)
        m = jnp.maximum(jnp.maximum(jnp.max(s, axis=-1, keepdims=True), s_self), sink)
        p = jnp.exp(s - m)
        p_self = jnp.exp(s_self - m)
        inv = 1.0 / (jnp.sum(p, axis=-1, keepdims=True) + p_self + jnp.exp(sink - m))
        o = _dot(p * inv, vb_ref[r]) + _bf(p_self * inv).astype(F32) * _bf(vn_ref[r:r + 1, :]).astype(F32)
        om_ref[r] = o


def _swa_sample(layer, qm, z, cache_k, cache_v, sinks_b, slopes_b):
    rb = 8
    kern = functools.partial(_swa_sample_kernel, rb=rb)
    return pl.pallas_call(
        kern,
        grid=(DEC_BATCH // rb,),
        in_specs=[pl.BlockSpec((rb, SWA_HQ, SWA_KV), lambda i: (i, 0, 0)),
                  pl.BlockSpec((rb, SWA_KV), lambda i: (i, Z_KC // SWA_KV)),
                  pl.BlockSpec((rb, SWA_KV), lambda i: (i, Z_VC // SWA_KV)),
                  pl.BlockSpec((None, rb, WINDOW, SWA_KV), lambda i: (layer, i, 0, 0)),
                  pl.BlockSpec((None, rb, WINDOW, SWA_KV), lambda i: (layer, i, 0, 0)),
                  pl.BlockSpec((None, SWA_HQ, LANE), lambda i: (layer, 0, 0)),
                  pl.BlockSpec((SWA_HQ, LANE), lambda i: (0, 0))],
        out_specs=pl.BlockSpec((rb, SWA_HQ, SWA_KV), lambda i: (i, 0, 0)),
        out_shape=jax.ShapeDtypeStruct((DEC_BATCH, SWA_HQ, SWA_KV), F32),
        compiler_params=_cparams(1),
        name="swa_sample",
    )(qm, z, z, cache_k, cache_v, sinks_b, slopes_b)


def _repack_kernel(a_ref, b_ref, o_ref, *, n_plain):
    j = pl.program_id(1)

    @pl.when(j < n_plain)
    def _():
        o_ref[...] = a_ref[...].T.astype(o_ref.dtype)

    @pl.when(j >= n_plain)
    def _():
        o_ref[...] = jnp.concatenate([a_ref[GLA_RANK:, :], b_ref[...]], axis=0).T.astype(o_ref.dtype)


def _repack_w_in_t(w_in_t):
    tn = 512
    assert LR_COL % tn == 0 and Z_WIDTH % tn == 0 and tn % GLA_RANK == 0
    kern = functools.partial(_repack_kernel, n_plain=LR_COL // tn)
    return pl.pallas_call(
        kern,
        grid=(DEPTH, Z_WIDTH // tn),
        in_specs=[pl.BlockSpec((None, tn, D), lambda l, j: (l, j, 0)),
                  pl.BlockSpec((None, GLA_RANK, D), lambda l, j: (l, (j + 1) * (tn // GLA_RANK), 0))],
        out_specs=pl.BlockSpec((None, D, tn), lambda l, j: (l, 0, j)),
        out_shape=jax.ShapeDtypeStruct((DEPTH, D, Z_WIDTH), BF16),
        compiler_params=_cparams(2),
        name="repack_w_in",
    )(w_in_t, w_in_t)


def _layer(layer, xp, xs, mod, p, state_gla, cache_k, cache_v, state_out):
    hp = _prep(layer, xp, p["norm1_w"], mod, MOD_SC1, MOD_SH1, sample=False)
    hs = _prep(layer, xs, p["norm1_w"], mod, MOD_SC1, MOD_SH1, sample=True)
    (zp,), (zs,) = _fused_matmul("w_in", layer, [hp], [hs], [(0, p["w_in_r"], 0, False)], [], _epi_plain, [F32],
                                 Z_WIDTH, 2048, 512)
    (lrp,), (lrs,) = _fused_matmul("w_lr", layer, [hp], [hs], [(0, p["w_lr_t"], 0, True)], [], _epi_plain, [F32],
                                   LR_PAD, 2048, LR_PAD)
    oa_p, s_p = _gla_prompt(layer, zp, lrp, p["wgk_pad"], p["b_gk"], p["gla_norm_w"])
    ob_p = _gmlp_prompt(layer, zp, p["gm_ws"], p["gm_bs_t"], p["gm_norm_w"], p["gm_norm_b"])
    oc_p = _swa_prompt(layer, zp, p["sinks_b"])
    z4 = zp.reshape(BATCH, SEQ, Z_WIDTH)
    kp_rows = z4[:, SEQ - WINDOW:, Z_KC:Z_KC + SWA_KV].reshape(BATCH, WINDOW, SWA_HKV, SWA_HD)
    vp_rows = z4[:, SEQ - WINDOW:, Z_VC:Z_VC + SWA_KV].reshape(BATCH, WINDOW, SWA_HKV, SWA_HD)
    oa_s, state_out = _gla_sample(layer, zs, lrs, p["wgk_pad"], p["b_gk"], p["gla_norm_w"], state_gla, state_out)
    ob_s, v_gm = _gmlp_sample(layer, zs, p["gm_w0"], p["gm_b0"], p["gm_norm_w"], p["gm_norm_b"])
    q4 = zs[:, Z_QC:Z_QC + SWA_Q].reshape(DEC_BATCH, SWA_HKV, SWA_G, 1, SWA_HD)
    eye = jnp.eye(SWA_HKV, dtype=F32).reshape(SWA_HKV, 1, SWA_HKV, 1)
    qm = (q4 * eye[None]).reshape(DEC_BATCH, SWA_HQ, SWA_KV)
    om = _swa_sample(layer, qm, zs, cache_k, cache_v, p["sinks_b"], p["slopes_b"])
    om5 = om.reshape(DEC_BATCH, SWA_HKV, SWA_G, SWA_HKV, SWA_HD)
    oc_s = jnp.sum(om5 * eye[None], axis=3).reshape(DEC_BATCH, SWA_Q)
    ks_rows = zs[:, Z_KC:Z_KC + SWA_KV].reshape(DEC_BATCH, 1, SWA_HKV, SWA_HD)
    vs_rows = zs[:, Z_VC:Z_VC + SWA_KV].reshape(DEC_BATCH, 1, SWA_HKV, SWA_HD)
    (mp,), (ms,) = _fused_matmul(
        "merge", layer, [oa_p, ob_p, oc_p], [oa_s, ob_s, oc_s],
        [(0, p["w_pa"], 0, False), (1, p["w_pb"], 0, False), (2, p["w_pc"], 0, False)],
        [("tile", zp, zs, Z_GATES), ("tile", zp, zs, Z_GATES + D), ("tile", zp, zs, Z_GATES + 2 * D)],
        _epi_merge, [BF16], D, 2048, 256)
    (x1p,), (x1s,) = _fused_matmul("w_o", layer, [mp], [ms], [(0, p["w_o"], 0, False)],
                                   [("tile", xp, xs, 0), ("mod", mod, MOD_G1)], _epi_residual, [F32], D, 2048, 512)
    h2p = _prep(layer, x1p, p["norm2_w"], mod, MOD_SC2, MOD_SH2, sample=False)
    h2s = _prep(layer, x1s, p["norm2_w"], mod, MOD_SC2, MOD_SH2, sample=True)
    (hidp,), (hids,) = _fused_matmul(
        "ffn_in", layer, [h2p], [h2s], [(0, p["w_ffn_in"], 0, False), (0, p["w_ffn_in"], FFN_HIDDEN, False)],
        [], _epi_swiglu, [BF16], FFN_HIDDEN, 2048, 512)
    (x2p,), (x2s,) = _fused_matmul("ffn_out", layer, [hidp], [hids], [(0, p["w_ffn_out"], 0, False)],
                                   [("tile", x1p, x1s, 0), ("mod", mod, MOD_G2)], _epi_residual, [F32],
                                   D, 1024, 512)
    return x2p, x2s, s_p, state_out, kp_rows, vp_rows, ks_rows, vs_rows, v_gm


def kernel(x_prompt, x_sample, c_prompt, c_sample, state_gla, cache_swa_k, cache_swa_v, w_ada, b_ada, norm1_w,
           norm2_w, w_in, w_gk2, b_gk, gla_norm_w, gm_norm_w, gm_norm_b, gm_ws, gm_bs, swa_sinks, w_pa, w_pb,
           w_pc, w_o, w_ffn_in, w_ffn_out, final_norm_w):
    w_in_t = jnp.swapaxes(w_in, 1, 2)
    w_lr_t = jnp.pad(w_in_t[:, LR_COL:LR_COL + GLA_RANK, :], ((0, 0), (0, LR_PAD - GLA_RANK), (0, 0))).astype(BF16)
    p = {
        "norm1_w": norm1_w, "norm2_w": norm2_w, "w_in_r": _repack_w_in_t(w_in_t), "w_lr_t": w_lr_t,
        "wgk_pad": jnp.pad(w_gk2, ((0, 0), (0, LR_PAD - GLA_RANK), (0, 0))),
        "b_gk": b_gk, "gla_norm_w": gla_norm_w, "gm_norm_w": gm_norm_w, "gm_norm_b": gm_norm_b,
        "gm_ws": gm_ws, "gm_bs_t": jnp.swapaxes(gm_bs, 1, 2),
        "gm_w0": jnp.repeat(gm_ws[:, :, 0, 0], GM_GW, axis=1).reshape(DEPTH, 1, GM_WIDTH),
        "gm_b0": jnp.repeat(gm_bs[:, :, 0], GM_GW, axis=1).reshape(DEPTH, 1, GM_WIDTH),
        "sinks_b": jnp.broadcast_to(swa_sinks[:, :, None], (DEPTH, SWA_HQ, LANE)),
        "slopes_b": jnp.broadcast_to(
            jnp.asarray([_alibi_slope(h) for h in range(SWA_HQ)], F32)[:, None], (SWA_HQ, LANE)),
        "w_pa": w_pa, "w_pb": w_pb, "w_pc": w_pc, "w_o": w_o, "w_ffn_in": w_ffn_in, "w_ffn_out": w_ffn_out,
    }
    c_all = jnp.concatenate([c_sample, c_prompt, jnp.zeros((MOD_ROWS - DEC_BATCH - BATCH, D), F32)], axis=0)
    mod = _ada(c_all, w_ada, b_ada)

    xp = x_prompt.reshape(BATCH * SEQ, D)
    xs = x_sample.reshape(DEC_BATCH, D)
    cache_k = cache_swa_k.reshape(DEPTH, DEC_BATCH, WINDOW, SWA_KV)
    cache_v = cache_swa_v.reshape(DEPTH, DEC_BATCH, WINDOW, SWA_KV)
    gla_p, kp, vp, ksm, vsm, gmv = [], [], [], [], [], []
    state_out = None
    for l in range(DEPTH):
        xp, xs, s_p, state_out, k_p, v_p, k_s, v_s, gv = _layer(l, xp, xs, mod, p, state_gla, cache_k, cache_v,
                                                                state_out)
        gla_p.append(s_p)
        kp.append(k_p)
        vp.append(v_p)
        ksm.append(k_s)
        vsm.append(v_s)
        gmv.append(gv.reshape(DEC_BATCH, 1, GM_WIDTH))
    y_prompt = _final_norm(xp, final_norm_w).reshape(BATCH, SEQ, D)
    y_sample = _final_norm(xs, final_norm_w).reshape(DEC_BATCH, 1, D)
    return (y_prompt, y_sample, jnp.stack(gla_p), state_out, jnp.stack(kp), jnp.stack(vp),
            jnp.stack(ksm), jnp.stack(vsm), jnp.stack(gmv))
```

```python
import functools

import jax
import jax.numpy as jnp
import numpy as np
from jax import lax
from jax.experimental import pallas as pl
from jax.experimental.pallas import tpu as pltpu

F32 = jnp.float32
BF16 = jnp.bfloat16

D = 2048
BATCH, SEQ = 2, 4096
DEPTH = 2
DEC_BATCH = 128
GLA_H, GLA_DK, GLA_DV = 4, 256, 512
GLA_KEY, GLA_VAL = GLA_H * GLA_DK, GLA_H * GLA_DV
GLA_RANK = 16
GLA_CHUNK = 16
GLA_SC = 128
GM_WIDTH, GM_GROUPS, GM_CHUNK = 1024, 4, 128
GM_GW = GM_WIDTH // GM_GROUPS
SWA_HQ, SWA_HKV, SWA_HD, WINDOW = 16, 4, 64, 128
SWA_G = SWA_HQ // SWA_HKV
SWA_Q, SWA_KV = SWA_HQ * SWA_HD, SWA_HKV * SWA_HD
FFN_HIDDEN = 5632
EPS = 1e-6
NEG_BIG = -1e30

Z_QA, Z_KA, Z_VA, Z_GA = 0, 1024, 2048, 4096
Z_UB, Z_VB = 6144, 7168
Z_QC, Z_KC, Z_VC = 8192, 9216, 9472
Z_GATES = 9728
Z_WIDTH = 15872
LR_COL = 6144
LANE = 128
SUBLANE = 8
LR_PAD = LANE

MOD_SH1, MOD_SC1, MOD_G1, MOD_SH2, MOD_SC2, MOD_G2 = range(6)
MOD_ROWS = DEC_BATCH + 8

VMEM_LIMIT = 56 * 1024 * 1024


def _cparams(n_axes):
    return pltpu.CompilerParams(dimension_semantics=("arbitrary",) * n_axes,
                                vmem_limit_bytes=VMEM_LIMIT)


def _bf(x):
    return x if x.dtype == BF16 else x.astype(BF16)


def _dot(a, b):
    return jnp.dot(_bf(a), _bf(b), preferred_element_type=F32)


def _dot_nt(a, b):
    return lax.dot_general(_bf(a), _bf(b), (((1,), (1,)), ((), ())), preferred_element_type=F32)


def _silu(x):
    return x * (1.0 / (1.0 + jnp.exp(-x)))


def _sigmoid(x):
    return 1.0 / (1.0 + jnp.exp(-x))


def _gelu(x):
    return 0.5 * x * (1.0 + jnp.tanh(np.sqrt(2.0 / np.pi).astype(np.float32) * (x + 0.044715 * (x * x * x))))


def _mod_spec_prompt(layer, chunk, tm, tn):
    cb, bpb = chunk * D // tn, SEQ // tm
    return pl.BlockSpec((None, None, 1, tn), lambda i, j: (layer, DEC_BATCH + i // bpb, 0, j + cb))


def _mod_spec_sample(layer, chunk, tn, jmap):
    cb = chunk * D // tn
    return pl.BlockSpec((None, DEC_BATCH, tn), lambda i, j: (layer, 0, jmap(i, j) + cb))


def _ada_kernel(c_ref, w_ref, b_ref, o_ref):
    o_ref[...] = _dot(_silu(c_ref[...]), w_ref[...]) + b_ref[...]


def _ada(c_all, w_ada, b_ada):
    tn = 1024
    return pl.pallas_call(
        _ada_kernel,
        grid=(DEPTH, 6 * D // tn),
        in_specs=[pl.BlockSpec((MOD_ROWS, D), lambda l, j: (0, 0)),
                  pl.BlockSpec((None, D, tn), lambda l, j: (l, 0, j)),
                  pl.BlockSpec((None, 1, tn), lambda l, j: (l, 0, j))],
        out_specs=pl.BlockSpec((None, MOD_ROWS, tn), lambda l, j: (l, 0, j)),
        out_shape=jax.ShapeDtypeStruct((DEPTH, MOD_ROWS, 6 * D), F32),
        compiler_params=_cparams(2),
        name="ada",
    )(c_all, w_ada, b_ada.reshape(DEPTH, 1, 6 * D))


def _prep_kernel(x_ref, nw_ref, sc_ref, sh_ref, *rest):
    x = x_ref[...]
    y = x * lax.rsqrt(jnp.mean(x * x, axis=-1, keepdims=True) + EPS) * nw_ref[...]
    h = (y * (1.0 + sc_ref[...]) + sh_ref[...]).astype(BF16)
    if len(rest) == 1:
        (o_ref,) = rest
    else:
        wlr_ref, o_ref, lr_ref = rest
        lr_ref[...] = _dot_nt(h, wlr_ref[...])
    o_ref[...] = h


def _prep(layer, x, norm_w, mod, sc_chunk, sh_chunk, sample, w_lr_t=None):
    rows = x.shape[0]
    if sample:
        tm, modop = rows, mod
        mod_specs = [_mod_spec_sample(layer, c, D, lambda i, j: j) for c in (sc_chunk, sh_chunk)]
    else:
        tm, modop = 512, mod.reshape(DEPTH, MOD_ROWS, 1, 6 * D)
        mod_specs = [_mod_spec_prompt(layer, c, tm, D) for c in (sc_chunk, sh_chunk)]
    args = [x, norm_w.reshape(DEPTH, 1, D), modop, modop]
    in_specs = [pl.BlockSpec((tm, D), lambda i, j: (i, 0)),
                pl.BlockSpec((None, 1, D), lambda i, j: (layer, 0, 0))] + mod_specs
    out_specs = [pl.BlockSpec((tm, D), lambda i, j: (i, 0))]
    out_shape = [jax.ShapeDtypeStruct((rows, D), BF16)]
    if w_lr_t is not None:
        args.append(w_lr_t)
        in_specs.append(pl.BlockSpec((None, LR_PAD, D), lambda i, j: (layer, 0, 0)))
        out_specs.append(pl.BlockSpec((tm, LR_PAD), lambda i, j: (i, 0)))
        out_shape.append(jax.ShapeDtypeStruct((rows, LR_PAD), F32))
    res = pl.pallas_call(
        _prep_kernel,
        grid=(rows // tm, 1),
        in_specs=in_specs,
        out_specs=out_specs,
        out_shape=out_shape,
        compiler_params=_cparams(2),
        name="prep",
    )(*args)
    return res if w_lr_t is not None else res[0]


def _final_norm_kernel(x_ref, nw_ref, o_ref):
    x = x_ref[...]
    o_ref[...] = x * lax.rsqrt(jnp.mean(x * x, axis=-1, keepdims=True) + EPS) * nw_ref[...]


def _final_norm(x, w):
    rows = x.shape[0]
    tm = min(rows, 512)
    return pl.pallas_call(
        _final_norm_kernel,
        grid=(rows // tm,),
        in_specs=[pl.BlockSpec((tm, D), lambda i: (i, 0)), pl.BlockSpec((1, D), lambda i: (0, 0))],
        out_specs=pl.BlockSpec((tm, D), lambda i: (i, 0)),
        out_shape=jax.ShapeDtypeStruct((rows, D), F32),
        compiler_params=_cparams(1),
        name="final_norm",
    )(x, w.reshape(1, D))


def _mm_kernel(*refs, n_a, term_a, term_t, n_extra, n_out, epilogue):
    sizes = (n_a, n_a, len(term_a), n_extra, n_extra, n_out, n_out)
    groups, pos = [], 0
    for n in sizes:
        groups.append(refs[pos:pos + n])
        pos += n
    a_p, a_s, w_refs, e_p, e_s, o_p, o_s = groups
    w_vals = [_bf(w[...]) for w in w_refs]

    def run(a_refs, e_refs, o_refs):
        a_vals = [_bf(a[...]) for a in a_refs]
        dots = [(_dot_nt if t else _dot)(a_vals[ai], w) for ai, t, w in zip(term_a, term_t, w_vals)]
        outs = epilogue(dots, [e[...] for e in e_refs])
        for o_ref, o in zip(o_refs, outs):
            o_ref[...] = o.astype(o_ref.dtype)

    run(a_p, e_p, o_p)

    @pl.when(pl.program_id(0) == 0)
    def _():
        run(a_s, e_s, o_s)


def _fused_matmul(name, layer, a_p, a_s, terms, extras, epilogue, out_dtypes, n_cols, tm, tn):
    rows_p, rows_s = a_p[0].shape[0], a_s[0].shape[0]
    nj = n_cols // tn
    grid = (rows_p // tm, nj)
    sj = lambda i, j: jnp.where(i == 0, j, nj - 1)
    args, in_specs = [], []
    for a in a_p:
        args.append(a)
        in_specs.append(pl.BlockSpec((tm, a.shape[1]), lambda i, j: (i, 0), pipeline_mode=pl.Buffered(1)))
    for a in a_s:
        args.append(a)
        in_specs.append(pl.BlockSpec((rows_s, a.shape[1]), lambda i, j: (0, 0)))
    for ai, w, col0, transposed in terms:
        assert col0 % tn == 0 and w.shape[-1 if transposed else -2] == a_p[ai].shape[1]
        args.append(w)
        if transposed:
            in_specs.append(pl.BlockSpec((None, tn, w.shape[-1]), lambda i, j, cb=col0 // tn: (layer, j + cb, 0)))
        else:
            in_specs.append(pl.BlockSpec((None, w.shape[-2], tn), lambda i, j, cb=col0 // tn: (layer, 0, j + cb)))
    s_args, s_specs = [], []
    for ex in extras:
        if ex[0] == "tile":
            _, arr_p, arr_s, col0 = ex
            assert col0 % tn == 0
            args.append(arr_p)
            in_specs.append(pl.BlockSpec((tm, tn), lambda i, j, cb=col0 // tn: (i, j + cb)))
            s_args.append(arr_s)
            s_specs.append(pl.BlockSpec((rows_s, tn), lambda i, j, cb=col0 // tn: (0, sj(i, j) + cb)))
        else:
            _, mod, chunk = ex
            args.append(mod.reshape(DEPTH, MOD_ROWS, 1, 6 * D))
            in_specs.append(_mod_spec_prompt(layer, chunk, tm, tn))
            s_args.append(mod)
            s_specs.append(_mod_spec_sample(layer, chunk, tn, sj))
    kern = functools.partial(_mm_kernel, n_a=len(a_p), term_a=tuple(t[0] for t in terms),
                             term_t=tuple(t[3] for t in terms), n_extra=len(extras), n_out=len(out_dtypes),
                             epilogue=epilogue)
    res = pl.pallas_call(
        kern,
        grid=grid,
        in_specs=in_specs + s_specs,
        out_specs=([pl.BlockSpec((tm, tn), lambda i, j: (i, j)) for _ in out_dtypes]
                   + [pl.BlockSpec((rows_s, tn), lambda i, j: (0, sj(i, j))) for _ in out_dtypes]),
        out_shape=([jax.ShapeDtypeStruct((rows_p, n_cols), dt) for dt in out_dtypes]
                   + [jax.ShapeDtypeStruct((rows_s, n_cols), dt) for dt in out_dtypes]),
        compiler_params=_cparams(2),
        name=name,
    )(*args, *s_args)
    return res[:len(out_dtypes)], res[len(out_dtypes):]


def _epi_plain(dots, ex):
    return [dots[0]]


def _epi_merge(dots, ex):
    return [_sigmoid(ex[0]) * dots[0] + _sigmoid(ex[1]) * dots[1] + _sigmoid(ex[2]) * dots[2]]


def _epi_residual(dots, ex):
    return [ex[0] + ex[1] * dots[0]]


def _epi_swiglu(dots, ex):
    return [_silu(dots[0]) * dots[1]]


def _log_sigmoid(u):
    return -(jnp.maximum(-u, 0.0) + jnp.log1p(jnp.exp(-jnp.abs(u))))


def _gla_prompt_kernel(q_ref, k_ref, v_ref, ga_ref, lr_ref, wgk_ref, bgk_ref, nw_ref,
                       oa_ref, sfin_ref, st_s, vt_s, o_s, *, tb):
    t = pl.program_id(1)
    nsc = tb // GLA_SC
    nd = GLA_SC // GLA_CHUNK - 1

    @pl.when(t == 0)
    def _():
        st_s[...] = jnp.zeros_like(st_s)

    u = _dot(lr_ref[...], wgk_ref[...]) + bgk_ref[...]
    gk = _log_sigmoid(u) * (1.0 / 16.0)
    sub = lax.broadcasted_iota(jnp.int32, gk.shape, 0) % SUBLANE
    p8 = gk
    for s in (1, 2, 4):
        p8 = p8 + jnp.where(sub >= s, pltpu.roll(p8, s, 0), 0.0)
    nchunk = tb // GLA_CHUNK
    b_parts, bs_parts, tot = [], [], []
    acc = None
    for c in range(nchunk):
        r0 = c * GLA_CHUNK
        lo = p8[r0:r0 + SUBLANE, :]
        hi = p8[r0 + SUBLANE:r0 + GLA_CHUNK, :] + lo[SUBLANE - 1:SUBLANE, :]
        if c % (GLA_SC // GLA_CHUNK) == 0:
            b_parts += [lo, hi]
            bs_parts += [lo, hi]
            acc = hi[SUBLANE - 1:SUBLANE, :]
        else:
            b_parts += [lo, hi]
            bs_parts += [lo + acc, hi + acc]
            acc = acc + hi[SUBLANE - 1:SUBLANE, :]
        tot.append(hi[SUBLANE - 1:SUBLANE, :])
    b = jnp.concatenate(b_parts, axis=0)
    bs = jnp.concatenate(bs_parts, axis=0)

    def per_chunk(vals, shift):
        return jnp.concatenate([jnp.broadcast_to(vals[(c + shift) % nchunk], (GLA_CHUNK, GLA_KEY))
                                for c in range(nchunk)], axis=0)

    blb = per_chunk(tot, 0)
    etot = [jnp.exp(t) for t in tot]

    q = q_ref[...] * (GLA_DK ** -0.5)
    k = k_ref[...]
    qin = _bf(q * jnp.exp(b))
    kout = _bf(k * jnp.exp(-b))
    kd = k * jnp.exp(blb - b)
    qsc = _bf(q * jnp.exp(bs))
    vt_s[...] = v_ref[...].T

    ri = lax.broadcasted_iota(jnp.int32, (tb, tb), 0)
    ci = lax.broadcasted_iota(jnp.int32, (tb, tb), 1)
    delta = jnp.where(ri // GLA_SC == ci // GLA_SC, ri // GLA_CHUNK - ci // GLA_CHUNK, -1)
    m_intra = (delta == 0) & (ci <= ri)
    kds = [_bf(kd)]
    for d in range(1, nd):
        kd = kd * per_chunk(etot, d)
        kds.append(_bf(kd))
    for h in range(GLA_H):
        ks = slice(h * GLA_DK, (h + 1) * GLA_DK)
        vs = slice(h * GLA_DV, (h + 1) * GLA_DV)
        a = jnp.where(m_intra, _dot_nt(qin[:, ks], kout[:, ks]), 0.0)
        for d in range(nd):
            a = jnp.where(delta == d + 1, _dot_nt(qin[:, ks], kds[d][:, ks]), a)
        o_s[:, vs] = _dot(a, v_ref[:, vs])

    for sc in range(nsc):
        rows = slice(sc * GLA_SC, (sc + 1) * GLA_SC)
        last = bs[(sc + 1) * GLA_SC - 1:(sc + 1) * GLA_SC, :]
        k2 = _bf(k[rows, :] * jnp.exp(last - bs[rows, :]))
        elast = jnp.exp(last)
        for h in range(GLA_H):
            ks = slice(h * GLA_DK, (h + 1) * GLA_DK)
            vs = slice(h * GLA_DV, (h + 1) * GLA_DV)
            st = st_s[h]
            o_s[rows, vs] += _dot_nt(qsc[rows, ks], st)
            st_s[h] = st * elast[:, ks] + _dot(vt_s[vs, rows], k2[:, ks])

    nw = nw_ref[...]
    for h in range(GLA_H):
        vs = slice(h * GLA_DV, (h + 1) * GLA_DV)
        o = o_s[:, vs]
        y = o * lax.rsqrt(jnp.mean(o * o, axis=-1, keepdims=True) + EPS) * nw
        oa_ref[:, vs] = (y * _silu(ga_ref[:, vs])).astype(oa_ref.dtype)

    @pl.when(t == pl.num_programs(1) - 1)
    def _():
        for h in range(GLA_H):
            sfin_ref[h] = st_s[h].T


def _gla_prompt(layer, z, lr, wgk_pad, b_gk, gla_norm_w):
    tb = 256
    nt = SEQ // tb
    row = lambda b, t: b * nt + t
    kern = functools.partial(_gla_prompt_kernel, tb=tb)
    return pl.pallas_call(
        kern,
        grid=(BATCH, nt),
        in_specs=[pl.BlockSpec((tb, GLA_KEY), lambda b, t: (row(b, t), Z_QA // GLA_KEY)),
                  pl.BlockSpec((tb, GLA_KEY), lambda b, t: (row(b, t), Z_KA // GLA_KEY)),
                  pl.BlockSpec((tb, GLA_VAL), lambda b, t: (row(b, t), Z_VA // GLA_VAL)),
                  pl.BlockSpec((tb, GLA_VAL), lambda b, t: (row(b, t), Z_GA // GLA_VAL)),
                  pl.BlockSpec((tb, LR_PAD), lambda b, t: (row(b, t), 0)),
                  pl.BlockSpec((None, LR_PAD, GLA_KEY), lambda b, t: (layer, 0, 0)),
                  pl.BlockSpec((None, 1, GLA_KEY), lambda b, t: (layer, 0, 0)),
                  pl.BlockSpec((None, 1, GLA_DV), lambda b, t: (layer, 0, 0))],
        out_specs=[pl.BlockSpec((tb, GLA_VAL), lambda b, t: (row(b, t), 0)),
                   pl.BlockSpec((None, GLA_H, GLA_DK, GLA_DV), lambda b, t: (b, 0, 0, 0))],
        out_shape=[jax.ShapeDtypeStruct((BATCH * SEQ, GLA_VAL), BF16),
                   jax.ShapeDtypeStruct((BATCH, GLA_H, GLA_DK, GLA_DV), F32)],
        scratch_shapes=[pltpu.VMEM((GLA_H, GLA_DV, GLA_DK), F32),
                        pltpu.VMEM((GLA_VAL, tb), F32),
                        pltpu.VMEM((tb, GLA_VAL), F32)],
        compiler_params=_cparams(2),
        name="gla_prompt",
    )(z, z, z, z, lr, wgk_pad, b_gk.reshape(DEPTH, 1, GLA_KEY), gla_norm_w.reshape(DEPTH, 1, GLA_DV))


def _gla_sample_kernel(q_ref, k_ref, v_ref, ga_ref, lr_ref, wgk_ref, bgk_ref, nw_ref, s_ref,
                       oa_ref, snew_ref, x_s, *, rb):
    u = _dot(lr_ref[...], wgk_ref[...]) + bgk_ref[...]
    eg = jnp.exp(_log_sigmoid(u) * (1.0 / 16.0))
    x_s[...] = jnp.zeros_like(x_s)
    x_s[0:rb, :] = q_ref[...] * (GLA_DK ** -0.5)
    x_s[rb:2 * rb, :] = k_ref[...]
    x_s[2 * rb:3 * rb, :] = eg
    xt = x_s[...].T
    nw = nw_ref[...]
    for r in range(rb):
        qc = xt[:, r:r + 1]
        kc = xt[:, rb + r:rb + r + 1]
        gc = xt[:, 2 * rb + r:2 * rb + r + 1]
        s_new = gc * s_ref[r, 0] + kc * v_ref[r:r + 1, :]
        snew_ref[r, 0] = s_new
        o = jnp.sum(qc * s_new, axis=0, keepdims=True)
        y = o * lax.rsqrt(jnp.mean(o * o, axis=-1, keepdims=True) + EPS) * nw
        oa_ref[r:r + 1, :] = y * _silu(ga_ref[r:r + 1, :])


def _gla_sample(layer, z, lr, wgk_pad, b_gk, gla_norm_w, state_gla, state_out):
    rb = 8
    kern = functools.partial(_gla_sample_kernel, rb=rb)
    in_specs = [pl.BlockSpec((rb, GLA_DK), lambda i, h: (i, Z_QA // GLA_DK + h)),
                pl.BlockSpec((rb, GLA_DK), lambda i, h: (i, Z_KA // GLA_DK + h)),
                pl.BlockSpec((rb, GLA_DV), lambda i, h: (i, Z_VA // GLA_DV + h)),
                pl.BlockSpec((rb, GLA_DV), lambda i, h: (i, Z_GA // GLA_DV + h)),
                pl.BlockSpec((rb, LR_PAD), lambda i, h: (i, 0)),
                pl.BlockSpec((None, LR_PAD, GLA_DK), lambda i, h: (layer, 0, h)),
                pl.BlockSpec((None, 1, GLA_DK), lambda i, h: (layer, 0, h)),
                pl.BlockSpec((None, 1, GLA_DV), lambda i, h: (layer, 0, 0)),
                pl.BlockSpec((None, rb, 1, GLA_DK, GLA_DV), lambda i, h: (layer, i, h, 0, 0))]
    args = [z, z, z, z, lr, wgk_pad, b_gk.reshape(DEPTH, 1, GLA_KEY), gla_norm_w.reshape(DEPTH, 1, GLA_DV),
            state_gla]
    aliases = {}
    if state_out is not None:
        in_specs.append(pl.BlockSpec(memory_space=pl.ANY))
        args.append(state_out)
        aliases = {len(args) - 1: 1}
        kern = functools.partial(_drop_last_input, kern, n_in=len(args))
    return pl.pallas_call(
        kern,
        grid=(DEC_BATCH // rb, GLA_H),
        in_specs=in_specs,
        out_specs=[pl.BlockSpec((rb, GLA_DV), lambda i, h: (i, h)),
                   pl.BlockSpec((None, rb, 1, GLA_DK, GLA_DV), lambda i, h: (layer, i, h, 0, 0))],
        out_shape=[jax.ShapeDtypeStruct((DEC_BATCH, GLA_VAL), F32),
                   jax.ShapeDtypeStruct((DEPTH, DEC_BATCH, GLA_H, GLA_DK, GLA_DV), F32)],
        scratch_shapes=[pltpu.VMEM((LANE, GLA_DK), F32)],
        input_output_aliases=aliases,
        compiler_params=_cparams(2),
        name="gla_sample",
    )(*args)


def _drop_last_input(kern, *refs, n_in):
    return kern(*refs[:n_in - 1], *refs[n_in:])


def _layernorm(x, w, b):
    mu = jnp.mean(x, axis=-1, keepdims=True)
    xc = x - mu
    var = jnp.mean(xc * xc, axis=-1, keepdims=True)
    return xc * lax.rsqrt(var + EPS) * w + b


def _gmlp_prompt_kernel(u_ref, v_ref, ws_ref, bst_ref, nw_ref, nb_ref, ob_ref, *, nsub):
    ri = lax.broadcasted_iota(jnp.int32, (GM_CHUNK, GM_CHUNK), 0)
    ci = lax.broadcasted_iota(jnp.int32, (GM_CHUNK, GM_CHUNK), 1)
    tril = ci <= ri
    for s in range(nsub):
        rs = slice(s * GM_CHUNK, (s + 1) * GM_CHUNK)
        u = _gelu(u_ref[rs, :])
        v = _layernorm(_gelu(v_ref[rs, :]), nw_ref[...], nb_ref[...])
        for g in range(GM_GROUPS):
            cs = slice(g * GM_GW, (g + 1) * GM_GW)
            wm = jnp.where(tril, ws_ref[g], 0.0)
            mixed = _dot(wm, v[:, cs]) + bst_ref[:, g:g + 1]
            ob_ref[rs, cs] = (u[:, cs] * mixed).astype(ob_ref.dtype)


def _gmlp_prompt(layer, z, gm_ws, gm_bs_t, gm_norm_w, gm_norm_b):
    nsub = 4
    tb = nsub * GM_CHUNK
    kern = functools.partial(_gmlp_prompt_kernel, nsub=nsub)
    return pl.pallas_call(
        kern,
        grid=(BATCH * SEQ // tb,),
        in_specs=[pl.BlockSpec((tb, GM_WIDTH), lambda i: (i, Z_UB // GM_WIDTH)),
                  pl.BlockSpec((tb, GM_WIDTH), lambda i: (i, Z_VB // GM_WIDTH)),
                  pl.BlockSpec((None, GM_GROUPS, GM_CHUNK, GM_CHUNK), lambda i: (layer, 0, 0, 0)),
                  pl.BlockSpec((None, GM_CHUNK, GM_GROUPS), lambda i: (layer, 0, 0)),
                  pl.BlockSpec((None, 1, GM_WIDTH), lambda i: (layer, 0, 0)),
                  pl.BlockSpec((None, 1, GM_WIDTH), lambda i: (layer, 0, 0))],
        out_specs=pl.BlockSpec((tb, GM_WIDTH), lambda i: (i, 0)),
        out_shape=jax.ShapeDtypeStruct((BATCH * SEQ, GM_WIDTH), BF16),
        compiler_params=_cparams(1),
        name="gmlp_prompt",
    )(z, z, gm_ws, gm_bs_t, gm_norm_w.reshape(DEPTH, 1, GM_WIDTH), gm_norm_b.reshape(DEPTH, 1, GM_WIDTH))


def _gmlp_sample_kernel(u_ref, v_ref, w0_ref, b0_ref, nw_ref, nb_ref, ob_ref, vn_ref):
    u = _gelu(u_ref[...])
    v = _layernorm(_gelu(v_ref[...]), nw_ref[...], nb_ref[...])
    vn_ref[...] = v
    ob_ref[...] = u * (w0_ref[...] * v + b0_ref[...])


def _gmlp_sample(layer, z, w0_row, b0_row, gm_norm_w, gm_norm_b):
    full = lambda i: (0, 0)
    lrow = lambda i: (layer, 0, 0)
    return pl.pallas_call(
        _gmlp_sample_kernel,
        grid=(1,),
        in_specs=[pl.BlockSpec((DEC_BATCH, GM_WIDTH), lambda i: (0, Z_UB // GM_WIDTH)),
                  pl.BlockSpec((DEC_BATCH, GM_WIDTH), lambda i: (0, Z_VB // GM_WIDTH)),
                  pl.BlockSpec((None, 1, GM_WIDTH), lrow),
                  pl.BlockSpec((None, 1, GM_WIDTH), lrow),
                  pl.BlockSpec((None, 1, GM_WIDTH), lrow),
                  pl.BlockSpec((None, 1, GM_WIDTH), lrow)],
        out_specs=[pl.BlockSpec((DEC_BATCH, GM_WIDTH), full), pl.BlockSpec((DEC_BATCH, GM_WIDTH), full)],
        out_shape=[jax.ShapeDtypeStruct((DEC_BATCH, GM_WIDTH), F32),
                   jax.ShapeDtypeStruct((DEC_BATCH, GM_WIDTH), F32)],
        compiler_params=_cparams(1),
        name="gmlp_sample",
    )(z, z, w0_row, b0_row, gm_norm_w.reshape(DEPTH, 1, GM_WIDTH), gm_norm_b.reshape(DEPTH, 1, GM_WIDTH))


def _alibi_slope(h):
    return float(2.0 ** (-8.0 * (h + 1) / SWA_HQ))


def _swa_lane_halves(x, half):
    lane = lax.broadcasted_iota(jnp.int32, x.shape, 1)
    own = jnp.where((lane >= half * SWA_HD) & (lane < (half + 1) * SWA_HD), x, 0.0)
    other = pltpu.roll(own, SWA_HD, 1)
    return (own, other) if half == 0 else (other, own)


def _swa_prompt_kernel(q_ref, kc_ref, kp_ref, vc_ref, vp_ref, sink_ref, oc_ref, s_s, p_s):
    n = pl.program_id(1)
    w = WINDOW
    ri = lax.broadcasted_iota(jnp.int32, (w, 2 * w), 0)
    ci = lax.broadcasted_iota(jnp.int32, (w, 2 * w), 1)
    dist_i = w + ri - ci
    valid = (dist_i >= 0) & (dist_i < w) & ((ci >= w) | (n > 0))
    dist = dist_i.astype(F32)
    kcat = jnp.concatenate([kp_ref[...], kc_ref[...]], axis=0)
    vcat = jnp.concatenate([vp_ref[...], vc_ref[...]], axis=0)
    heads = []
    for kv in range(SWA_HKV):
        t, half = kv // 2, kv % 2
        k_lo, k_hi = _swa_lane_halves(kcat[:, t * LANE:(t + 1) * LANE], half)
        q2 = jnp.concatenate([q_ref[:, 2 * kv * LANE:(2 * kv + 1) * LANE],
                              q_ref[:, (2 * kv + 1) * LANE:(2 * kv + 2) * LANE]], axis=0)
        for par, kk in ((0, k_lo), (1, k_hi)):
            s = _dot_nt(q2, kk) * (SWA_HD ** -0.5)
            for e in range(2):
                h = SWA_G * kv + 2 * e + par
                seg = len(heads)
                heads.append(h)
                s_s[seg * w:(seg + 1) * w, :] = jnp.where(
                    valid, s[e * w:(e + 1) * w, :] - _alibi_slope(h) * dist, NEG_BIG)
    s = s_s[...]
    sink = jnp.concatenate([jnp.broadcast_to(sink_ref[h:h + 1, 0:1], (w, 1)) for h in heads], axis=0)
    m = jnp.maximum(jnp.max(s, axis=-1, keepdims=True), sink)
    p = jnp.exp(s - m)
    inv = 1.0 / (jnp.sum(p, axis=-1, keepdims=True) + jnp.exp(sink - m))
    p_s[...] = (p * inv).astype(p_s.dtype)
    for kv in range(SWA_HKV):
        t, half = kv // 2, kv % 2
        v_lo, v_hi = _swa_lane_halves(vcat[:, t * LANE:(t + 1) * LANE], half)
        r0 = SWA_G * kv * w
        o = _dot(p_s[r0:r0 + 2 * w, :], v_lo) + _dot(p_s[r0 + 2 * w:r0 + 4 * w, :], v_hi)
        oc_ref[:, 2 * kv * LANE:(2 * kv + 1) * LANE] = o[:w].astype(oc_ref.dtype)
        oc_ref[:, (2 * kv + 1) * LANE:(2 * kv + 2) * LANE] = o[w:].astype(oc_ref.dtype)


def _swa_prompt(layer, z, sinks_b):
    nb = SEQ // WINDOW
    row = lambda b, n: b * nb + n
    prev = lambda b, n: b * nb + jnp.maximum(n - 1, 0)
    return pl.pallas_call(
        _swa_prompt_kernel,
        grid=(BATCH, nb),
        in_specs=[pl.BlockSpec((WINDOW, SWA_Q), lambda b, n: (row(b, n), Z_QC // SWA_Q)),
                  pl.BlockSpec((WINDOW, SWA_KV), lambda b, n: (row(b, n), Z_KC // SWA_KV)),
                  pl.BlockSpec((WINDOW, SWA_KV), lambda b, n: (prev(b, n), Z_KC // SWA_KV)),
                  pl.BlockSpec((WINDOW, SWA_KV), lambda b, n: (row(b, n), Z_VC // SWA_KV)),
                  pl.BlockSpec((WINDOW, SWA_KV), lambda b, n: (prev(b, n), Z_VC // SWA_KV)),
                  pl.BlockSpec((None, SWA_HQ, LANE), lambda b, n: (layer, 0, 0))],
        out_specs=pl.BlockSpec((WINDOW, SWA_Q), lambda b, n: (row(b, n), 0)),
        out_shape=jax.ShapeDtypeStruct((BATCH * SEQ, SWA_Q), BF16),
        scratch_shapes=[pltpu.VMEM((SWA_HQ * WINDOW, 2 * WINDOW), F32),
                        pltpu.VMEM((SWA_HQ * WINDOW, 2 * WINDOW), BF16)],
        compiler_params=_cparams(2),
        name="swa_prompt",
    )(z, z, z, z, z, sinks_b)


def _swa_sample_kernel(q_ref, kn_ref, vn_ref, kt_ref, vt_ref, sink_ref, slope_ref, o_ref, *, rb):
    wb = WINDOW
    j = lax.broadcasted_iota(jnp.int32, (SWA_HQ, wb), 1)
    dist = (wb - j).astype(F32)
    ok = j >= 1
    grp = lax.broadcasted_iota(jnp.int32, (SWA_HQ, 1), 0) // SWA_G
    slope = slope_ref[:, 0:1]
    sink = sink_ref[:, 0:1]

    def per_head(rows):
        out = jnp.broadcast_to(rows[0:1, :], (SWA_HQ, SWA_HD))
        for kv in range(1, SWA_HKV):
            out = jnp.where(grp == kv, rows[kv:kv + 1, :], out)
        return out

    s_rows, self_rows = [], []
    for r in range(rb):
        q = q_ref[r]
        s = _dot(q, kt_ref[r, 0])
        for kv in range(1, SWA_HKV):
            s = jnp.where(grp == kv, _dot(q, kt_ref[r, kv]), s)
        s_rows.append(jnp.where(ok, s * (SWA_HD ** -0.5) - slope * dist, NEG_BIG))
        self_rows.append(jnp.sum(_bf(q).astype(F32) * _bf(per_head(kn_ref[r])).astype(F32), axis=-1,
                                 keepdims=True) * (SWA_HD ** -0.5))
    s = jnp.concatenate(s_rows, axis=0)
    s_self = jnp.concatenate(self_rows, axis=0)
    sink = jnp.concatenate([sink] * rb, axis=0)
    m = jnp.maximum(jnp.maximum(jnp.max(s, axis=-1, keepdims=True), s_self), sink)
    p = jnp.exp(s - m)
    p_self = jnp.exp(s_self - m)
    inv = 1.0 / (jnp.sum(p, axis=-1, keepdims=True) + p_self + jnp.exp(sink - m))
    pn = _bf(p * inv)
    pn_self = _bf(p_self * inv).astype(F32)
    for r in range(rb):
        rows = slice(r * SWA_HQ, (r + 1) * SWA_HQ)
        o = _dot_nt(pn[rows, :], vt_ref[r, 0])
        for kv in range(1, SWA_HKV):
            o = jnp.where(grp == kv, _dot_nt(pn[rows, :], vt_ref[r, kv]), o)
        o_ref[r] = o + pn_self[rows, :] * _bf(per_head(vn_ref[r])).astype(F32)


def _swa_sample(layer, q3, kn3, vn3, cache_kt, cache_vt, sinks_b, slopes_b):
    rb = 8
    kern = functools.partial(_swa_sample_kernel, rb=rb)
    return pl.pallas_call(
        kern,
        grid=(DEC_BATCH // rb,),
        in_specs=[pl.BlockSpec((rb, SWA_HQ, SWA_HD), lambda i: (i, 0, 0)),
                  pl.BlockSpec((rb, SWA_HKV, SWA_HD), lambda i: (i, 0, 0)),
                  pl.BlockSpec((rb, SWA_HKV, SWA_HD), lambda i: (i, 0, 0)),
                  pl.BlockSpec((None, rb, SWA_HKV, SWA_HD, WINDOW), lambda i: (layer, i, 0, 0, 0)),
                  pl.BlockSpec((None, rb, SWA_HKV, SWA_HD, WINDOW), lambda i: (layer, i, 0, 0, 0)),
                  pl.BlockSpec((None, SWA_HQ, LANE), lambda i: (layer, 0, 0)),
                  pl.BlockSpec((SWA_HQ, LANE), lambda i: (0, 0))],
        out_specs=pl.BlockSpec((rb, SWA_HQ, SWA_HD), lambda i: (i, 0, 0)),
        out_shape=jax.ShapeDtypeStruct((DEC_BATCH, SWA_HQ, SWA_HD), F32),
        compiler_params=_cparams(1),
        name="swa_sample",
    )(q3, kn3, vn3, cache_kt, cache_vt, sinks_b, slopes_b)


def _repack_kernel(a_ref, b_ref, o_ref, *, n_plain):
    j = pl.program_id(1)

    @pl.when(j < n_plain)
    def _():
        o_ref[...] = a_ref[...].T.astype(o_ref.dtype)

    @pl.when(j >= n_plain)
    def _():
        o_ref[...] = jnp.concatenate([a_ref[GLA_RANK:, :], b_ref[...]], axis=0).T.astype(o_ref.dtype)


def _repack_w_in_t(w_in_t):
    tn = 512
    assert LR_COL % tn == 0 and Z_WIDTH % tn == 0 and tn % GLA_RANK == 0
    kern = functools.partial(_repack_kernel, n_plain=LR_COL // tn)
    return pl.pallas_call(
        kern,
        grid=(DEPTH, Z_WIDTH // tn),
        in_specs=[pl.BlockSpec((None, tn, D), lambda l, j: (l, j, 0)),
                  pl.BlockSpec((None, GLA_RANK, D), lambda l, j: (l, (j + 1) * (tn // GLA_RANK), 0))],
        out_specs=pl.BlockSpec((None, D, tn), lambda l, j: (l, 0, j)),
        out_shape=jax.ShapeDtypeStruct((DEPTH, D, Z_WIDTH), BF16),
        compiler_params=_cparams(2),
        name="repack_w_in",
    )(w_in_t, w_in_t)


def _layer(layer, xp, xs, mod, p, state_gla, cache_k, cache_v, state_out):
    hp, lrp = _prep(layer, xp, p["norm1_w"], mod, MOD_SC1, MOD_SH1, False, p["w_lr_t"])
    hs, lrs = _prep(layer, xs, p["norm1_w"], mod, MOD_SC1, MOD_SH1, True, p["w_lr_t"])
    (zp,), (zs,) = _fused_matmul("w_in", layer, [hp], [hs], [(0, p["w_in_r"], 0, False)], [], _epi_plain, [F32],
                                 Z_WIDTH, 2048, 512)
    oa_p, s_p = _gla_prompt(layer, zp, lrp, p["wgk_pad"], p["b_gk"], p["gla_norm_w"])
    ob_p = _gmlp_prompt(layer, zp, p["gm_ws"], p["gm_bs_t"], p["gm_norm_w"], p["gm_norm_b"])
    oc_p = _swa_prompt(layer, zp, p["sinks_b"])
    z4 = zp.reshape(BATCH, SEQ, Z_WIDTH)
    kp_rows = z4[:, SEQ - WINDOW:, Z_KC:Z_KC + SWA_KV].reshape(BATCH, WINDOW, SWA_HKV, SWA_HD)
    vp_rows = z4[:, SEQ - WINDOW:, Z_VC:Z_VC + SWA_KV].reshape(BATCH, WINDOW, SWA_HKV, SWA_HD)
    oa_s, state_out = _gla_sample(layer, zs, lrs, p["wgk_pad"], p["b_gk"], p["gla_norm_w"], state_gla, state_out)
    ob_s, v_gm = _gmlp_sample(layer, zs, p["gm_w0"], p["gm_b0"], p["gm_norm_w"], p["gm_norm_b"])
    q3 = zs[:, Z_QC:Z_QC + SWA_Q].reshape(DEC_BATCH, SWA_HQ, SWA_HD)
    kn3 = zs[:, Z_KC:Z_KC + SWA_KV].reshape(DEC_BATCH, SWA_HKV, SWA_HD)
    vn3 = zs[:, Z_VC:Z_VC + SWA_KV].reshape(DEC_BATCH, SWA_HKV, SWA_HD)
    oc_s = _swa_sample(layer, q3, kn3, vn3, cache_k, cache_v, p["sinks_b"], p["slopes_b"]).reshape(DEC_BATCH, SWA_Q)
    ks_rows = kn3.reshape(DEC_BATCH, 1, SWA_HKV, SWA_HD)
    vs_rows = vn3.reshape(DEC_BATCH, 1, SWA_HKV, SWA_HD)
    (mp,), (ms,) = _fused_matmul(
        "merge", layer, [oa_p, ob_p, oc_p], [oa_s, ob_s, oc_s],
        [(0, p["w_pa"], 0, False), (1, p["w_pb"], 0, False), (2, p["w_pc"], 0, False)],
        [("tile", zp, zs, Z_GATES), ("tile", zp, zs, Z_GATES + D), ("tile", zp, zs, Z_GATES + 2 * D)],
        _epi_merge, [BF16], D, 2048, 256)
    (x1p,), (x1s,) = _fused_matmul("w_o", layer, [mp], [ms], [(0, p["w_o"], 0, False)],
                                   [("tile", xp, xs, 0), ("mod", mod, MOD_G1)], _epi_residual, [F32], D, 2048, 512)
    h2p = _prep(layer, x1p, p["norm2_w"], mod, MOD_SC2, MOD_SH2, sample=False)
    h2s = _prep(layer, x1s, p["norm2_w"], mod, MOD_SC2, MOD_SH2, sample=True)
    (hidp,), (hids,) = _fused_matmul(
        "ffn_in", layer, [h2p], [h2s], [(0, p["w_ffn_in"], 0, False), (0, p["w_ffn_in"], FFN_HIDDEN, False)],
        [], _epi_swiglu, [BF16], FFN_HIDDEN, 2048, 512)
    (x2p,), (x2s,) = _fused_matmul("ffn_out", layer, [hidp], [hids], [(0, p["w_ffn_out"], 0, False)],
                                   [("tile", x1p, x1s, 0), ("mod", mod, MOD_G2)], _epi_residual, [F32],
                                   D, 1024, 512)
    return x2p, x2s, s_p, state_out, kp_rows, vp_rows, ks_rows, vs_rows, v_gm


def kernel(x_prompt, x_sample, c_prompt, c_sample, state_gla, cache_swa_k, cache_swa_v, w_ada, b_ada, norm1_w,
           norm2_w, w_in, w_gk2, b_gk, gla_norm_w, gm_norm_w, gm_norm_b, gm_ws, gm_bs, swa_sinks, w_pa, w_pb,
           w_pc, w_o, w_ffn_in, w_ffn_out, final_norm_w):
    w_in_t = jnp.swapaxes(w_in, 1, 2)
    w_lr_t = jnp.pad(w_in_t[:, LR_COL:LR_COL + GLA_RANK, :], ((0, 0), (0, LR_PAD - GLA_RANK), (0, 0))).astype(BF16)
    p = {
        "norm1_w": norm1_w, "norm2_w": norm2_w, "w_in_r": _repack_w_in_t(w_in_t), "w_lr_t": w_lr_t,
        "wgk_pad": jnp.pad(w_gk2, ((0, 0), (0, LR_PAD - GLA_RANK), (0, 0))),
        "b_gk": b_gk, "gla_norm_w": gla_norm_w, "gm_norm_w": gm_norm_w, "gm_norm_b": gm_norm_b,
        "gm_ws": gm_ws, "gm_bs_t": jnp.swapaxes(gm_bs, 1, 2),
        "gm_w0": jnp.repeat(gm_ws[:, :, 0, 0], GM_GW, axis=1).reshape(DEPTH, 1, GM_WIDTH),
        "gm_b0": jnp.repeat(gm_bs[:, :, 0], GM_GW, axis=1).reshape(DEPTH, 1, GM_WIDTH),
        "sinks_b": jnp.broadcast_to(swa_sinks[:, :, None], (DEPTH, SWA_HQ, LANE)),
        "slopes_b": jnp.broadcast_to(
            jnp.asarray([_alibi_slope(h) for h in range(SWA_HQ)], F32)[:, None], (SWA_HQ, LANE)),
        "w_pa": w_pa, "w_pb": w_pb, "w_pc": w_pc, "w_o": w_o, "w_ffn_in": w_ffn_in, "w_ffn_out": w_ffn_out,
    }
    c_all = jnp.concatenate([c_sample, c_prompt, jnp.zeros((MOD_ROWS - DEC_BATCH - BATCH, D), F32)], axis=0)
    mod = _ada(c_all, w_ada, b_ada)

    xp = x_prompt.reshape(BATCH * SEQ, D)
    xs = x_sample.reshape(DEC_BATCH, D)
    cache_k = jnp.transpose(cache_swa_k, (0, 1, 3, 4, 2))
    cache_v = jnp.transpose(cache_swa_v, (0, 1, 3, 4, 2))
    gla_p, kp, vp, ksm, vsm, gmv = [], [], [], [], [], []
    state_out = None
    for l in range(DEPTH):
        xp, xs, s_p, state_out, k_p, v_p, k_s, v_s, gv = _layer(l, xp, xs, mod, p, state_gla, cache_k, cache_v,
                                                                state_out)
        gla_p.append(s_p)
        kp.append(k_p)
        vp.append(v_p)
        ksm.append(k_s)
        vsm.append(v_s)
        gmv.append(gv.reshape(DEC_BATCH, 1, GM_WIDTH))
    y_prompt = _final_norm(xp, final_norm_w).reshape(BATCH, SEQ, D)
    y_sample = _final_norm(xs, final_norm_w).reshape(DEC_BATCH, 1, D)
    return (y_prompt, y_sample, jnp.stack(gla_p), state_out, jnp.stack(kp), jnp.stack(vp),
            jnp.stack(ksm), jnp.stack(vsm), jnp.stack(gmv))
```

```python
import functools

import jax
import jax.numpy as jnp
import numpy as np
from jax import lax
from jax.experimental import pallas as pl
from jax.experimental.pallas import tpu as pltpu

F32 = jnp.float32
BF16 = jnp.bfloat16

D = 2048
BATCH, SEQ = 2, 4096
DEPTH = 2
DEC_BATCH = 128
GLA_H, GLA_DK, GLA_DV = 4, 256, 512
GLA_KEY, GLA_VAL = GLA_H * GLA_DK, GLA_H * GLA_DV
GLA_RANK = 16
GLA_CHUNK = 16
GLA_SC = 128
GM_WIDTH, GM_GROUPS, GM_CHUNK = 1024, 4, 128
GM_GW = GM_WIDTH // GM_GROUPS
SWA_HQ, SWA_HKV, SWA_HD, WINDOW = 16, 4, 64, 128
SWA_G = SWA_HQ // SWA_HKV
SWA_Q, SWA_KV = SWA_HQ * SWA_HD, SWA_HKV * SWA_HD
FFN_HIDDEN = 5632
EPS = 1e-6
NEG_BIG = -1e30

Z_QA, Z_KA, Z_VA, Z_GA = 0, 1024, 2048, 4096
Z_UB, Z_VB = 6144, 7168
Z_QC, Z_KC, Z_VC = 8192, 9216, 9472
Z_GATES = 9728
Z_WIDTH = 15872
LR_COL = 6144
LANE = 128
SUBLANE = 8
LR_PAD = LANE

MOD_SH1, MOD_SC1, MOD_G1, MOD_SH2, MOD_SC2, MOD_G2 = range(6)
MOD_ROWS = DEC_BATCH + 8

VMEM_LIMIT = 56 * 1024 * 1024


def _cparams(n_axes):
    return pltpu.CompilerParams(dimension_semantics=("arbitrary",) * n_axes,
                                vmem_limit_bytes=VMEM_LIMIT)


def _bf(x):
    return x if x.dtype == BF16 else x.astype(BF16)


def _dot(a, b):
    return jnp.dot(_bf(a), _bf(b), preferred_element_type=F32)


def _dot_nt(a, b):
    return lax.dot_general(_bf(a), _bf(b), (((1,), (1,)), ((), ())), preferred_element_type=F32)


def _silu(x):
    return x * (1.0 / (1.0 + jnp.exp(-x)))


def _sigmoid(x):
    return 1.0 / (1.0 + jnp.exp(-x))


def _gelu(x):
    return 0.5 * x * (1.0 + jnp.tanh(np.sqrt(2.0 / np.pi).astype(np.float32) * (x + 0.044715 * (x * x * x))))


def _mod_spec_prompt(layer, chunk, tm, tn):
    cb, bpb = chunk * D // tn, SEQ // tm
    return pl.BlockSpec((None, None, 1, tn), lambda i, j: (layer, DEC_BATCH + i // bpb, 0, j + cb))


def _mod_spec_sample(layer, chunk, tn, jmap):
    cb = chunk * D // tn
    return pl.BlockSpec((None, DEC_BATCH, tn), lambda i, j: (layer, 0, jmap(i, j) + cb))


def _ada_kernel(c_ref, w_ref, b_ref, o_ref):
    o_ref[...] = _dot(_silu(c_ref[...]), w_ref[...]) + b_ref[...]


def _ada(c_all, w_ada, b_ada):
    tn = 1024
    return pl.pallas_call(
        _ada_kernel,
        grid=(DEPTH, 6 * D // tn),
        in_specs=[pl.BlockSpec((MOD_ROWS, D), lambda l, j: (0, 0)),
                  pl.BlockSpec((None, D, tn), lambda l, j: (l, 0, j)),
                  pl.BlockSpec((None, 1, tn), lambda l, j: (l, 0, j))],
        out_specs=pl.BlockSpec((None, MOD_ROWS, tn), lambda l, j: (l, 0, j)),
        out_shape=jax.ShapeDtypeStruct((DEPTH, MOD_ROWS, 6 * D), F32),
        compiler_params=_cparams(2),
        name="ada",
    )(c_all, w_ada, b_ada.reshape(DEPTH, 1, 6 * D))


def _prep_kernel(x_ref, nw_ref, sc_ref, sh_ref, *rest):
    x = x_ref[...]
    y = x * lax.rsqrt(jnp.mean(x * x, axis=-1, keepdims=True) + EPS) * nw_ref[...]
    h = (y * (1.0 + sc_ref[...]) + sh_ref[...]).astype(BF16)
    if len(rest) == 1:
        (o_ref,) = rest
    else:
        wlr_ref, o_ref, lr_ref = rest
        lr_ref[...] = _dot_nt(h, wlr_ref[...])
    o_ref[...] = h


def _prep(layer, x, norm_w, mod, sc_chunk, sh_chunk, sample, w_lr_t=None):
    rows = x.shape[0]
    if sample:
        tm, modop = rows, mod
        mod_specs = [_mod_spec_sample(layer, c, D, lambda i, j: j) for c in (sc_chunk, sh_chunk)]
    else:
        tm, modop = 512, mod.reshape(DEPTH, MOD_ROWS, 1, 6 * D)
        mod_specs = [_mod_spec_prompt(layer, c, tm, D) for c in (sc_chunk, sh_chunk)]
    args = [x, norm_w.reshape(DEPTH, 1, D), modop, modop]
    in_specs = [pl.BlockSpec((tm, D), lambda i, j: (i, 0)),
                pl.BlockSpec((None, 1, D), lambda i, j: (layer, 0, 0))] + mod_specs
    out_specs = [pl.BlockSpec((tm, D), lambda i, j: (i, 0))]
    out_shape = [jax.ShapeDtypeStruct((rows, D), BF16)]
    if w_lr_t is not None:
        args.append(w_lr_t)
        in_specs.append(pl.BlockSpec((None, LR_PAD, D), lambda i, j: (layer, 0, 0)))
        out_specs.append(pl.BlockSpec((tm, LR_PAD), lambda i, j: (i, 0)))
        out_shape.append(jax.ShapeDtypeStruct((rows, LR_PAD), F32))
    res = pl.pallas_call(
        _prep_kernel,
        grid=(rows // tm, 1),
        in_specs=in_specs,
        out_specs=out_specs,
        out_shape=out_shape,
        compiler_params=_cparams(2),
        name="prep",
    )(*args)
    return res if w_lr_t is not None else res[0]


def _final_norm_kernel(x_ref, nw_ref, o_ref):
    x = x_ref[...]
    o_ref[...] = x * lax.rsqrt(jnp.mean(x * x, axis=-1, keepdims=True) + EPS) * nw_ref[...]


def _final_norm(x, w):
    rows = x.shape[0]
    tm = min(rows, 512)
    return pl.pallas_call(
        _final_norm_kernel,
        grid=(rows // tm,),
        in_specs=[pl.BlockSpec((tm, D), lambda i: (i, 0)), pl.BlockSpec((1, D), lambda i: (0, 0))],
        out_specs=pl.BlockSpec((tm, D), lambda i: (i, 0)),
        out_shape=jax.ShapeDtypeStruct((rows, D), F32),
        compiler_params=_cparams(1),
        name="final_norm",
    )(x, w.reshape(1, D))


def _mm_kernel(*refs, n_a, term_a, term_t, n_extra, n_out, epilogue):
    sizes = (n_a, n_a, len(term_a), n_extra, n_extra, n_out, n_out)
    groups, pos = [], 0
    for n in sizes:
        groups.append(refs[pos:pos + n])
        pos += n
    a_p, a_s, w_refs, e_p, e_s, o_p, o_s = groups
    w_vals = [_bf(w[...]) for w in w_refs]

    def run(a_refs, e_refs, o_refs):
        a_vals = [_bf(a[...]) for a in a_refs]
        dots = [(_dot_nt if t else _dot)(a_vals[ai], w) for ai, t, w in zip(term_a, term_t, w_vals)]
        outs = epilogue(dots, [e[...] for e in e_refs])
        for o_ref, o in zip(o_refs, outs):
            o_ref[...] = o.astype(o_ref.dtype)

    run(a_p, e_p, o_p)

    @pl.when(pl.program_id(0) == 0)
    def _():
        run(a_s, e_s, o_s)


def _fused_matmul(name, layer, a_p, a_s, terms, extras, epilogue, out_dtypes, n_cols, tm, tn):
    rows_p, rows_s = a_p[0].shape[0], a_s[0].shape[0]
    nj = n_cols // tn
    grid = (rows_p // tm, nj)
    sj = lambda i, j: jnp.where(i == 0, j, nj - 1)
    args, in_specs = [], []
    for a in a_p:
        args.append(a)
        in_specs.append(pl.BlockSpec((tm, a.shape[1]), lambda i, j: (i, 0), pipeline_mode=pl.Buffered(1)))
    for a in a_s:
        args.append(a)
        in_specs.append(pl.BlockSpec((rows_s, a.shape[1]), lambda i, j: (0, 0)))
    for ai, w, col0, transposed in terms:
        assert col0 % tn == 0 and w.shape[-1 if transposed else -2] == a_p[ai].shape[1]
        args.append(w)
        if transposed:
            in_specs.append(pl.BlockSpec((None, tn, w.shape[-1]), lambda i, j, cb=col0 // tn: (layer, j + cb, 0)))
        else:
            in_specs.append(pl.BlockSpec((None, w.shape[-2], tn), lambda i, j, cb=col0 // tn: (layer, 0, j + cb)))
    s_args, s_specs = [], []
    for ex in extras:
        if ex[0] == "tile":
            _, arr_p, arr_s, col0 = ex
            assert col0 % tn == 0
            args.append(arr_p)
            in_specs.append(pl.BlockSpec((tm, tn), lambda i, j, cb=col0 // tn: (i, j + cb)))
            s_args.append(arr_s)
            s_specs.append(pl.BlockSpec((rows_s, tn), lambda i, j, cb=col0 // tn: (0, sj(i, j) + cb)))
        else:
            _, mod, chunk = ex
            args.append(mod.reshape(DEPTH, MOD_ROWS, 1, 6 * D))
            in_specs.append(_mod_spec_prompt(layer, chunk, tm, tn))
            s_args.append(mod)
            s_specs.append(_mod_spec_sample(layer, chunk, tn, sj))
    kern = functools.partial(_mm_kernel, n_a=len(a_p), term_a=tuple(t[0] for t in terms),
                             term_t=tuple(t[3] for t in terms), n_extra=len(extras), n_out=len(out_dtypes),
                             epilogue=epilogue)
    res = pl.pallas_call(
        kern,
        grid=grid,
        in_specs=in_specs + s_specs,
        out_specs=([pl.BlockSpec((tm, tn), lambda i, j: (i, j)) for _ in out_dtypes]
                   + [pl.BlockSpec((rows_s, tn), lambda i, j: (0, sj(i, j))) for _ in out_dtypes]),
        out_shape=([jax.ShapeDtypeStruct((rows_p, n_cols), dt) for dt in out_dtypes]
                   + [jax.ShapeDtypeStruct((rows_s, n_cols), dt) for dt in out_dtypes]),
        compiler_params=_cparams(2),
        name=name,
    )(*args, *s_args)
    return res[:len(out_dtypes)], res[len(out_dtypes):]


def _epi_plain(dots, ex):
    return [dots[0]]


def _epi_merge(dots, ex):
    return [_sigmoid(ex[0]) * dots[0] + _sigmoid(ex[1]) * dots[1] + _sigmoid(ex[2]) * dots[2]]


def _epi_residual(dots, ex):
    return [ex[0] + ex[1] * dots[0]]


def _epi_swiglu(dots, ex):
    return [_silu(dots[0]) * dots[1]]


def _log_sigmoid(u):
    return -(jnp.maximum(-u, 0.0) + jnp.log1p(jnp.exp(-jnp.abs(u))))


def _gla_prompt_kernel(q_ref, k_ref, v_ref, ga_ref, lr_ref, wgk_ref, bgk_ref, nw_ref,
                       oa_ref, sfin_ref, st_s, vt_s, o_s, *, tb):
    t = pl.program_id(1)
    nsc = tb // GLA_SC
    nd = GLA_SC // GLA_CHUNK - 1

    @pl.when(t == 0)
    def _():
        st_s[...] = jnp.zeros_like(st_s)

    u = _dot(lr_ref[...], wgk_ref[...]) + bgk_ref[...]
    gk = _log_sigmoid(u) * (1.0 / 16.0)
    sub = lax.broadcasted_iota(jnp.int32, gk.shape, 0) % SUBLANE
    p8 = gk
    for s in (1, 2, 4):
        p8 = p8 + jnp.where(sub >= s, pltpu.roll(p8, s, 0), 0.0)
    nchunk = tb // GLA_CHUNK
    b_parts, bs_parts, tot = [], [], []
    acc = None
    for c in range(nchunk):
        r0 = c * GLA_CHUNK
        lo = p8[r0:r0 + SUBLANE, :]
        hi = p8[r0 + SUBLANE:r0 + GLA_CHUNK, :] + lo[SUBLANE - 1:SUBLANE, :]
        if c % (GLA_SC // GLA_CHUNK) == 0:
            b_parts += [lo, hi]
            bs_parts += [lo, hi]
            acc = hi[SUBLANE - 1:SUBLANE, :]
        else:
            b_parts += [lo, hi]
            bs_parts += [lo + acc, hi + acc]
            acc = acc + hi[SUBLANE - 1:SUBLANE, :]
        tot.append(hi[SUBLANE - 1:SUBLANE, :])
    b = jnp.concatenate(b_parts, axis=0)
    bs = jnp.concatenate(bs_parts, axis=0)

    def per_chunk(vals, shift):
        return jnp.concatenate([jnp.broadcast_to(vals[(c + shift) % nchunk], (GLA_CHUNK, GLA_KEY))
                                for c in range(nchunk)], axis=0)

    blb = per_chunk(tot, 0)
    etot = [jnp.exp(t) for t in tot]

    q = q_ref[...] * (GLA_DK ** -0.5)
    k = k_ref[...]
    qin = _bf(q * jnp.exp(b))
    kout = _bf(k * jnp.exp(-b))
    kd = k * jnp.exp(blb - b)
    qsc = _bf(q * jnp.exp(bs))
    vt_s[...] = v_ref[...].T

    ri = lax.broadcasted_iota(jnp.int32, (tb, tb), 0)
    ci = lax.broadcasted_iota(jnp.int32, (tb, tb), 1)
    delta = jnp.where(ri // GLA_SC == ci // GLA_SC, ri // GLA_CHUNK - ci // GLA_CHUNK, -1)
    m_intra = (delta == 0) & (ci <= ri)
    kds = [_bf(kd)]
    for d in range(1, nd):
        kd = kd * per_chunk(etot, d)
        kds.append(_bf(kd))
    for h in range(GLA_H):
        ks = slice(h * GLA_DK, (h + 1) * GLA_DK)
        vs = slice(h * GLA_DV, (h + 1) * GLA_DV)
        a = jnp.where(m_intra, _dot_nt(qin[:, ks], kout[:, ks]), 0.0)
        for d in range(nd):
            a = jnp.where(delta == d + 1, _dot_nt(qin[:, ks], kds[d][:, ks]), a)
        o_s[:, vs] = _dot(a, v_ref[:, vs])

    for sc in range(nsc):
        rows = slice(sc * GLA_SC, (sc + 1) * GLA_SC)
        last = bs[(sc + 1) * GLA_SC - 1:(sc + 1) * GLA_SC, :]
        k2 = _bf(k[rows, :] * jnp.exp(last - bs[rows, :]))
        elast = jnp.exp(last)
        for h in range(GLA_H):
            ks = slice(h * GLA_DK, (h + 1) * GLA_DK)
            vs = slice(h * GLA_DV, (h + 1) * GLA_DV)
            st = st_s[h]
            o_s[rows, vs] += _dot_nt(qsc[rows, ks], st)
            st_s[h] = st * elast[:, ks] + _dot(vt_s[vs, rows], k2[:, ks])

    nw = nw_ref[...]
    for h in range(GLA_H):
        vs = slice(h * GLA_DV, (h + 1) * GLA_DV)
        o = o_s[:, vs]
        y = o * lax.rsqrt(jnp.mean(o * o, axis=-1, keepdims=True) + EPS) * nw
        oa_ref[:, vs] = (y * _silu(ga_ref[:, vs])).astype(oa_ref.dtype)

    @pl.when(t == pl.num_programs(1) - 1)
    def _():
        for h in range(GLA_H):
            sfin_ref[h] = st_s[h].T


def _gla_prompt(layer, z, lr, wgk_pad, b_gk, gla_norm_w):
    tb = 256
    nt = SEQ // tb
    row = lambda b, t: b * nt + t
    kern = functools.partial(_gla_prompt_kernel, tb=tb)
    return pl.pallas_call(
        kern,
        grid=(BATCH, nt),
        in_specs=[pl.BlockSpec((tb, GLA_KEY), lambda b, t: (row(b, t), Z_QA // GLA_KEY)),
                  pl.BlockSpec((tb, GLA_KEY), lambda b, t: (row(b, t), Z_KA // GLA_KEY)),
                  pl.BlockSpec((tb, GLA_VAL), lambda b, t: (row(b, t), Z_VA // GLA_VAL)),
                  pl.BlockSpec((tb, GLA_VAL), lambda b, t: (row(b, t), Z_GA // GLA_VAL)),
                  pl.BlockSpec((tb, LR_PAD), lambda b, t: (row(b, t), 0)),
                  pl.BlockSpec((None, LR_PAD, GLA_KEY), lambda b, t: (layer, 0, 0)),
                  pl.BlockSpec((None, 1, GLA_KEY), lambda b, t: (layer, 0, 0)),
                  pl.BlockSpec((None, 1, GLA_DV), lambda b, t: (layer, 0, 0))],
        out_specs=[pl.BlockSpec((tb, GLA_VAL), lambda b, t: (row(b, t), 0)),
                   pl.BlockSpec((None, GLA_H, GLA_DK, GLA_DV), lambda b, t: (b, 0, 0, 0))],
        out_shape=[jax.ShapeDtypeStruct((BATCH * SEQ, GLA_VAL), BF16),
                   jax.ShapeDtypeStruct((BATCH, GLA_H, GLA_DK, GLA_DV), F32)],
        scratch_shapes=[pltpu.VMEM((GLA_H, GLA_DV, GLA_DK), F32),
                        pltpu.VMEM((GLA_VAL, tb), F32),
                        pltpu.VMEM((tb, GLA_VAL), F32)],
        compiler_params=_cparams(2),
        name="gla_prompt",
    )(z, z, z, z, lr, wgk_pad, b_gk.reshape(DEPTH, 1, GLA_KEY), gla_norm_w.reshape(DEPTH, 1, GLA_DV))


def _layernorm(x, w, b):
    mu = jnp.mean(x, axis=-1, keepdims=True)
    xc = x - mu
    var = jnp.mean(xc * xc, axis=-1, keepdims=True)
    return xc * lax.rsqrt(var + EPS) * w + b


def _gmlp_prompt_kernel(u_ref, v_ref, ws_ref, bst_ref, nw_ref, nb_ref, ob_ref, *, nsub):
    ri = lax.broadcasted_iota(jnp.int32, (GM_CHUNK, GM_CHUNK), 0)
    ci = lax.broadcasted_iota(jnp.int32, (GM_CHUNK, GM_CHUNK), 1)
    tril = ci <= ri
    for s in range(nsub):
        rs = slice(s * GM_CHUNK, (s + 1) * GM_CHUNK)
        u = _gelu(u_ref[rs, :])
        v = _layernorm(_gelu(v_ref[rs, :]), nw_ref[...], nb_ref[...])
        for g in range(GM_GROUPS):
            cs = slice(g * GM_GW, (g + 1) * GM_GW)
            wm = jnp.where(tril, ws_ref[g], 0.0)
            mixed = _dot(wm, v[:, cs]) + bst_ref[:, g:g + 1]
            ob_ref[rs, cs] = (u[:, cs] * mixed).astype(ob_ref.dtype)


def _gmlp_prompt(layer, z, gm_ws, gm_bs_t, gm_norm_w, gm_norm_b):
    nsub = 4
    tb = nsub * GM_CHUNK
    kern = functools.partial(_gmlp_prompt_kernel, nsub=nsub)
    return pl.pallas_call(
        kern,
        grid=(BATCH * SEQ // tb,),
        in_specs=[pl.BlockSpec((tb, GM_WIDTH), lambda i: (i, Z_UB // GM_WIDTH)),
                  pl.BlockSpec((tb, GM_WIDTH), lambda i: (i, Z_VB // GM_WIDTH)),
                  pl.BlockSpec((None, GM_GROUPS, GM_CHUNK, GM_CHUNK), lambda i: (layer, 0, 0, 0)),
                  pl.BlockSpec((None, GM_CHUNK, GM_GROUPS), lambda i: (layer, 0, 0)),
                  pl.BlockSpec((None, 1, GM_WIDTH), lambda i: (layer, 0, 0)),
                  pl.BlockSpec((None, 1, GM_WIDTH), lambda i: (layer, 0, 0))],
        out_specs=pl.BlockSpec((tb, GM_WIDTH), lambda i: (i, 0)),
        out_shape=jax.ShapeDtypeStruct((BATCH * SEQ, GM_WIDTH), BF16),
        compiler_params=_cparams(1),
        name="gmlp_prompt",
    )(z, z, gm_ws, gm_bs_t, gm_norm_w.reshape(DEPTH, 1, GM_WIDTH), gm_norm_b.reshape(DEPTH, 1, GM_WIDTH))


def _gmlp_sample_kernel(u_ref, v_ref, w0_ref, b0_ref, nw_ref, nb_ref, ob_ref, vn_ref):
    u = _gelu(u_ref[...])
    v = _layernorm(_gelu(v_ref[...]), nw_ref[...], nb_ref[...])
    vn_ref[...] = v
    ob_ref[...] = u * (w0_ref[...] * v + b0_ref[...])


def _gmlp_sample(layer, z, w0_row, b0_row, gm_norm_w, gm_norm_b):
    full = lambda i: (0, 0)
    lrow = lambda i: (layer, 0, 0)
    return pl.pallas_call(
        _gmlp_sample_kernel,
        grid=(1,),
        in_specs=[pl.BlockSpec((DEC_BATCH, GM_WIDTH), lambda i: (0, Z_UB // GM_WIDTH)),
                  pl.BlockSpec((DEC_BATCH, GM_WIDTH), lambda i: (0, Z_VB // GM_WIDTH)),
                  pl.BlockSpec((None, 1, GM_WIDTH), lrow),
                  pl.BlockSpec((None, 1, GM_WIDTH), lrow),
                  pl.BlockSpec((None, 1, GM_WIDTH), lrow),
                  pl.BlockSpec((None, 1, GM_WIDTH), lrow)],
        out_specs=[pl.BlockSpec((DEC_BATCH, GM_WIDTH), full), pl.BlockSpec((DEC_BATCH, GM_WIDTH), full)],
        out_shape=[jax.ShapeDtypeStruct((DEC_BATCH, GM_WIDTH), F32),
                   jax.ShapeDtypeStruct((DEC_BATCH, GM_WIDTH), F32)],
        compiler_params=_cparams(1),
        name="gmlp_sample",
    )(z, z, w0_row, b0_row, gm_norm_w.reshape(DEPTH, 1, GM_WIDTH), gm_norm_b.reshape(DEPTH, 1, GM_WIDTH))


def _alibi_slope(h):
    return float(2.0 ** (-8.0 * (h + 1) / SWA_HQ))


def _swa_lane_halves(x, half):
    lane = lax.broadcasted_iota(jnp.int32, x.shape, 1)
    own = jnp.where((lane >= half * SWA_HD) & (lane < (half + 1) * SWA_HD), x, 0.0)
    other = pltpu.roll(own, SWA_HD, 1)
    return (own, other) if half == 0 else (other, own)


def _swa_prompt_block(n, q_ref, kc_ref, kp_ref, vc_ref, vp_ref, sink_ref, oc_ref, s_s, p_s):
    w = WINDOW
    ri = lax.broadcasted_iota(jnp.int32, (w, 2 * w), 0)
    ci = lax.broadcasted_iota(jnp.int32, (w, 2 * w), 1)
    dist_i = w + ri - ci
    valid = (dist_i >= 0) & (dist_i < w) & ((ci >= w) | (n > 0))
    dist = dist_i.astype(F32)
    kcat = jnp.concatenate([kp_ref[...], kc_ref[...]], axis=0)
    vcat = jnp.concatenate([vp_ref[...], vc_ref[...]], axis=0)
    heads = []
    for kv in range(SWA_HKV):
        t, half = kv // 2, kv % 2
        k_lo, k_hi = _swa_lane_halves(kcat[:, t * LANE:(t + 1) * LANE], half)
        q2 = jnp.concatenate([q_ref[:, 2 * kv * LANE:(2 * kv + 1) * LANE],
                              q_ref[:, (2 * kv + 1) * LANE:(2 * kv + 2) * LANE]], axis=0)
        for par, kk in ((0, k_lo), (1, k_hi)):
            s = _dot_nt(q2, kk) * (SWA_HD ** -0.5)
            for e in range(2):
                h = SWA_G * kv + 2 * e + par
                seg = len(heads)
                heads.append(h)
                s_s[seg * w:(seg + 1) * w, :] = jnp.where(
                    valid, s[e * w:(e + 1) * w, :] - _alibi_slope(h) * dist, NEG_BIG)
    s = s_s[...]
    sink = jnp.concatenate([jnp.broadcast_to(sink_ref[h:h + 1, 0:1], (w, 1)) for h in heads], axis=0)
    m = jnp.maximum(jnp.max(s, axis=-1, keepdims=True), sink)
    p = jnp.exp(s - m)
    inv = 1.0 / (jnp.sum(p, axis=-1, keepdims=True) + jnp.exp(sink - m))
    p_s[...] = (p * inv).astype(p_s.dtype)
    for kv in range(SWA_HKV):
        t, half = kv // 2, kv % 2
        v_lo, v_hi = _swa_lane_halves(vcat[:, t * LANE:(t + 1) * LANE], half)
        r0 = SWA_G * kv * w
        o = _dot(p_s[r0:r0 + 2 * w, :], v_lo) + _dot(p_s[r0 + 2 * w:r0 + 4 * w, :], v_hi)
        oc_ref[:, 2 * kv * LANE:(2 * kv + 1) * LANE] = o[:w].astype(oc_ref.dtype)
        oc_ref[:, (2 * kv + 1) * LANE:(2 * kv + 2) * LANE] = o[w:].astype(oc_ref.dtype)


STATE_SLOTS = 3


def _state_copy(hbm, sbuf, sem, layer, step, rps, to_hbm):
    slot = lax.rem(step, STATE_SLOTS)
    rows = hbm.at[layer, pl.ds(step * rps, rps)]
    if to_hbm:
        return pltpu.make_async_copy(sbuf.at[slot], rows, sem.at[slot])
    return pltpu.make_async_copy(rows, sbuf.at[slot], sem.at[slot])


def _state_ring_begin(g, n_steps, copy_in, copy_out, x_s, oa_s):
    @pl.when(g == 0)
    def _():
        x_s[...] = jnp.zeros_like(x_s)
        oa_s[...] = jnp.zeros_like(oa_s)
        copy_in(0).start()

    @pl.when(g >= STATE_SLOTS - 1)
    def _():
        copy_out(g - (STATE_SLOTS - 1)).wait()

    @pl.when(g + 1 < n_steps)
    def _():
        copy_in(g + 1).start()

    copy_in(g).wait()


def _state_ring_end(g, n_steps, copy_out):
    copy_out(g).start()

    @pl.when(g == n_steps - 1)
    def _():
        for back in range(STATE_SLOTS - 2, -1, -1):
            copy_out(g - back).wait()


def _gla_decode_rows(g, rps, zq_ref, zk_ref, zv_ref, zga_ref, lr_ref, wgk_ref, bgk_ref, nw_ref,
                     oa_ref, sbuf, x_s, oa_s):
    slot = lax.rem(g, STATE_SLOTS)
    sub = lax.rem(g * rps, SUBLANE)
    up = lax.rem(SUBLANE - sub, SUBLANE)
    u = _dot(lr_ref[...], wgk_ref[...]) + bgk_ref[...]
    eg = pltpu.roll(jnp.exp(_log_sigmoid(u) * (1.0 / 16.0)), up, 0)
    zq = pltpu.roll(zq_ref[...], up, 0) * (GLA_DK ** -0.5)
    zk = pltpu.roll(zk_ref[...], up, 0)
    zv = pltpu.roll(zv_ref[...], up, 0)
    gate = _silu(pltpu.roll(zga_ref[...], up, 0))
    nw = nw_ref[...]
    row8 = lax.broadcasted_iota(jnp.int32, (SUBLANE, GLA_DV), 0)
    for h in range(GLA_H):
        ks = slice(h * GLA_DK, (h + 1) * GLA_DK)
        vs = slice(h * GLA_DV, (h + 1) * GLA_DV)
        x_s[0:SUBLANE, :] = zq[:, ks]
        x_s[SUBLANE:2 * SUBLANE, :] = zk[:, ks]
        x_s[2 * SUBLANE:3 * SUBLANE, :] = eg[:, ks]
        xt = x_s[...].T
        y8 = jnp.zeros((SUBLANE, GLA_DV), F32)
        for r in range(rps):
            qc = xt[:, r:r + 1]
            kc = xt[:, SUBLANE + r:SUBLANE + r + 1]
            gc = xt[:, 2 * SUBLANE + r:2 * SUBLANE + r + 1]
            s_new = gc * sbuf[slot, r, h] + kc * zv[r:r + 1, vs]
            sbuf[slot, r, h] = s_new
            o = jnp.sum(qc * s_new, axis=0, keepdims=True)
            y = o * lax.rsqrt(jnp.mean(o * o, axis=-1, keepdims=True) + EPS) * nw * gate[r:r + 1, vs]
            y8 = jnp.where(row8 == r, y, y8)
        y8 = pltpu.roll(y8, sub, 0)
        acc = jnp.where(sub == 0, y8, oa_s[:, vs] + y8)
        oa_s[:, vs] = acc
        oa_ref[:, vs] = acc


def _swa_gla_kernel(q_ref, kc_ref, kp_ref, vc_ref, vp_ref, sink_ref,
                    zq_ref, zk_ref, zv_ref, zga_ref, lr_ref, wgk_ref, bgk_ref, nw_ref, sin_hbm, *rest,
                    layer, rps, aliased):
    if aliased:
        rest = rest[1:]
    oc_ref, oa_ref, sout_hbm, s_s, p_s, sbuf, x_s, oa_s, in_sem, out_sem = rest
    n = pl.program_id(1)
    g = pl.program_id(0) * pl.num_programs(1) + n
    n_steps = pl.num_programs(0) * pl.num_programs(1)
    copy_in = functools.partial(_state_copy, sin_hbm, sbuf, in_sem, layer, rps=rps, to_hbm=False)
    copy_out = functools.partial(_state_copy, sout_hbm, sbuf, out_sem, layer, rps=rps, to_hbm=True)
    _state_ring_begin(g, n_steps, copy_in, copy_out, x_s, oa_s)
    _gla_decode_rows(g, rps, zq_ref, zk_ref, zv_ref, zga_ref, lr_ref, wgk_ref, bgk_ref, nw_ref,
                     oa_ref, sbuf, x_s, oa_s)
    _swa_prompt_block(n, q_ref, kc_ref, kp_ref, vc_ref, vp_ref, sink_ref, oc_ref, s_s, p_s)
    _state_ring_end(g, n_steps, copy_out)


def _swa_prompt_gla_sample(layer, zp, sinks_b, zs, lrs, wgk_pad, b_gk, gla_norm_w, state_gla, state_out):
    nb = SEQ // WINDOW
    n_steps = BATCH * nb
    rps = DEC_BATCH // n_steps
    assert rps * n_steps == DEC_BATCH and SUBLANE % rps == 0 and n_steps >= STATE_SLOTS
    row = lambda b, n: b * nb + n
    prev = lambda b, n: b * nb + jnp.maximum(n - 1, 0)
    srow = lambda b, n: (row(b, n) * rps) // SUBLANE
    in_specs = [pl.BlockSpec((WINDOW, SWA_Q), lambda b, n: (row(b, n), Z_QC // SWA_Q)),
                pl.BlockSpec((WINDOW, SWA_KV), lambda b, n: (row(b, n), Z_KC // SWA_KV)),
                pl.BlockSpec((WINDOW, SWA_KV), lambda b, n: (prev(b, n), Z_KC // SWA_KV)),
                pl.BlockSpec((WINDOW, SWA_KV), lambda b, n: (row(b, n), Z_VC // SWA_KV)),
                pl.BlockSpec((WINDOW, SWA_KV), lambda b, n: (prev(b, n), Z_VC // SWA_KV)),
                pl.BlockSpec((None, SWA_HQ, LANE), lambda b, n: (layer, 0, 0)),
                pl.BlockSpec((SUBLANE, GLA_KEY), lambda b, n: (srow(b, n), Z_QA // GLA_KEY)),
                pl.BlockSpec((SUBLANE, GLA_KEY), lambda b, n: (srow(b, n), Z_KA // GLA_KEY)),
                pl.BlockSpec((SUBLANE, GLA_VAL), lambda b, n: (srow(b, n), Z_VA // GLA_VAL)),
                pl.BlockSpec((SUBLANE, GLA_VAL), lambda b, n: (srow(b, n), Z_GA // GLA_VAL)),
                pl.BlockSpec((SUBLANE, LR_PAD), lambda b, n: (srow(b, n), 0)),
                pl.BlockSpec((None, LR_PAD, GLA_KEY), lambda b, n: (layer, 0, 0)),
                pl.BlockSpec((None, 1, GLA_KEY), lambda b, n: (layer, 0, 0)),
                pl.BlockSpec((None, 1, GLA_DV), lambda b, n: (layer, 0, 0)),
                pl.BlockSpec(memory_space=pl.ANY)]
    args = [zp, zp, zp, zp, zp, sinks_b, zs, zs, zs, zs, lrs, wgk_pad, b_gk.reshape(DEPTH, 1, GLA_KEY),
            gla_norm_w.reshape(DEPTH, 1, GLA_DV), state_gla]
    aliases = {}
    if state_out is not None:
        in_specs.append(pl.BlockSpec(memory_space=pl.ANY))
        args.append(state_out)
        aliases = {len(args) - 1: 2}
    kern = functools.partial(_swa_gla_kernel, layer=layer, rps=rps, aliased=state_out is not None)
    return pl.pallas_call(
        kern,
        grid=(BATCH, nb),
        in_specs=in_specs,
        out_specs=[pl.BlockSpec((WINDOW, SWA_Q), lambda b, n: (row(b, n), 0)),
                   pl.BlockSpec((SUBLANE, GLA_VAL), lambda b, n: (srow(b, n), 0)),
                   pl.BlockSpec(memory_space=pl.ANY)],
        out_shape=[jax.ShapeDtypeStruct((BATCH * SEQ, SWA_Q), BF16),
                   jax.ShapeDtypeStruct((DEC_BATCH, GLA_VAL), F32),
                   jax.ShapeDtypeStruct((DEPTH, DEC_BATCH, GLA_H, GLA_DK, GLA_DV), F32)],
        scratch_shapes=[pltpu.VMEM((SWA_HQ * WINDOW, 2 * WINDOW), F32),
                        pltpu.VMEM((SWA_HQ * WINDOW, 2 * WINDOW), BF16),
                        pltpu.VMEM((STATE_SLOTS, rps, GLA_H, GLA_DK, GLA_DV), F32),
                        pltpu.VMEM((LANE, GLA_DK), F32),
                        pltpu.VMEM((SUBLANE, GLA_VAL), F32),
                        pltpu.SemaphoreType.DMA((STATE_SLOTS,)),
                        pltpu.SemaphoreType.DMA((STATE_SLOTS,))],
        input_output_aliases=aliases,
        compiler_params=_cparams(2),
        name="swa_prompt_gla_sample",
    )(*args)


def _swa_sample_kernel(q_ref, kn_ref, vn_ref, kt_ref, vt_ref, sink_ref, slope_ref, o_ref, *, rb):
    wb = WINDOW
    j = lax.broadcasted_iota(jnp.int32, (SWA_HQ, wb), 1)
    dist = (wb - j).astype(F32)
    ok = j >= 1
    grp = lax.broadcasted_iota(jnp.int32, (SWA_HQ, 1), 0) // SWA_G
    slope = slope_ref[:, 0:1]
    sink = sink_ref[:, 0:1]

    def per_head(rows):
        out = jnp.broadcast_to(rows[0:1, :], (SWA_HQ, SWA_HD))
        for kv in range(1, SWA_HKV):
            out = jnp.where(grp == kv, rows[kv:kv + 1, :], out)
        return out

    s_rows, self_rows = [], []
    for r in range(rb):
        q = q_ref[r]
        s = _dot(q, kt_ref[r, 0])
        for kv in range(1, SWA_HKV):
            s = jnp.where(grp == kv, _dot(q, kt_ref[r, kv]), s)
        s_rows.append(jnp.where(ok, s * (SWA_HD ** -0.5) - slope * dist, NEG_BIG))
        self_rows.append(jnp.sum(_bf(q).astype(F32) * _bf(per_head(kn_ref[r])).astype(F32), axis=-1,
                                 keepdims=True) * (SWA_HD ** -0.5))
    s = jnp.concatenate(s_rows, axis=0)
    s_self = jnp.concatenate(self_rows, axis=0)
    sink = jnp.concatenate([sink] * rb, axis=0)
    m = jnp.maximum(jnp.maximum(jnp.max(s, axis=-1, keepdims=True), s_self), sink)
    p = jnp.exp(s - m)
    p_self = jnp.exp(s_self - m)
    inv = 1.0 / (jnp.sum(p, axis=-1, keepdims=True) + p_self + jnp.exp(sink - m))
    pn = _bf(p * inv)
    pn_self = _bf(p_self * inv).astype(F32)
    for r in range(rb):
        rows = slice(r * SWA_HQ, (r + 1) * SWA_HQ)
        o = _dot_nt(pn[rows, :], vt_ref[r, 0])
        for kv in range(1, SWA_HKV):
            o = jnp.where(grp == kv, _dot_nt(pn[rows, :], vt_ref[r, kv]), o)
        o_ref[r] = o + pn_self[rows, :] * _bf(per_head(vn_ref[r])).astype(F32)


def _swa_sample(layer, q3, kn3, vn3, cache_kt, cache_vt, sinks_b, slopes_b):
    rb = 8
    kern = functools.partial(_swa_sample_kernel, rb=rb)
    return pl.pallas_call(
        kern,
        grid=(DEC_BATCH // rb,),
        in_specs=[pl.BlockSpec((rb, SWA_HQ, SWA_HD), lambda i: (i, 0, 0)),
                  pl.BlockSpec((rb, SWA_HKV, SWA_HD), lambda i: (i, 0, 0)),
                  pl.BlockSpec((rb, SWA_HKV, SWA_HD), lambda i: (i, 0, 0)),
                  pl.BlockSpec((None, rb, SWA_HKV, SWA_HD, WINDOW), lambda i: (layer, i, 0, 0, 0)),
                  pl.BlockSpec((None, rb, SWA_HKV, SWA_HD, WINDOW), lambda i: (layer, i, 0, 0, 0)),
                  pl.BlockSpec((None, SWA_HQ, LANE), lambda i: (layer, 0, 0)),
                  pl.BlockSpec((SWA_HQ, LANE), lambda i: (0, 0))],
        out_specs=pl.BlockSpec((rb, SWA_HQ, SWA_HD), lambda i: (i, 0, 0)),
        out_shape=jax.ShapeDtypeStruct((DEC_BATCH, SWA_HQ, SWA_HD), F32),
        compiler_params=_cparams(1),
        name="swa_sample",
    )(q3, kn3, vn3, cache_kt, cache_vt, sinks_b, slopes_b)


def _repack_kernel(a_ref, b_ref, o_ref, *, n_plain):
    j = pl.program_id(1)

    @pl.when(j < n_plain)
    def _():
        o_ref[...] = a_ref[...].T.astype(o_ref.dtype)

    @pl.when(j >= n_plain)
    def _():
        o_ref[...] = jnp.concatenate([a_ref[GLA_RANK:, :], b_ref[...]], axis=0).T.astype(o_ref.dtype)


def _repack_w_in_t(w_in_t):
    tn = 512
    assert LR_COL % tn == 0 and Z_WIDTH % tn == 0 and tn % GLA_RANK == 0
    kern = functools.partial(_repack_kernel, n_plain=LR_COL // tn)
    return pl.pallas_call(
        kern,
        grid=(DEPTH, Z_WIDTH // tn),
        in_specs=[pl.BlockSpec((None, tn, D), lambda l, j: (l, j, 0)),
                  pl.BlockSpec((None, GLA_RANK, D), lambda l, j: (l, (j + 1) * (tn // GLA_RANK), 0))],
        out_specs=pl.BlockSpec((None, D, tn), lambda l, j: (l, 0, j)),
        out_shape=jax.ShapeDtypeStruct((DEPTH, D, Z_WIDTH), BF16),
        compiler_params=_cparams(2),
        name="repack_w_in",
    )(w_in_t, w_in_t)


def _layer(layer, xp, xs, mod, p, state_gla, cache_k, cache_v, state_out):
    hp, lrp = _prep(layer, xp, p["norm1_w"], mod, MOD_SC1, MOD_SH1, False, p["w_lr_t"])
    hs, lrs = _prep(layer, xs, p["norm1_w"], mod, MOD_SC1, MOD_SH1, True, p["w_lr_t"])
    (zp,), (zs,) = _fused_matmul("w_in", layer, [hp], [hs], [(0, p["w_in_r"], 0, False)], [], _epi_plain, [F32],
                                 Z_WIDTH, 2048, 512)
    oa_p, s_p = _gla_prompt(layer, zp, lrp, p["wgk_pad"], p["b_gk"], p["gla_norm_w"])
    ob_p = _gmlp_prompt(layer, zp, p["gm_ws"], p["gm_bs_t"], p["gm_norm_w"], p["gm_norm_b"])
    oc_p, oa_s, state_out = _swa_prompt_gla_sample(layer, zp, p["sinks_b"], zs, lrs, p["wgk_pad"], p["b_gk"],
                                                   p["gla_norm_w"], state_gla, state_out)
    z4 = zp.reshape(BATCH, SEQ, Z_WIDTH)
    kp_rows = z4[:, SEQ - WINDOW:, Z_KC:Z_KC + SWA_KV].reshape(BATCH, WINDOW, SWA_HKV, SWA_HD)
    vp_rows = z4[:, SEQ - WINDOW:, Z_VC:Z_VC + SWA_KV].reshape(BATCH, WINDOW, SWA_HKV, SWA_HD)
    ob_s, v_gm = _gmlp_sample(layer, zs, p["gm_w0"], p["gm_b0"], p["gm_norm_w"], p["gm_norm_b"])
    q3 = zs[:, Z_QC:Z_QC + SWA_Q].reshape(DEC_BATCH, SWA_HQ, SWA_HD)
    kn3 = zs[:, Z_KC:Z_KC + SWA_KV].reshape(DEC_BATCH, SWA_HKV, SWA_HD)
    vn3 = zs[:, Z_VC:Z_VC + SWA_KV].reshape(DEC_BATCH, SWA_HKV, SWA_HD)
    oc_s = _swa_sample(layer, q3, kn3, vn3, cache_k, cache_v, p["sinks_b"], p["slopes_b"]).reshape(DEC_BATCH, SWA_Q)
    ks_rows = kn3.reshape(DEC_BATCH, 1, SWA_HKV, SWA_HD)
    vs_rows = vn3.reshape(DEC_BATCH, 1, SWA_HKV, SWA_HD)
    (mp,), (ms,) = _fused_matmul(
        "merge", layer, [oa_p, ob_p, oc_p], [oa_s, ob_s, oc_s],
        [(0, p["w_pa"], 0, False), (1, p["w_pb"], 0, False), (2, p["w_pc"], 0, False)],
        [("tile", zp, zs, Z_GATES), ("tile", zp, zs, Z_GATES + D), ("tile", zp, zs, Z_GATES + 2 * D)],
        _epi_merge, [BF16], D, 2048, 256)
    (x1p,), (x1s,) = _fused_matmul("w_o", layer, [mp], [ms], [(0, p["w_o"], 0, False)],
                                   [("tile", xp, xs, 0), ("mod", mod, MOD_G1)], _epi_residual, [F32], D, 2048, 512)
    h2p = _prep(layer, x1p, p["norm2_w"], mod, MOD_SC2, MOD_SH2, sample=False)
    h2s = _prep(layer, x1s, p["norm2_w"], mod, MOD_SC2, MOD_SH2, sample=True)
    (hidp,), (hids,) = _fused_matmul(
        "ffn_in", layer, [h2p], [h2s], [(0, p["w_ffn_in"], 0, False), (0, p["w_ffn_in"], FFN_HIDDEN, False)],
        [], _epi_swiglu, [BF16], FFN_HIDDEN, 2048, 512)
    (x2p,), (x2s,) = _fused_matmul("ffn_out", layer, [hidp], [hids], [(0, p["w_ffn_out"], 0, False)],
                                   [("tile", x1p, x1s, 0), ("mod", mod, MOD_G2)], _epi_residual, [F32],
                                   D, 1024, 512)
    return x2p, x2s, s_p, state_out, kp_rows, vp_rows, ks_rows, vs_rows, v_gm


def kernel(x_prompt, x_sample, c_prompt, c_sample, state_gla, cache_swa_k, cache_swa_v, w_ada, b_ada, norm1_w,
           norm2_w, w_in, w_gk2, b_gk, gla_norm_w, gm_norm_w, gm_norm_b, gm_ws, gm_bs, swa_sinks, w_pa, w_pb,
           w_pc, w_o, w_ffn_in, w_ffn_out, final_norm_w):
    w_in_t = jnp.swapaxes(w_in, 1, 2)
    w_lr_t = jnp.pad(w_in_t[:, LR_COL:LR_COL + GLA_RANK, :], ((0, 0), (0, LR_PAD - GLA_RANK), (0, 0))).astype(BF16)
    p = {
        "norm1_w": norm1_w, "norm2_w": norm2_w, "w_in_r": _repack_w_in_t(w_in_t), "w_lr_t": w_lr_t,
        "wgk_pad": jnp.pad(w_gk2, ((0, 0), (0, LR_PAD - GLA_RANK), (0, 0))),
        "b_gk": b_gk, "gla_norm_w": gla_norm_w, "gm_norm_w": gm_norm_w, "gm_norm_b": gm_norm_b,
        "gm_ws": gm_ws, "gm_bs_t": jnp.swapaxes(gm_bs, 1, 2),
        "gm_w0": jnp.repeat(gm_ws[:, :, 0, 0], GM_GW, axis=1).reshape(DEPTH, 1, GM_WIDTH),
        "gm_b0": jnp.repeat(gm_bs[:, :, 0], GM_GW, axis=1).reshape(DEPTH, 1, GM_WIDTH),
        "sinks_b": jnp.broadcast_to(swa_sinks[:, :, None], (DEPTH, SWA_HQ, LANE)),
        "slopes_b": jnp.broadcast_to(
            jnp.asarray([_alibi_slope(h) for h in range(SWA_HQ)], F32)[:, None], (SWA_HQ, LANE)),
        "w_pa": w_pa, "w_pb": w_pb, "w_pc": w_pc, "w_o": w_o, "w_ffn_in": w_ffn_in, "w_ffn_out": w_ffn_out,
    }
    c_all = jnp.concatenate([c_sample, c_prompt, jnp.zeros((MOD_ROWS - DEC_BATCH - BATCH, D), F32)], axis=0)
    mod = _ada(c_all, w_ada, b_ada)

    xp = x_prompt.reshape(BATCH * SEQ, D)
    xs = x_sample.reshape(DEC_BATCH, D)
    cache_k = jnp.transpose(cache_swa_k, (0, 1, 3, 4, 2))
    cache_v = jnp.transpose(cache_swa_v, (0, 1, 3, 4, 2))
    gla_p, kp, vp, ksm, vsm, gmv = [], [], [], [], [], []
    state_out = None
    for l in range(DEPTH):
        xp, xs, s_p, state_out, k_p, v_p, k_s, v_s, gv = _layer(l, xp, xs, mod, p, state_gla, cache_k, cache_v,
                                                                state_out)
        gla_p.append(s_p)
        kp.append(k_p)
        vp.append(v_p)
        ksm.append(k_s)
        vsm.append(v_s)
        gmv.append(gv.reshape(DEC_BATCH, 1, GM_WIDTH))
    y_prompt = _final_norm(xp, final_norm_w).reshape(BATCH, SEQ, D)
    y_sample = _final_norm(xs, final_norm_w).reshape(DEC_BATCH, 1, D)
    return (y_prompt, y_sample, jnp.stack(gla_p), state_out, jnp.stack(kp), jnp.stack(vp),
            jnp.stack(ksm), jnp.stack(vsm), jnp.stack(gmv))
```

```python
import functools

import jax
import jax.numpy as jnp
import numpy as np
from jax import lax
from jax.experimental import pallas as pl
from jax.experimental.pallas import tpu as pltpu

F32 = jnp.float32
BF16 = jnp.bfloat16

D = 2048
BATCH, SEQ = 2, 4096
DEPTH = 2
DEC_BATCH = 128
GLA_H, GLA_DK, GLA_DV = 4, 256, 512
GLA_KEY, GLA_VAL = GLA_H * GLA_DK, GLA_H * GLA_DV
GLA_RANK = 16
GLA_CHUNK = 16
GLA_SC = 128
GM_WIDTH, GM_GROUPS, GM_CHUNK = 1024, 4, 128
GM_GW = GM_WIDTH // GM_GROUPS
SWA_HQ, SWA_HKV, SWA_HD, WINDOW = 16, 4, 64, 128
SWA_G = SWA_HQ // SWA_HKV
SWA_Q, SWA_KV = SWA_HQ * SWA_HD, SWA_HKV * SWA_HD
FFN_HIDDEN = 5632
EPS = 1e-6
NEG_BIG = -1e30

Z_QA, Z_KA, Z_VA, Z_GA = 0, 1024, 2048, 4096
Z_UB, Z_VB = 6144, 7168
Z_QC, Z_KC, Z_VC = 8192, 9216, 9472
Z_GATES = 9728
Z_WIDTH = 15872
LR_COL = 6144
LANE = 128
SUBLANE = 8
LR_PAD = LANE

MOD_SH1, MOD_SC1, MOD_G1, MOD_SH2, MOD_SC2, MOD_G2 = range(6)
MOD_ROWS = DEC_BATCH + 8

VMEM_LIMIT = 56 * 1024 * 1024


def _cparams(n_axes):
    return pltpu.CompilerParams(dimension_semantics=("arbitrary",) * n_axes,
                                vmem_limit_bytes=VMEM_LIMIT)


def _bf(x):
    return x if x.dtype == BF16 else x.astype(BF16)


def _dot(a, b):
    return jnp.dot(_bf(a), _bf(b), preferred_element_type=F32)


def _dot_nt(a, b):
    return lax.dot_general(_bf(a), _bf(b), (((1,), (1,)), ((), ())), preferred_element_type=F32)


def _silu(x):
    return x * (1.0 / (1.0 + jnp.exp(-x)))


def _sigmoid(x):
    return 1.0 / (1.0 + jnp.exp(-x))


def _gelu(x):
    return 0.5 * x * (1.0 + jnp.tanh(np.sqrt(2.0 / np.pi).astype(np.float32) * (x + 0.044715 * (x * x * x))))


def _mod_spec_prompt(layer, chunk, tm, tn):
    cb, bpb = chunk * D // tn, SEQ // tm
    return pl.BlockSpec((None, None, 1, tn), lambda i, j: (layer, DEC_BATCH + i // bpb, 0, j + cb))


def _mod_spec_sample(layer, chunk, tn, jmap):
    cb = chunk * D // tn
    return pl.BlockSpec((None, DEC_BATCH, tn), lambda i, j: (layer, 0, jmap(i, j) + cb))


def _ada_kernel(c_ref, w_ref, b_ref, o_ref):
    o_ref[...] = _dot(_silu(c_ref[...]), w_ref[...]) + b_ref[...]


def _ada(c_all, w_ada, b_ada):
    tn = 1024
    return pl.pallas_call(
        _ada_kernel,
        grid=(DEPTH, 6 * D // tn),
        in_specs=[pl.BlockSpec((MOD_ROWS, D), lambda l, j: (0, 0)),
                  pl.BlockSpec((None, D, tn), lambda l, j: (l, 0, j)),
                  pl.BlockSpec((None, 1, tn), lambda l, j: (l, 0, j))],
        out_specs=pl.BlockSpec((None, MOD_ROWS, tn), lambda l, j: (l, 0, j)),
        out_shape=jax.ShapeDtypeStruct((DEPTH, MOD_ROWS, 6 * D), F32),
        compiler_params=_cparams(2),
        name="ada",
    )(c_all, w_ada, b_ada.reshape(DEPTH, 1, 6 * D))


def _prep_kernel(x_ref, nw_ref, sc_ref, sh_ref, *rest):
    x = x_ref[...]
    y = x * lax.rsqrt(jnp.mean(x * x, axis=-1, keepdims=True) + EPS) * nw_ref[...]
    h = (y * (1.0 + sc_ref[...]) + sh_ref[...]).astype(BF16)
    if len(rest) == 1:
        (o_ref,) = rest
    else:
        wlr_ref, o_ref, lr_ref = rest
        lr_ref[...] = _dot_nt(h, wlr_ref[...])
    o_ref[...] = h


def _prep(layer, x, norm_w, mod, sc_chunk, sh_chunk, sample, w_lr_t=None):
    rows = x.shape[0]
    if sample:
        tm, modop = rows, mod
        mod_specs = [_mod_spec_sample(layer, c, D, lambda i, j: j) for c in (sc_chunk, sh_chunk)]
    else:
        tm, modop = 512, mod.reshape(DEPTH, MOD_ROWS, 1, 6 * D)
        mod_specs = [_mod_spec_prompt(layer, c, tm, D) for c in (sc_chunk, sh_chunk)]
    args = [x, norm_w.reshape(DEPTH, 1, D), modop, modop]
    in_specs = [pl.BlockSpec((tm, D), lambda i, j: (i, 0)),
                pl.BlockSpec((None, 1, D), lambda i, j: (layer, 0, 0))] + mod_specs
    out_specs = [pl.BlockSpec((tm, D), lambda i, j: (i, 0))]
    out_shape = [jax.ShapeDtypeStruct((rows, D), BF16)]
    if w_lr_t is not None:
        args.append(w_lr_t)
        in_specs.append(pl.BlockSpec((None, LR_PAD, D), lambda i, j: (layer, 0, 0)))
        out_specs.append(pl.BlockSpec((tm, LR_PAD), lambda i, j: (i, 0)))
        out_shape.append(jax.ShapeDtypeStruct((rows, LR_PAD), F32))
    res = pl.pallas_call(
        _prep_kernel,
        grid=(rows // tm, 1),
        in_specs=in_specs,
        out_specs=out_specs,
        out_shape=out_shape,
        compiler_params=_cparams(2),
        name="prep",
    )(*args)
    return res if w_lr_t is not None else res[0]


def _final_norm_kernel(x_ref, nw_ref, o_ref):
    x = x_ref[...]
    o_ref[...] = x * lax.rsqrt(jnp.mean(x * x, axis=-1, keepdims=True) + EPS) * nw_ref[...]


def _final_norm(x, w):
    rows = x.shape[0]
    tm = min(rows, 512)
    return pl.pallas_call(
        _final_norm_kernel,
        grid=(rows // tm,),
        in_specs=[pl.BlockSpec((tm, D), lambda i: (i, 0)), pl.BlockSpec((1, D), lambda i: (0, 0))],
        out_specs=pl.BlockSpec((tm, D), lambda i: (i, 0)),
        out_shape=jax.ShapeDtypeStruct((rows, D), F32),
        compiler_params=_cparams(1),
        name="final_norm",
    )(x, w.reshape(1, D))


def _mm_kernel(*refs, n_a, term_a, term_t, n_extra, n_out, epilogue):
    sizes = (n_a, n_a, len(term_a), n_extra, n_extra, n_out, n_out)
    groups, pos = [], 0
    for n in sizes:
        groups.append(refs[pos:pos + n])
        pos += n
    a_p, a_s, w_refs, e_p, e_s, o_p, o_s = groups
    w_vals = [_bf(w[...]) for w in w_refs]

    def run(a_refs, e_refs, o_refs):
        a_vals = [_bf(a[...]) for a in a_refs]
        dots = [(_dot_nt if t else _dot)(a_vals[ai], w) for ai, t, w in zip(term_a, term_t, w_vals)]
        outs = epilogue(dots, [e[...] for e in e_refs])
        for o_ref, o in zip(o_refs, outs):
            o_ref[...] = o.astype(o_ref.dtype)

    run(a_p, e_p, o_p)

    @pl.when(pl.program_id(0) == 0)
    def _():
        run(a_s, e_s, o_s)


def _fused_matmul(name, layer, a_p, a_s, terms, extras, epilogue, out_dtypes, n_cols, tm, tn, lhs_buffers=2):
    rows_p, rows_s = a_p[0].shape[0], a_s[0].shape[0]
    nj = n_cols // tn
    grid = (rows_p // tm, nj)
    sj = lambda i, j: jnp.where(i == 0, j, nj - 1)
    args, in_specs = [], []
    for a in a_p:
        args.append(a)
        in_specs.append(pl.BlockSpec((tm, a.shape[1]), lambda i, j: (i, 0),
                                     pipeline_mode=pl.Buffered(lhs_buffers)))
    for a in a_s:
        args.append(a)
        in_specs.append(pl.BlockSpec((rows_s, a.shape[1]), lambda i, j: (0, 0)))
    for ai, w, col0, transposed in terms:
        assert col0 % tn == 0 and w.shape[-1 if transposed else -2] == a_p[ai].shape[1]
        args.append(w)
        if transposed:
            in_specs.append(pl.BlockSpec((None, tn, w.shape[-1]), lambda i, j, cb=col0 // tn: (layer, j + cb, 0)))
        else:
            in_specs.append(pl.BlockSpec((None, w.shape[-2], tn), lambda i, j, cb=col0 // tn: (layer, 0, j + cb)))
    s_args, s_specs = [], []
    for ex in extras:
        if ex[0] == "tile":
            _, arr_p, arr_s, col0 = ex
            assert col0 % tn == 0
            args.append(arr_p)
            in_specs.append(pl.BlockSpec((tm, tn), lambda i, j, cb=col0 // tn: (i, j + cb)))
            s_args.append(arr_s)
            s_specs.append(pl.BlockSpec((rows_s, tn), lambda i, j, cb=col0 // tn: (0, sj(i, j) + cb)))
        else:
            _, mod, chunk = ex
            args.append(mod.reshape(DEPTH, MOD_ROWS, 1, 6 * D))
            in_specs.append(_mod_spec_prompt(layer, chunk, tm, tn))
            s_args.append(mod)
            s_specs.append(_mod_spec_sample(layer, chunk, tn, sj))
    kern = functools.partial(_mm_kernel, n_a=len(a_p), term_a=tuple(t[0] for t in terms),
                             term_t=tuple(t[3] for t in terms), n_extra=len(extras), n_out=len(out_dtypes),
                             epilogue=epilogue)
    res = pl.pallas_call(
        kern,
        grid=grid,
        in_specs=in_specs + s_specs,
        out_specs=([pl.BlockSpec((tm, tn), lambda i, j: (i, j)) for _ in out_dtypes]
                   + [pl.BlockSpec((rows_s, tn), lambda i, j: (0, sj(i, j))) for _ in out_dtypes]),
        out_shape=([jax.ShapeDtypeStruct((rows_p, n_cols), dt) for dt in out_dtypes]
                   + [jax.ShapeDtypeStruct((rows_s, n_cols), dt) for dt in out_dtypes]),
        compiler_params=_cparams(2),
        name=name,
    )(*args, *s_args)
    return res[:len(out_dtypes)], res[len(out_dtypes):]


def _norm_mod(x, nw, sc, sh):
    y = x * lax.rsqrt(jnp.mean(x * x, axis=-1, keepdims=True) + EPS) * nw
    return (y * (1.0 + sc) + sh).astype(BF16)


def _wo_prep_kernel(mp_ref, ms_ref, w_ref, xp_ref, xs_ref, nw_ref, g1p_ref, scp_ref, shp_ref,
                    g1s_ref, scs_ref, shs_ref, x1p_ref, h2p_ref, x1s_ref, h2s_ref):
    w = w_ref[...]
    nw = nw_ref[...]
    x1 = xp_ref[...] + g1p_ref[...] * _dot(mp_ref[...], w)
    x1p_ref[...] = x1
    h2p_ref[...] = _norm_mod(x1, nw, scp_ref[...], shp_ref[...])

    @pl.when(pl.program_id(0) == 0)
    def _():
        x1s = xs_ref[...] + g1s_ref[...] * _dot(ms_ref[...], w)
        x1s_ref[...] = x1s
        h2s_ref[...] = _norm_mod(x1s, nw, scs_ref[...], shs_ref[...])


def _wo_prep(layer, mp, ms, w_o_bf, xp, xs, norm_w, mod):
    tm = 512
    rows_p, rows_s = mp.shape[0], ms.shape[0]
    mod4 = mod.reshape(DEPTH, MOD_ROWS, 1, 6 * D)
    whole = lambda i, j=0: (0, 0)
    pm = lambda c: pl.BlockSpec((None, None, 1, D), lambda i: (layer, DEC_BATCH + i // (SEQ // tm), 0, c))
    sm = lambda c: pl.BlockSpec((None, rows_s, D), lambda i: (layer, 0, c))
    return pl.pallas_call(
        _wo_prep_kernel,
        grid=(rows_p // tm,),
        in_specs=[pl.BlockSpec((tm, D), lambda i: (i, 0)),
                  pl.BlockSpec((rows_s, D), whole),
                  pl.BlockSpec((None, D, D), lambda i: (layer, 0, 0), pipeline_mode=pl.Buffered(1)),
                  pl.BlockSpec((tm, D), lambda i: (i, 0)),
                  pl.BlockSpec((rows_s, D), whole),
                  pl.BlockSpec((None, 1, D), lambda i: (layer, 0, 0)),
                  pm(MOD_G1), pm(MOD_SC2), pm(MOD_SH2), sm(MOD_G1), sm(MOD_SC2), sm(MOD_SH2)],
        out_specs=[pl.BlockSpec((tm, D), lambda i: (i, 0)),
                   pl.BlockSpec((tm, D), lambda i: (i, 0)),
                   pl.BlockSpec((rows_s, D), whole),
                   pl.BlockSpec((rows_s, D), whole)],
        out_shape=[jax.ShapeDtypeStruct((rows_p, D), F32), jax.ShapeDtypeStruct((rows_p, D), BF16),
                   jax.ShapeDtypeStruct((rows_s, D), F32), jax.ShapeDtypeStruct((rows_s, D), BF16)],
        compiler_params=_cparams(1),
        name="w_o_prep",
    )(mp, ms, w_o_bf, xp, xs, norm_w.reshape(DEPTH, 1, D), mod4, mod4, mod4, mod, mod, mod)


def _epi_plain(dots, ex):
    return [dots[0]]


def _epi_merge(dots, ex):
    return [_sigmoid(ex[0]) * dots[0] + _sigmoid(ex[1]) * dots[1] + _sigmoid(ex[2]) * dots[2]]


def _epi_residual(dots, ex):
    return [ex[0] + ex[1] * dots[0]]


def _epi_swiglu(dots, ex):
    return [_silu(dots[0]) * dots[1]]


def _log_sigmoid(u):
    return -(jnp.maximum(-u, 0.0) + jnp.log1p(jnp.exp(-jnp.abs(u))))


def _gla_prompt_kernel(q_ref, k_ref, v_ref, ga_ref, lr_ref, wgk_ref, bgk_ref, nw_ref,
                       oa_ref, sfin_ref, st_s, vt_s, o_s, *, tb):
    t = pl.program_id(1)
    nsc = tb // GLA_SC
    nd = GLA_SC // GLA_CHUNK - 1

    @pl.when(t == 0)
    def _():
        st_s[...] = jnp.zeros_like(st_s)

    u = _dot(lr_ref[...], wgk_ref[...]) + bgk_ref[...]
    gk = _log_sigmoid(u) * (1.0 / 16.0)
    sub = lax.broadcasted_iota(jnp.int32, gk.shape, 0) % SUBLANE
    p8 = gk
    for s in (1, 2, 4):
        p8 = p8 + jnp.where(sub >= s, pltpu.roll(p8, s, 0), 0.0)
    nchunk = tb // GLA_CHUNK
    b_parts, bs_parts, tot = [], [], []
    acc = None
    for c in range(nchunk):
        r0 = c * GLA_CHUNK
        lo = p8[r0:r0 + SUBLANE, :]
        hi = p8[r0 + SUBLANE:r0 + GLA_CHUNK, :] + lo[SUBLANE - 1:SUBLANE, :]
        if c % (GLA_SC // GLA_CHUNK) == 0:
            b_parts += [lo, hi]
            bs_parts += [lo, hi]
            acc = hi[SUBLANE - 1:SUBLANE, :]
        else:
            b_parts += [lo, hi]
            bs_parts += [lo + acc, hi + acc]
            acc = acc + hi[SUBLANE - 1:SUBLANE, :]
        tot.append(hi[SUBLANE - 1:SUBLANE, :])
    b = jnp.concatenate(b_parts, axis=0)
    bs = jnp.concatenate(bs_parts, axis=0)

    def per_chunk(vals, shift):
        return jnp.concatenate([jnp.broadcast_to(vals[(c + shift) % nchunk], (GLA_CHUNK, GLA_KEY))
                                for c in range(nchunk)], axis=0)

    blb = per_chunk(tot, 0)
    etot = [jnp.exp(t) for t in tot]

    q = q_ref[...] * (GLA_DK ** -0.5)
    k = k_ref[...]
    qin = _bf(q * jnp.exp(b))
    kout = _bf(k * jnp.exp(-b))
    kd = k * jnp.exp(blb - b)
    qsc = _bf(q * jnp.exp(bs))
    vt_s[...] = v_ref[...].T

    ri = lax.broadcasted_iota(jnp.int32, (tb, tb), 0)
    ci = lax.broadcasted_iota(jnp.int32, (tb, tb), 1)
    delta = jnp.where(ri // GLA_SC == ci // GLA_SC, ri // GLA_CHUNK - ci // GLA_CHUNK, -1)
    m_intra = (delta == 0) & (ci <= ri)
    kds = [_bf(kd)]
    for d in range(1, nd):
        kd = kd * per_chunk(etot, d)
        kds.append(_bf(kd))
    for h in range(GLA_H):
        ks = slice(h * GLA_DK, (h + 1) * GLA_DK)
        vs = slice(h * GLA_DV, (h + 1) * GLA_DV)
        a = jnp.where(m_intra, _dot_nt(qin[:, ks], kout[:, ks]), 0.0)
        for d in range(nd):
            a = jnp.where(delta == d + 1, _dot_nt(qin[:, ks], kds[d][:, ks]), a)
        o_s[:, vs] = _dot(a, v_ref[:, vs])

    for sc in range(nsc):
        rows = slice(sc * GLA_SC, (sc + 1) * GLA_SC)
        last = bs[(sc + 1) * GLA_SC - 1:(sc + 1) * GLA_SC, :]
        k2 = _bf(k[rows, :] * jnp.exp(last - bs[rows, :]))
        elast = jnp.exp(last)
        for h in range(GLA_H):
            ks = slice(h * GLA_DK, (h + 1) * GLA_DK)
            vs = slice(h * GLA_DV, (h + 1) * GLA_DV)
            st = st_s[h]
            o_s[rows, vs] += _dot_nt(qsc[rows, ks], st)
            st_s[h] = st * elast[:, ks] + _dot(vt_s[vs, rows], k2[:, ks])

    nw = nw_ref[...]
    for h in range(GLA_H):
        vs = slice(h * GLA_DV, (h + 1) * GLA_DV)
        o = o_s[:, vs]
        y = o * lax.rsqrt(jnp.mean(o * o, axis=-1, keepdims=True) + EPS) * nw
        oa_ref[:, vs] = (y * _silu(ga_ref[:, vs])).astype(oa_ref.dtype)

    @pl.when(t == pl.num_programs(1) - 1)
    def _():
        for h in range(GLA_H):
            sfin_ref[h] = st_s[h].T


def _gla_prompt(layer, z, lr, wgk_pad, b_gk, gla_norm_w):
    tb = 256
    nt = SEQ // tb
    row = lambda b, t: b * nt + t
    kern = functools.partial(_gla_prompt_kernel, tb=tb)
    return pl.pallas_call(
        kern,
        grid=(BATCH, nt),
        in_specs=[pl.BlockSpec((tb, GLA_KEY), lambda b, t: (row(b, t), Z_QA // GLA_KEY)),
                  pl.BlockSpec((tb, GLA_KEY), lambda b, t: (row(b, t), Z_KA // GLA_KEY)),
                  pl.BlockSpec((tb, GLA_VAL), lambda b, t: (row(b, t), Z_VA // GLA_VAL)),
                  pl.BlockSpec((tb, GLA_VAL), lambda b, t: (row(b, t), Z_GA // GLA_VAL)),
                  pl.BlockSpec((tb, LR_PAD), lambda b, t: (row(b, t), 0)),
                  pl.BlockSpec((None, LR_PAD, GLA_KEY), lambda b, t: (layer, 0, 0)),
                  pl.BlockSpec((None, 1, GLA_KEY), lambda b, t: (layer, 0, 0)),
                  pl.BlockSpec((None, 1, GLA_DV), lambda b, t: (layer, 0, 0))],
        out_specs=[pl.BlockSpec((tb, GLA_VAL), lambda b, t: (row(b, t), 0)),
                   pl.BlockSpec((None, GLA_H, GLA_DK, GLA_DV), lambda b, t: (b, 0, 0, 0))],
        out_shape=[jax.ShapeDtypeStruct((BATCH * SEQ, GLA_VAL), BF16),
                   jax.ShapeDtypeStruct((BATCH, GLA_H, GLA_DK, GLA_DV), F32)],
        scratch_shapes=[pltpu.VMEM((GLA_H, GLA_DV, GLA_DK), F32),
                        pltpu.VMEM((GLA_VAL, tb), F32),
                        pltpu.VMEM((tb, GLA_VAL), F32)],
        compiler_params=_cparams(2),
        name="gla_prompt",
    )(z, z, z, z, lr, wgk_pad, b_gk.reshape(DEPTH, 1, GLA_KEY), gla_norm_w.reshape(DEPTH, 1, GLA_DV))


def _layernorm(x, w, b):
    mu = jnp.mean(x, axis=-1, keepdims=True)
    xc = x - mu
    var = jnp.mean(xc * xc, axis=-1, keepdims=True)
    return xc * lax.rsqrt(var + EPS) * w + b


def _gmlp_prompt_kernel(u_ref, v_ref, ws_ref, bst_ref, nw_ref, nb_ref, ob_ref, *, nsub):
    ri = lax.broadcasted_iota(jnp.int32, (GM_CHUNK, GM_CHUNK), 0)
    ci = lax.broadcasted_iota(jnp.int32, (GM_CHUNK, GM_CHUNK), 1)
    tril = ci <= ri
    for s in range(nsub):
        rs = slice(s * GM_CHUNK, (s + 1) * GM_CHUNK)
        u = _gelu(u_ref[rs, :])
        v = _layernorm(_gelu(v_ref[rs, :]), nw_ref[...], nb_ref[...])
        for g in range(GM_GROUPS):
            cs = slice(g * GM_GW, (g + 1) * GM_GW)
            wm = jnp.where(tril, ws_ref[g], 0.0)
            mixed = _dot(wm, v[:, cs]) + bst_ref[:, g:g + 1]
            ob_ref[rs, cs] = (u[:, cs] * mixed).astype(ob_ref.dtype)


def _gmlp_prompt(layer, z, gm_ws, gm_bs_t, gm_norm_w, gm_norm_b):
    nsub = 4
    tb = nsub * GM_CHUNK
    kern = functools.partial(_gmlp_prompt_kernel, nsub=nsub)
    return pl.pallas_call(
        kern,
        grid=(BATCH * SEQ // tb,),
        in_specs=[pl.BlockSpec((tb, GM_WIDTH), lambda i: (i, Z_UB // GM_WIDTH)),
                  pl.BlockSpec((tb, GM_WIDTH), lambda i: (i, Z_VB // GM_WIDTH)),
                  pl.BlockSpec((None, GM_GROUPS, GM_CHUNK, GM_CHUNK), lambda i: (layer, 0, 0, 0)),
                  pl.BlockSpec((None, GM_CHUNK, GM_GROUPS), lambda i: (layer, 0, 0)),
                  pl.BlockSpec((None, 1, GM_WIDTH), lambda i: (layer, 0, 0)),
                  pl.BlockSpec((None, 1, GM_WIDTH), lambda i: (layer, 0, 0))],
        out_specs=pl.BlockSpec((tb, GM_WIDTH), lambda i: (i, 0)),
        out_shape=jax.ShapeDtypeStruct((BATCH * SEQ, GM_WIDTH), BF16),
        compiler_params=_cparams(1),
        name="gmlp_prompt",
    )(z, z, gm_ws, gm_bs_t, gm_norm_w.reshape(DEPTH, 1, GM_WIDTH), gm_norm_b.reshape(DEPTH, 1, GM_WIDTH))


def _gmlp_sample_kernel(u_ref, v_ref, w0_ref, b0_ref, nw_ref, nb_ref, ob_ref, vn_ref):
    u = _gelu(u_ref[...])
    v = _layernorm(_gelu(v_ref[...]), nw_ref[...], nb_ref[...])
    vn_ref[...] = v
    ob_ref[...] = u * (w0_ref[...] * v + b0_ref[...])


def _gmlp_sample(layer, z, w0_row, b0_row, gm_norm_w, gm_norm_b):
    full = lambda i: (0, 0)
    lrow = lambda i: (layer, 0, 0)
    return pl.pallas_call(
        _gmlp_sample_kernel,
        grid=(1,),
        in_specs=[pl.BlockSpec((DEC_BATCH, GM_WIDTH), lambda i: (0, Z_UB // GM_WIDTH)),
                  pl.BlockSpec((DEC_BATCH, GM_WIDTH), lambda i: (0, Z_VB // GM_WIDTH)),
                  pl.BlockSpec((None, 1, GM_WIDTH), lrow),
                  pl.BlockSpec((None, 1, GM_WIDTH), lrow),
                  pl.BlockSpec((None, 1, GM_WIDTH), lrow),
                  pl.BlockSpec((None, 1, GM_WIDTH), lrow)],
        out_specs=[pl.BlockSpec((DEC_BATCH, GM_WIDTH), full), pl.BlockSpec((DEC_BATCH, GM_WIDTH), full)],
        out_shape=[jax.ShapeDtypeStruct((DEC_BATCH, GM_WIDTH), F32),
                   jax.ShapeDtypeStruct((DEC_BATCH, GM_WIDTH), F32)],
        compiler_params=_cparams(1),
        name="gmlp_sample",
    )(z, z, w0_row, b0_row, gm_norm_w.reshape(DEPTH, 1, GM_WIDTH), gm_norm_b.reshape(DEPTH, 1, GM_WIDTH))


def _alibi_slope(h):
    return float(2.0 ** (-8.0 * (h + 1) / SWA_HQ))


def _swa_lane_halves(x, half):
    lane = lax.broadcasted_iota(jnp.int32, x.shape, 1)
    own = jnp.where((lane >= half * SWA_HD) & (lane < (half + 1) * SWA_HD), x, 0.0)
    other = pltpu.roll(own, SWA_HD, 1)
    return (own, other) if half == 0 else (other, own)


def _swa_prompt_block(n, q_ref, kc_ref, kp_ref, vc_ref, vp_ref, sink_ref, oc_ref, s_s, p_s):
    w = WINDOW
    ri = lax.broadcasted_iota(jnp.int32, (w, 2 * w), 0)
    ci = lax.broadcasted_iota(jnp.int32, (w, 2 * w), 1)
    dist_i = w + ri - ci
    valid = (dist_i >= 0) & (dist_i < w) & ((ci >= w) | (n > 0))
    dist = dist_i.astype(F32)
    kcat = jnp.concatenate([kp_ref[...], kc_ref[...]], axis=0)
    vcat = jnp.concatenate([vp_ref[...], vc_ref[...]], axis=0)
    heads = []
    for kv in range(SWA_HKV):
        t, half = kv // 2, kv % 2
        k_lo, k_hi = _swa_lane_halves(kcat[:, t * LANE:(t + 1) * LANE], half)
        q2 = jnp.concatenate([q_ref[:, 2 * kv * LANE:(2 * kv + 1) * LANE],
                              q_ref[:, (2 * kv + 1) * LANE:(2 * kv + 2) * LANE]], axis=0)
        for par, kk in ((0, k_lo), (1, k_hi)):
            s = _dot_nt(q2, kk) * (SWA_HD ** -0.5)
            for e in range(2):
                h = SWA_G * kv + 2 * e + par
                seg = len(heads)
                heads.append(h)
                s_s[seg * w:(seg + 1) * w, :] = jnp.where(
                    valid, s[e * w:(e + 1) * w, :] - _alibi_slope(h) * dist, NEG_BIG)
    s = s_s[...]
    sink = jnp.concatenate([jnp.broadcast_to(sink_ref[h:h + 1, 0:1], (w, 1)) for h in heads], axis=0)
    m = jnp.maximum(jnp.max(s, axis=-1, keepdims=True), sink)
    p = jnp.exp(s - m)
    inv = 1.0 / (jnp.sum(p, axis=-1, keepdims=True) + jnp.exp(sink - m))
    p_s[...] = (p * inv).astype(p_s.dtype)
    for kv in range(SWA_HKV):
        t, half = kv // 2, kv % 2
        v_lo, v_hi = _swa_lane_halves(vcat[:, t * LANE:(t + 1) * LANE], half)
        r0 = SWA_G * kv * w
        o = _dot(p_s[r0:r0 + 2 * w, :], v_lo) + _dot(p_s[r0 + 2 * w:r0 + 4 * w, :], v_hi)
        oc_ref[:, 2 * kv * LANE:(2 * kv + 1) * LANE] = o[:w].astype(oc_ref.dtype)
        oc_ref[:, (2 * kv + 1) * LANE:(2 * kv + 2) * LANE] = o[w:].astype(oc_ref.dtype)


STATE_SLOTS = 3


def _state_copy(hbm, sbuf, sem, layer, step, rps, to_hbm):
    slot = lax.rem(step, STATE_SLOTS)
    rows = hbm.at[layer, pl.ds(step * rps, rps)]
    if to_hbm:
        return pltpu.make_async_copy(sbuf.at[slot], rows, sem.at[slot])
    return pltpu.make_async_copy(rows, sbuf.at[slot], sem.at[slot])


def _state_ring_begin(g, n_steps, copy_in, copy_out, x_s, oa_s):
    @pl.when(g == 0)
    def _():
        x_s[...] = jnp.zeros_like(x_s)
        oa_s[...] = jnp.zeros_like(oa_s)
        copy_in(0).start()

    @pl.when(g >= STATE_SLOTS - 1)
    def _():
        copy_out(g - (STATE_SLOTS - 1)).wait()

    @pl.when(g + 1 < n_steps)
    def _():
        copy_in(g + 1).start()

    copy_in(g).wait()


def _state_ring_end(g, n_steps, copy_out):
    copy_out(g).start()

    @pl.when(g == n_steps - 1)
    def _():
        for back in range(STATE_SLOTS - 2, -1, -1):
            copy_out(g - back).wait()


def _gla_decode_rows(g, rps, zq_ref, zk_ref, zv_ref, zga_ref, lr_ref, wgk_ref, bgk_ref, nw_ref,
                     oa_ref, sbuf, x_s, oa_s):
    slot = lax.rem(g, STATE_SLOTS)
    sub = lax.rem(g * rps, SUBLANE)
    up = lax.rem(SUBLANE - sub, SUBLANE)
    u = _dot(lr_ref[...], wgk_ref[...]) + bgk_ref[...]
    eg = pltpu.roll(jnp.exp(_log_sigmoid(u) * (1.0 / 16.0)), up, 0)
    zq = pltpu.roll(zq_ref[...], up, 0) * (GLA_DK ** -0.5)
    zk = pltpu.roll(zk_ref[...], up, 0)
    zv = pltpu.roll(zv_ref[...], up, 0)
    gate = _silu(pltpu.roll(zga_ref[...], up, 0))
    nw = nw_ref[...]
    row8 = lax.broadcasted_iota(jnp.int32, (SUBLANE, GLA_DV), 0)
    for h in range(GLA_H):
        ks = slice(h * GLA_DK, (h + 1) * GLA_DK)
        vs = slice(h * GLA_DV, (h + 1) * GLA_DV)
        x_s[0:SUBLANE, :] = zq[:, ks]
        x_s[SUBLANE:2 * SUBLANE, :] = zk[:, ks]
        x_s[2 * SUBLANE:3 * SUBLANE, :] = eg[:, ks]
        xt = x_s[...].T
        y8 = jnp.zeros((SUBLANE, GLA_DV), F32)
        for r in range(rps):
            qc = xt[:, r:r + 1]
            kc = xt[:, SUBLANE + r:SUBLANE + r + 1]
            gc = xt[:, 2 * SUBLANE + r:2 * SUBLANE + r + 1]
            s_new = gc * sbuf[slot, r, h] + kc * zv[r:r + 1, vs]
            sbuf[slot, r, h] = s_new
            o = jnp.sum(qc * s_new, axis=0, keepdims=True)
            y = o * lax.rsqrt(jnp.mean(o * o, axis=-1, keepdims=True) + EPS) * nw * gate[r:r + 1, vs]
            y8 = jnp.where(row8 == r, y, y8)
        y8 = pltpu.roll(y8, sub, 0)
        acc = jnp.where(sub == 0, y8, oa_s[:, vs] + y8)
        oa_s[:, vs] = acc
        oa_ref[:, vs] = acc


def _swa_gla_kernel(q_ref, kc_ref, kp_ref, vc_ref, vp_ref, sink_ref,
                    zq_ref, zk_ref, zv_ref, zga_ref, lr_ref, wgk_ref, bgk_ref, nw_ref, sin_hbm, *rest,
                    layer, rps, aliased):
    if aliased:
        rest = rest[1:]
    oc_ref, oa_ref, sout_hbm, s_s, p_s, sbuf, x_s, oa_s, in_sem, out_sem = rest
    n = pl.program_id(1)
    g = pl.program_id(0) * pl.num_programs(1) + n
    n_steps = pl.num_programs(0) * pl.num_programs(1)
    copy_in = functools.partial(_state_copy, sin_hbm, sbuf, in_sem, layer, rps=rps, to_hbm=False)
    copy_out = functools.partial(_state_copy, sout_hbm, sbuf, out_sem, layer, rps=rps, to_hbm=True)
    _state_ring_begin(g, n_steps, copy_in, copy_out, x_s, oa_s)
    _gla_decode_rows(g, rps, zq_ref, zk_ref, zv_ref, zga_ref, lr_ref, wgk_ref, bgk_ref, nw_ref,
                     oa_ref, sbuf, x_s, oa_s)
    _swa_prompt_block(n, q_ref, kc_ref, kp_ref, vc_ref, vp_ref, sink_ref, oc_ref, s_s, p_s)
    _state_ring_end(g, n_steps, copy_out)


def _swa_prompt_gla_sample(layer, zp, sinks_b, zs, lrs, wgk_pad, b_gk, gla_norm_w, state_gla, state_out):
    nb = SEQ // WINDOW
    n_steps = BATCH * nb
    rps = DEC_BATCH // n_steps
    assert rps * n_steps == DEC_BATCH and SUBLANE % rps == 0 and n_steps >= STATE_SLOTS
    row = lambda b, n: b * nb + n
    prev = lambda b, n: b * nb + jnp.maximum(n - 1, 0)
    srow = lambda b, n: (row(b, n) * rps) // SUBLANE
    in_specs = [pl.BlockSpec((WINDOW, SWA_Q), lambda b, n: (row(b, n), Z_QC // SWA_Q)),
                pl.BlockSpec((WINDOW, SWA_KV), lambda b, n: (row(b, n), Z_KC // SWA_KV)),
                pl.BlockSpec((WINDOW, SWA_KV), lambda b, n: (prev(b, n), Z_KC // SWA_KV)),
                pl.BlockSpec((WINDOW, SWA_KV), lambda b, n: (row(b, n), Z_VC // SWA_KV)),
                pl.BlockSpec((WINDOW, SWA_KV), lambda b, n: (prev(b, n), Z_VC // SWA_KV)),
                pl.BlockSpec((None, SWA_HQ, LANE), lambda b, n: (layer, 0, 0)),
                pl.BlockSpec((SUBLANE, GLA_KEY), lambda b, n: (srow(b, n), Z_QA // GLA_KEY)),
                pl.BlockSpec((SUBLANE, GLA_KEY), lambda b, n: (srow(b, n), Z_KA // GLA_KEY)),
                pl.BlockSpec((SUBLANE, GLA_VAL), lambda b, n: (srow(b, n), Z_VA // GLA_VAL)),
                pl.BlockSpec((SUBLANE, GLA_VAL), lambda b, n: (srow(b, n), Z_GA // GLA_VAL)),
                pl.BlockSpec((SUBLANE, LR_PAD), lambda b, n: (srow(b, n), 0)),
                pl.BlockSpec((None, LR_PAD, GLA_KEY), lambda b, n: (layer, 0, 0)),
                pl.BlockSpec((None, 1, GLA_KEY), lambda b, n: (layer, 0, 0)),
                pl.BlockSpec((None, 1, GLA_DV), lambda b, n: (layer, 0, 0)),
                pl.BlockSpec(memory_space=pl.ANY)]
    args = [zp, zp, zp, zp, zp, sinks_b, zs, zs, zs, zs, lrs, wgk_pad, b_gk.reshape(DEPTH, 1, GLA_KEY),
            gla_norm_w.reshape(DEPTH, 1, GLA_DV), state_gla]
    aliases = {}
    if state_out is not None:
        in_specs.append(pl.BlockSpec(memory_space=pl.ANY))
        args.append(state_out)
        aliases = {len(args) - 1: 2}
    kern = functools.partial(_swa_gla_kernel, layer=layer, rps=rps, aliased=state_out is not None)
    return pl.pallas_call(
        kern,
        grid=(BATCH, nb),
        in_specs=in_specs,
        out_specs=[pl.BlockSpec((WINDOW, SWA_Q), lambda b, n: (row(b, n), 0)),
                   pl.BlockSpec((SUBLANE, GLA_VAL), lambda b, n: (srow(b, n), 0)),
                   pl.BlockSpec(memory_space=pl.ANY)],
        out_shape=[jax.ShapeDtypeStruct((BATCH * SEQ, SWA_Q), BF16),
                   jax.ShapeDtypeStruct((DEC_BATCH, GLA_VAL), F32),
                   jax.ShapeDtypeStruct((DEPTH, DEC_BATCH, GLA_H, GLA_DK, GLA_DV), F32)],
        scratch_shapes=[pltpu.VMEM((SWA_HQ * WINDOW, 2 * WINDOW), F32),
                        pltpu.VMEM((SWA_HQ * WINDOW, 2 * WINDOW), BF16),
                        pltpu.VMEM((STATE_SLOTS, rps, GLA_H, GLA_DK, GLA_DV), F32),
                        pltpu.VMEM((LANE, GLA_DK), F32),
                        pltpu.VMEM((SUBLANE, GLA_VAL), F32),
                        pltpu.SemaphoreType.DMA((STATE_SLOTS,)),
                        pltpu.SemaphoreType.DMA((STATE_SLOTS,))],
        input_output_aliases=aliases,
        compiler_params=_cparams(2),
        name="swa_prompt_gla_sample",
    )(*args)


def _swa_sample_kernel(q_ref, kn_ref, vn_ref, kt_ref, vt_ref, sink_ref, slope_ref, o_ref, *, rb):
    wb = WINDOW
    j = lax.broadcasted_iota(jnp.int32, (SWA_HQ, wb), 1)
    dist = (wb - j).astype(F32)
    ok = j >= 1
    grp = lax.broadcasted_iota(jnp.int32, (SWA_HQ, 1), 0) // SWA_G
    slope = slope_ref[:, 0:1]
    sink = sink_ref[:, 0:1]

    def per_head(rows):
        out = jnp.broadcast_to(rows[0:1, :], (SWA_HQ, SWA_HD))
        for kv in range(1, SWA_HKV):
            out = jnp.where(grp == kv, rows[kv:kv + 1, :], out)
        return out

    s_rows, self_rows = [], []
    for r in range(rb):
        q = q_ref[r]
        s = _dot(q, kt_ref[r, 0])
        for kv in range(1, SWA_HKV):
            s = jnp.where(grp == kv, _dot(q, kt_ref[r, kv]), s)
        s_rows.append(jnp.where(ok, s * (SWA_HD ** -0.5) - slope * dist, NEG_BIG))
        self_rows.append(jnp.sum(_bf(q).astype(F32) * _bf(per_head(kn_ref[r])).astype(F32), axis=-1,
                                 keepdims=True) * (SWA_HD ** -0.5))
    s = jnp.concatenate(s_rows, axis=0)
    s_self = jnp.concatenate(self_rows, axis=0)
    sink = jnp.concatenate([sink] * rb, axis=0)
    m = jnp.maximum(jnp.maximum(jnp.max(s, axis=-1, keepdims=True), s_self), sink)
    p = jnp.exp(s - m)
    p_self = jnp.exp(s_self - m)
    inv = 1.0 / (jnp.sum(p, axis=-1, keepdims=True) + p_self + jnp.exp(sink - m))
    pn = _bf(p * inv)
    pn_self = _bf(p_self * inv).astype(F32)
    for r in range(rb):
        rows = slice(r * SWA_HQ, (r + 1) * SWA_HQ)
        o = _dot_nt(pn[rows, :], vt_ref[r, 0])
        for kv in range(1, SWA_HKV):
            o = jnp.where(grp == kv, _dot_nt(pn[rows, :], vt_ref[r, kv]), o)
        o_ref[r] = o + pn_self[rows, :] * _bf(per_head(vn_ref[r])).astype(F32)


def _swa_sample(layer, q3, kn3, vn3, cache_kt, cache_vt, sinks_b, slopes_b):
    rb = 8
    kern = functools.partial(_swa_sample_kernel, rb=rb)
    return pl.pallas_call(
        kern,
        grid=(DEC_BATCH // rb,),
        in_specs=[pl.BlockSpec((rb, SWA_HQ, SWA_HD), lambda i: (i, 0, 0)),
                  pl.BlockSpec((rb, SWA_HKV, SWA_HD), lambda i: (i, 0, 0)),
                  pl.BlockSpec((rb, SWA_HKV, SWA_HD), lambda i: (i, 0, 0)),
                  pl.BlockSpec((None, rb, SWA_HKV, SWA_HD, WINDOW), lambda i: (layer, i, 0, 0, 0)),
                  pl.BlockSpec((None, rb, SWA_HKV, SWA_HD, WINDOW), lambda i: (layer, i, 0, 0, 0)),
                  pl.BlockSpec((None, SWA_HQ, LANE), lambda i: (layer, 0, 0)),
                  pl.BlockSpec((SWA_HQ, LANE), lambda i: (0, 0))],
        out_specs=pl.BlockSpec((rb, SWA_HQ, SWA_HD), lambda i: (i, 0, 0)),
        out_shape=jax.ShapeDtypeStruct((DEC_BATCH, SWA_HQ, SWA_HD), F32),
        compiler_params=_cparams(1),
        name="swa_sample",
    )(q3, kn3, vn3, cache_kt, cache_vt, sinks_b, slopes_b)


def _repack_kernel(a_ref, b_ref, o_ref, *, n_plain):
    j = pl.program_id(1)

    @pl.when(j < n_plain)
    def _():
        o_ref[...] = a_ref[...].T.astype(o_ref.dtype)

    @pl.when(j >= n_plain)
    def _():
        o_ref[...] = jnp.concatenate([a_ref[GLA_RANK:, :], b_ref[...]], axis=0).T.astype(o_ref.dtype)


def _repack_w_in_t(w_in_t):
    tn = 512
    assert LR_COL % tn == 0 and Z_WIDTH % tn == 0 and tn % GLA_RANK == 0
    kern = functools.partial(_repack_kernel, n_plain=LR_COL // tn)
    return pl.pallas_call(
        kern,
        grid=(DEPTH, Z_WIDTH // tn),
        in_specs=[pl.BlockSpec((None, tn, D), lambda l, j: (l, j, 0)),
                  pl.BlockSpec((None, GLA_RANK, D), lambda l, j: (l, (j + 1) * (tn // GLA_RANK), 0))],
        out_specs=pl.BlockSpec((None, D, tn), lambda l, j: (l, 0, j)),
        out_shape=jax.ShapeDtypeStruct((DEPTH, D, Z_WIDTH), BF16),
        compiler_params=_cparams(2),
        name="repack_w_in",
    )(w_in_t, w_in_t)


def _layer(layer, xp, xs, mod, p, state_gla, cache_k, cache_v, state_out):
    hp, lrp = _prep(layer, xp, p["norm1_w"], mod, MOD_SC1, MOD_SH1, False, p["w_lr_t"])
    hs, lrs = _prep(layer, xs, p["norm1_w"], mod, MOD_SC1, MOD_SH1, True, p["w_lr_t"])
    (zp,), (zs,) = _fused_matmul("w_in", layer, [hp], [hs], [(0, p["w_in_r"], 0, False)], [], _epi_plain, [F32],
                                 Z_WIDTH, 2048, 512)
    oa_p, s_p = _gla_prompt(layer, zp, lrp, p["wgk_pad"], p["b_gk"], p["gla_norm_w"])
    ob_p = _gmlp_prompt(layer, zp, p["gm_ws"], p["gm_bs_t"], p["gm_norm_w"], p["gm_norm_b"])
    oc_p, oa_s, state_out = _swa_prompt_gla_sample(layer, zp, p["sinks_b"], zs, lrs, p["wgk_pad"], p["b_gk"],
                                                   p["gla_norm_w"], state_gla, state_out)
    z4 = zp.reshape(BATCH, SEQ, Z_WIDTH)
    kp_rows = z4[:, SEQ - WINDOW:, Z_KC:Z_KC + SWA_KV].reshape(BATCH, WINDOW, SWA_HKV, SWA_HD)
    vp_rows = z4[:, SEQ - WINDOW:, Z_VC:Z_VC + SWA_KV].reshape(BATCH, WINDOW, SWA_HKV, SWA_HD)
    ob_s, v_gm = _gmlp_sample(layer, zs, p["gm_w0"], p["gm_b0"], p["gm_norm_w"], p["gm_norm_b"])
    q3 = zs[:, Z_QC:Z_QC + SWA_Q].reshape(DEC_BATCH, SWA_HQ, SWA_HD)
    kn3 = zs[:, Z_KC:Z_KC + SWA_KV].reshape(DEC_BATCH, SWA_HKV, SWA_HD)
    vn3 = zs[:, Z_VC:Z_VC + SWA_KV].reshape(DEC_BATCH, SWA_HKV, SWA_HD)
    oc_s = _swa_sample(layer, q3, kn3, vn3, cache_k, cache_v, p["sinks_b"], p["slopes_b"]).reshape(DEC_BATCH, SWA_Q)
    ks_rows = kn3.reshape(DEC_BATCH, 1, SWA_HKV, SWA_HD)
    vs_rows = vn3.reshape(DEC_BATCH, 1, SWA_HKV, SWA_HD)
    (mp,), (ms,) = _fused_matmul(
        "merge", layer, [oa_p, ob_p, oc_p], [oa_s, ob_s, oc_s],
        [(0, p["w_pa"], 0, False), (1, p["w_pb"], 0, False), (2, p["w_pc"], 0, False)],
        [("tile", zp, zs, Z_GATES), ("tile", zp, zs, Z_GATES + D), ("tile", zp, zs, Z_GATES + 2 * D)],
        _epi_merge, [BF16], D, 2048, 256, lhs_buffers=1)
    x1p, h2p, x1s, h2s = _wo_prep(layer, mp, ms, p["w_o_bf"], xp, xs, p["norm2_w"], mod)
    (hidp,), (hids,) = _fused_matmul(
        "ffn_in", layer, [h2p], [h2s], [(0, p["w_ffn_in"], 0, False), (0, p["w_ffn_in"], FFN_HIDDEN, False)],
        [], _epi_swiglu, [BF16], FFN_HIDDEN, 2048, 512)
    (x2p,), (x2s,) = _fused_matmul("ffn_out", layer, [hidp], [hids], [(0, p["w_ffn_out"], 0, False)],
                                   [("tile", x1p, x1s, 0), ("mod", mod, MOD_G2)], _epi_residual, [F32],
                                   D, 1024, 512, lhs_buffers=1)
    return x2p, x2s, s_p, state_out, kp_rows, vp_rows, ks_rows, vs_rows, v_gm


def kernel(x_prompt, x_sample, c_prompt, c_sample, state_gla, cache_swa_k, cache_swa_v, w_ada, b_ada, norm1_w,
           norm2_w, w_in, w_gk2, b_gk, gla_norm_w, gm_norm_w, gm_norm_b, gm_ws, gm_bs, swa_sinks, w_pa, w_pb,
           w_pc, w_o, w_ffn_in, w_ffn_out, final_norm_w):
    w_in_t = jnp.swapaxes(w_in, 1, 2)
    w_lr_t = jnp.pad(w_in_t[:, LR_COL:LR_COL + GLA_RANK, :], ((0, 0), (0, LR_PAD - GLA_RANK), (0, 0))).astype(BF16)
    p = {
        "norm1_w": norm1_w, "norm2_w": norm2_w, "w_in_r": _repack_w_in_t(w_in_t), "w_lr_t": w_lr_t,
        "wgk_pad": jnp.pad(w_gk2, ((0, 0), (0, LR_PAD - GLA_RANK), (0, 0))),
        "b_gk": b_gk, "gla_norm_w": gla_norm_w, "gm_norm_w": gm_norm_w, "gm_norm_b": gm_norm_b,
        "gm_ws": gm_ws, "gm_bs_t": jnp.swapaxes(gm_bs, 1, 2),
        "gm_w0": jnp.repeat(gm_ws[:, :, 0, 0], GM_GW, axis=1).reshape(DEPTH, 1, GM_WIDTH),
        "gm_b0": jnp.repeat(gm_bs[:, :, 0], GM_GW, axis=1).reshape(DEPTH, 1, GM_WIDTH),
        "sinks_b": jnp.broadcast_to(swa_sinks[:, :, None], (DEPTH, SWA_HQ, LANE)),
        "slopes_b": jnp.broadcast_to(
            jnp.asarray([_alibi_slope(h) for h in range(SWA_HQ)], F32)[:, None], (SWA_HQ, LANE)),
        "w_pa": w_pa, "w_pb": w_pb, "w_pc": w_pc, "w_o_bf": w_o.astype(BF16), "w_ffn_in": w_ffn_in, "w_ffn_out": w_ffn_out,
    }
    c_all = jnp.concatenate([c_sample, c_prompt, jnp.zeros((MOD_ROWS - DEC_BATCH - BATCH, D), F32)], axis=0)
    mod = _ada(c_all, w_ada, b_ada)

    xp = x_prompt.reshape(BATCH * SEQ, D)
    xs = x_sample.reshape(DEC_BATCH, D)
    cache_k = jnp.transpose(cache_swa_k, (0, 1, 3, 4, 2))
    cache_v = jnp.transpose(cache_swa_v, (0, 1, 3, 4, 2))
    gla_p, kp, vp, ksm, vsm, gmv = [], [], [], [], [], []
    state_out = None
    for l in range(DEPTH):
        xp, xs, s_p, state_out, k_p, v_p, k_s, v_s, gv = _layer(l, xp, xs, mod, p, state_gla, cache_k, cache_v,
                                                                state_out)
        gla_p.append(s_p)
        kp.append(k_p)
        vp.append(v_p)
        ksm.append(k_s)
        vsm.append(v_s)
        gmv.append(gv.reshape(DEC_BATCH, 1, GM_WIDTH))
    y_prompt = _final_norm(xp, final_norm_w).reshape(BATCH, SEQ, D)
    y_sample = _final_norm(xs, final_norm_w).reshape(DEC_BATCH, 1, D)
    return (y_prompt, y_sample, jnp.stack(gla_p), state_out, jnp.stack(kp), jnp.stack(vp),
            jnp.stack(ksm), jnp.stack(vsm), jnp.stack(gmv))
```

```python
import functools

import jax
import jax.numpy as jnp
import numpy as np
from jax import lax
from jax.experimental import pallas as pl
from jax.experimental.pallas import tpu as pltpu

F32 = jnp.float32
BF16 = jnp.bfloat16

D = 2048
BATCH, SEQ = 2, 4096
DEPTH = 2
DEC_BATCH = 128
GLA_H, GLA_DK, GLA_DV = 4, 256, 512
GLA_KEY, GLA_VAL = GLA_H * GLA_DK, GLA_H * GLA_DV
GLA_RANK = 16
GLA_CHUNK = 16
GLA_SC = 128
GM_WIDTH, GM_GROUPS, GM_CHUNK = 1024, 4, 128
GM_GW = GM_WIDTH // GM_GROUPS
SWA_HQ, SWA_HKV, SWA_HD, WINDOW = 16, 4, 64, 128
SWA_G = SWA_HQ // SWA_HKV
SWA_Q, SWA_KV = SWA_HQ * SWA_HD, SWA_HKV * SWA_HD
FFN_HIDDEN = 5632
EPS = 1e-6
NEG_BIG = -1e30

Z_QA, Z_KA, Z_VA, Z_GA = 0, 1024, 2048, 4096
Z_UB, Z_VB = 6144, 7168
Z_QC, Z_KC, Z_VC = 8192, 9216, 9472
Z_GATES = 9728
Z_WIDTH = 15872
LR_COL = 6144
LANE = 128
SUBLANE = 8
LR_PAD = LANE

MOD_SH1, MOD_SC1, MOD_G1, MOD_SH2, MOD_SC2, MOD_G2 = range(6)
MOD_ROWS = DEC_BATCH + 8

VMEM_LIMIT = 56 * 1024 * 1024


def _cparams(n_axes):
    return pltpu.CompilerParams(dimension_semantics=("arbitrary",) * n_axes,
                                vmem_limit_bytes=VMEM_LIMIT)


def _bf(x):
    return x if x.dtype == BF16 else x.astype(BF16)


def _dot(a, b):
    return jnp.dot(_bf(a), _bf(b), preferred_element_type=F32)


def _dot_nt(a, b):
    return lax.dot_general(_bf(a), _bf(b), (((1,), (1,)), ((), ())), preferred_element_type=F32)


def _silu(x):
    return x * (1.0 / (1.0 + jnp.exp(-x)))


def _sigmoid(x):
    return 1.0 / (1.0 + jnp.exp(-x))


def _gelu(x):
    return 0.5 * x * (1.0 + jnp.tanh(np.sqrt(2.0 / np.pi).astype(np.float32) * (x + 0.044715 * (x * x * x))))


def _mod_spec_prompt(layer, chunk, tm, tn):
    cb, bpb = chunk * D // tn, SEQ // tm
    return pl.BlockSpec((None, None, 1, tn), lambda i, j: (layer, DEC_BATCH + i // bpb, 0, j + cb))


def _mod_spec_sample(layer, chunk, tn, jmap):
    cb = chunk * D // tn
    return pl.BlockSpec((None, DEC_BATCH, tn), lambda i, j: (layer, 0, jmap(i, j) + cb))


def _ada_kernel(c_ref, w_ref, b_ref, o_ref):
    o_ref[...] = _dot(_silu(c_ref[...]), w_ref[...]) + b_ref[...]


def _ada(c_all, w_ada, b_ada):
    tn = 1024
    return pl.pallas_call(
        _ada_kernel,
        grid=(DEPTH, 6 * D // tn),
        in_specs=[pl.BlockSpec((MOD_ROWS, D), lambda l, j: (0, 0)),
                  pl.BlockSpec((None, D, tn), lambda l, j: (l, 0, j)),
                  pl.BlockSpec((None, 1, tn), lambda l, j: (l, 0, j))],
        out_specs=pl.BlockSpec((None, MOD_ROWS, tn), lambda l, j: (l, 0, j)),
        out_shape=jax.ShapeDtypeStruct((DEPTH, MOD_ROWS, 6 * D), F32),
        compiler_params=_cparams(2),
        name="ada",
    )(c_all, w_ada, b_ada.reshape(DEPTH, 1, 6 * D))


def _prep_kernel(x_ref, nw_ref, sc_ref, sh_ref, *rest):
    x = x_ref[...]
    y = x * lax.rsqrt(jnp.mean(x * x, axis=-1, keepdims=True) + EPS) * nw_ref[...]
    h = (y * (1.0 + sc_ref[...]) + sh_ref[...]).astype(BF16)
    if len(rest) == 1:
        (o_ref,) = rest
    else:
        wlr_ref, o_ref, lr_ref = rest
        lr_ref[...] = _dot_nt(h, wlr_ref[...])
    o_ref[...] = h


def _prep(layer, x, norm_w, mod, sc_chunk, sh_chunk, sample, w_lr_t=None):
    rows = x.shape[0]
    if sample:
        tm, modop = rows, mod
        mod_specs = [_mod_spec_sample(layer, c, D, lambda i, j: j) for c in (sc_chunk, sh_chunk)]
    else:
        tm, modop = 512, mod.reshape(DEPTH, MOD_ROWS, 1, 6 * D)
        mod_specs = [_mod_spec_prompt(layer, c, tm, D) for c in (sc_chunk, sh_chunk)]
    args = [x, norm_w.reshape(DEPTH, 1, D), modop, modop]
    in_specs = [pl.BlockSpec((tm, D), lambda i, j: (i, 0)),
                pl.BlockSpec((None, 1, D), lambda i, j: (layer, 0, 0))] + mod_specs
    out_specs = [pl.BlockSpec((tm, D), lambda i, j: (i, 0))]
    out_shape = [jax.ShapeDtypeStruct((rows, D), BF16)]
    if w_lr_t is not None:
        args.append(w_lr_t)
        in_specs.append(pl.BlockSpec((None, LR_PAD, D), lambda i, j: (layer, 0, 0)))
        out_specs.append(pl.BlockSpec((tm, LR_PAD), lambda i, j: (i, 0)))
        out_shape.append(jax.ShapeDtypeStruct((rows, LR_PAD), F32))
    res = pl.pallas_call(
        _prep_kernel,
        grid=(rows // tm, 1),
        in_specs=in_specs,
        out_specs=out_specs,
        out_shape=out_shape,
        compiler_params=_cparams(2),
        name="prep",
    )(*args)
    return res if w_lr_t is not None else res[0]


def _final_norm_kernel(x_ref, nw_ref, o_ref):
    x = x_ref[...]
    o_ref[...] = x * lax.rsqrt(jnp.mean(x * x, axis=-1, keepdims=True) + EPS) * nw_ref[...]


def _final_norm(x, w):
    rows = x.shape[0]
    tm = min(rows, 512)
    return pl.pallas_call(
        _final_norm_kernel,
        grid=(rows // tm,),
        in_specs=[pl.BlockSpec((tm, D), lambda i: (i, 0)), pl.BlockSpec((1, D), lambda i: (0, 0))],
        out_specs=pl.BlockSpec((tm, D), lambda i: (i, 0)),
        out_shape=jax.ShapeDtypeStruct((rows, D), F32),
        compiler_params=_cparams(1),
        name="final_norm",
    )(x, w.reshape(1, D))


def _mm_kernel(*refs, n_a, term_a, term_t, n_extra, n_out, epilogue):
    sizes = (n_a, n_a, len(term_a), n_extra, n_extra, n_out, n_out)
    groups, pos = [], 0
    for n in sizes:
        groups.append(refs[pos:pos + n])
        pos += n
    a_p, a_s, w_refs, e_p, e_s, o_p, o_s = groups
    w_vals = [_bf(w[...]) for w in w_refs]

    def run(a_refs, e_refs, o_refs):
        a_vals = [_bf(a[...]) for a in a_refs]
        dots = [(_dot_nt if t else _dot)(a_vals[ai], w) for ai, t, w in zip(term_a, term_t, w_vals)]
        outs = epilogue(dots, [e[...] for e in e_refs])
        for o_ref, o in zip(o_refs, outs):
            o_ref[...] = o.astype(o_ref.dtype)

    run(a_p, e_p, o_p)

    @pl.when(pl.program_id(0) == 0)
    def _():
        run(a_s, e_s, o_s)


def _fused_matmul(name, layer, a_p, a_s, terms, extras, epilogue, out_dtypes, n_cols, tm, tn, lhs_buffers=2):
    rows_p, rows_s = a_p[0].shape[0], a_s[0].shape[0]
    nj = n_cols // tn
    grid = (rows_p // tm, nj)
    sj = lambda i, j: jnp.where(i == 0, j, nj - 1)
    args, in_specs = [], []
    for a in a_p:
        args.append(a)
        in_specs.append(pl.BlockSpec((tm, a.shape[1]), lambda i, j: (i, 0),
                                     pipeline_mode=pl.Buffered(lhs_buffers)))
    for a in a_s:
        args.append(a)
        in_specs.append(pl.BlockSpec((rows_s, a.shape[1]), lambda i, j: (0, 0)))
    for ai, w, col0, transposed in terms:
        assert col0 % tn == 0 and w.shape[-1 if transposed else -2] == a_p[ai].shape[1]
        args.append(w)
        if transposed:
            in_specs.append(pl.BlockSpec((None, tn, w.shape[-1]), lambda i, j, cb=col0 // tn: (layer, j + cb, 0)))
        else:
            in_specs.append(pl.BlockSpec((None, w.shape[-2], tn), lambda i, j, cb=col0 // tn: (layer, 0, j + cb)))
    s_args, s_specs = [], []
    for ex in extras:
        if ex[0] == "tile":
            _, arr_p, arr_s, col0 = ex
            assert col0 % tn == 0
            args.append(arr_p)
            in_specs.append(pl.BlockSpec((tm, tn), lambda i, j, cb=col0 // tn: (i, j + cb)))
            s_args.append(arr_s)
            s_specs.append(pl.BlockSpec((rows_s, tn), lambda i, j, cb=col0 // tn: (0, sj(i, j) + cb)))
        else:
            _, mod, chunk = ex
            args.append(mod.reshape(DEPTH, MOD_ROWS, 1, 6 * D))
            in_specs.append(_mod_spec_prompt(layer, chunk, tm, tn))
            s_args.append(mod)
            s_specs.append(_mod_spec_sample(layer, chunk, tn, sj))
    kern = functools.partial(_mm_kernel, n_a=len(a_p), term_a=tuple(t[0] for t in terms),
                             term_t=tuple(t[3] for t in terms), n_extra=len(extras), n_out=len(out_dtypes),
                             epilogue=epilogue)
    res = pl.pallas_call(
        kern,
        grid=grid,
        in_specs=in_specs + s_specs,
        out_specs=([pl.BlockSpec((tm, tn), lambda i, j: (i, j)) for _ in out_dtypes]
                   + [pl.BlockSpec((rows_s, tn), lambda i, j: (0, sj(i, j))) for _ in out_dtypes]),
        out_shape=([jax.ShapeDtypeStruct((rows_p, n_cols), dt) for dt in out_dtypes]
                   + [jax.ShapeDtypeStruct((rows_s, n_cols), dt) for dt in out_dtypes]),
        compiler_params=_cparams(2),
        name=name,
    )(*args, *s_args)
    return res[:len(out_dtypes)], res[len(out_dtypes):]


def _norm_mod(x, nw, sc, sh):
    y = x * lax.rsqrt(jnp.mean(x * x, axis=-1, keepdims=True) + EPS) * nw
    return (y * (1.0 + sc) + sh).astype(BF16)


def _wo_prep_kernel(mp_ref, ms_ref, w_ref, xp_ref, xs_ref, nw_ref, g1p_ref, scp_ref, shp_ref,
                    g1s_ref, scs_ref, shs_ref, x1p_ref, h2p_ref, x1s_ref, h2s_ref):
    w = w_ref[...]
    nw = nw_ref[...]
    x1 = xp_ref[...] + g1p_ref[...] * _dot(mp_ref[...], w)
    x1p_ref[...] = x1
    h2p_ref[...] = _norm_mod(x1, nw, scp_ref[...], shp_ref[...])

    @pl.when(pl.program_id(0) == 0)
    def _():
        x1s = xs_ref[...] + g1s_ref[...] * _dot(ms_ref[...], w)
        x1s_ref[...] = x1s
        h2s_ref[...] = _norm_mod(x1s, nw, scs_ref[...], shs_ref[...])


def _wo_prep(layer, mp, ms, w_o_bf, xp, xs, norm_w, mod):
    tm = 512
    rows_p, rows_s = mp.shape[0], ms.shape[0]
    mod4 = mod.reshape(DEPTH, MOD_ROWS, 1, 6 * D)
    whole = lambda i, j=0: (0, 0)
    pm = lambda c: pl.BlockSpec((None, None, 1, D), lambda i: (layer, DEC_BATCH + i // (SEQ // tm), 0, c))
    sm = lambda c: pl.BlockSpec((None, rows_s, D), lambda i: (layer, 0, c))
    return pl.pallas_call(
        _wo_prep_kernel,
        grid=(rows_p // tm,),
        in_specs=[pl.BlockSpec((tm, D), lambda i: (i, 0)),
                  pl.BlockSpec((rows_s, D), whole),
                  pl.BlockSpec((None, D, D), lambda i: (layer, 0, 0), pipeline_mode=pl.Buffered(1)),
                  pl.BlockSpec((tm, D), lambda i: (i, 0)),
                  pl.BlockSpec((rows_s, D), whole),
                  pl.BlockSpec((None, 1, D), lambda i: (layer, 0, 0)),
                  pm(MOD_G1), pm(MOD_SC2), pm(MOD_SH2), sm(MOD_G1), sm(MOD_SC2), sm(MOD_SH2)],
        out_specs=[pl.BlockSpec((tm, D), lambda i: (i, 0)),
                   pl.BlockSpec((tm, D), lambda i: (i, 0)),
                   pl.BlockSpec((rows_s, D), whole),
                   pl.BlockSpec((rows_s, D), whole)],
        out_shape=[jax.ShapeDtypeStruct((rows_p, D), F32), jax.ShapeDtypeStruct((rows_p, D), BF16),
                   jax.ShapeDtypeStruct((rows_s, D), F32), jax.ShapeDtypeStruct((rows_s, D), BF16)],
        compiler_params=_cparams(1),
        name="w_o_prep",
    )(mp, ms, w_o_bf, xp, xs, norm_w.reshape(DEPTH, 1, D), mod4, mod4, mod4, mod, mod, mod)


def _epi_plain(dots, ex):
    return [dots[0]]


def _epi_merge(dots, ex):
    return [_sigmoid(ex[0]) * dots[0] + _sigmoid(ex[1]) * dots[1] + _sigmoid(ex[2]) * dots[2]]


def _epi_residual(dots, ex):
    return [ex[0] + ex[1] * dots[0]]


def _epi_swiglu(dots, ex):
    return [_silu(dots[0]) * dots[1]]


def _log_sigmoid(u):
    return -(jnp.maximum(-u, 0.0) + jnp.log1p(jnp.exp(-jnp.abs(u))))


def _gla_prompt_kernel(q_ref, k_ref, v_ref, ga_ref, lr_ref, wgk_ref, bgk_ref, nw_ref,
                       oa_ref, sfin_ref, st_s, vt_s, o_s, *, tb):
    t = pl.program_id(1)
    nsc = tb // GLA_SC
    nd = GLA_SC // GLA_CHUNK - 1

    @pl.when(t == 0)
    def _():
        st_s[...] = jnp.zeros_like(st_s)

    u = _dot(lr_ref[...], wgk_ref[...]) + bgk_ref[...]
    gk = _log_sigmoid(u) * (1.0 / 16.0)
    sub = lax.broadcasted_iota(jnp.int32, gk.shape, 0) % SUBLANE
    p8 = gk
    for s in (1, 2, 4):
        p8 = p8 + jnp.where(sub >= s, pltpu.roll(p8, s, 0), 0.0)
    nchunk = tb // GLA_CHUNK
    b_parts, bs_parts, tot = [], [], []
    acc = None
    for c in range(nchunk):
        r0 = c * GLA_CHUNK
        lo = p8[r0:r0 + SUBLANE, :]
        hi = p8[r0 + SUBLANE:r0 + GLA_CHUNK, :] + lo[SUBLANE - 1:SUBLANE, :]
        if c % (GLA_SC // GLA_CHUNK) == 0:
            b_parts += [lo, hi]
            bs_parts += [lo, hi]
            acc = hi[SUBLANE - 1:SUBLANE, :]
        else:
            b_parts += [lo, hi]
            bs_parts += [lo + acc, hi + acc]
            acc = acc + hi[SUBLANE - 1:SUBLANE, :]
        tot.append(hi[SUBLANE - 1:SUBLANE, :])
    b = jnp.concatenate(b_parts, axis=0)
    bs = jnp.concatenate(bs_parts, axis=0)

    def per_chunk(vals, shift):
        return jnp.concatenate([jnp.broadcast_to(vals[(c + shift) % nchunk], (GLA_CHUNK, GLA_KEY))
                                for c in range(nchunk)], axis=0)

    blb = per_chunk(tot, 0)
    etot = [jnp.exp(t) for t in tot]

    q = q_ref[...] * (GLA_DK ** -0.5)
    k = k_ref[...]
    qin = _bf(q * jnp.exp(b))
    kout = _bf(k * jnp.exp(-b))
    kd = k * jnp.exp(blb - b)
    qsc = _bf(q * jnp.exp(bs))
    vt_s[...] = v_ref[...].T

    ri = lax.broadcasted_iota(jnp.int32, (tb, tb), 0)
    ci = lax.broadcasted_iota(jnp.int32, (tb, tb), 1)
    delta = jnp.where(ri // GLA_SC == ci // GLA_SC, ri // GLA_CHUNK - ci // GLA_CHUNK, -1)
    m_intra = (delta == 0) & (ci <= ri)
    m_dist = [delta == d + 1 for d in range(nd)]
    kds = [_bf(kd)]
    for d in range(1, nd):
        kd = kd * per_chunk(etot, d)
        kds.append(_bf(kd))
    for h in range(GLA_H):
        ks = slice(h * GLA_DK, (h + 1) * GLA_DK)
        vs = slice(h * GLA_DV, (h + 1) * GLA_DV)
        a = jnp.where(m_intra, _dot_nt(qin[:, ks], kout[:, ks]), 0.0)
        for d in range(nd):
            a = jnp.where(m_dist[d], _dot_nt(qin[:, ks], kds[d][:, ks]), a)
        o_s[:, vs] = _dot(a, v_ref[:, vs])

    for sc in range(nsc):
        rows = slice(sc * GLA_SC, (sc + 1) * GLA_SC)
        last = bs[(sc + 1) * GLA_SC - 1:(sc + 1) * GLA_SC, :]
        k2 = _bf(k[rows, :] * jnp.exp(last - bs[rows, :]))
        elast = jnp.exp(last)
        for h in range(GLA_H):
            ks = slice(h * GLA_DK, (h + 1) * GLA_DK)
            vs = slice(h * GLA_DV, (h + 1) * GLA_DV)
            st = st_s[h]
            o_s[rows, vs] += _dot_nt(qsc[rows, ks], st)
            st_s[h] = st * elast[:, ks] + _dot(vt_s[vs, rows], k2[:, ks])

    nw = nw_ref[...]
    for h in range(GLA_H):
        vs = slice(h * GLA_DV, (h + 1) * GLA_DV)
        o = o_s[:, vs]
        y = o * lax.rsqrt(jnp.mean(o * o, axis=-1, keepdims=True) + EPS) * nw
        oa_ref[:, vs] = (y * _silu(ga_ref[:, vs])).astype(oa_ref.dtype)

    @pl.when(t == pl.num_programs(1) - 1)
    def _():
        for h in range(GLA_H):
            sfin_ref[h] = st_s[h].T


def _gla_prompt(layer, z, lr, wgk_pad, b_gk, gla_norm_w):
    tb = 256
    nt = SEQ // tb
    row = lambda b, t: b * nt + t
    kern = functools.partial(_gla_prompt_kernel, tb=tb)
    return pl.pallas_call(
        kern,
        grid=(BATCH, nt),
        in_specs=[pl.BlockSpec((tb, GLA_KEY), lambda b, t: (row(b, t), Z_QA // GLA_KEY)),
                  pl.BlockSpec((tb, GLA_KEY), lambda b, t: (row(b, t), Z_KA // GLA_KEY)),
                  pl.BlockSpec((tb, GLA_VAL), lambda b, t: (row(b, t), Z_VA // GLA_VAL)),
                  pl.BlockSpec((tb, GLA_VAL), lambda b, t: (row(b, t), Z_GA // GLA_VAL)),
                  pl.BlockSpec((tb, LR_PAD), lambda b, t: (row(b, t), 0)),
                  pl.BlockSpec((None, LR_PAD, GLA_KEY), lambda b, t: (layer, 0, 0)),
                  pl.BlockSpec((None, 1, GLA_KEY), lambda b, t: (layer, 0, 0)),
                  pl.BlockSpec((None, 1, GLA_DV), lambda b, t: (layer, 0, 0))],
        out_specs=[pl.BlockSpec((tb, GLA_VAL), lambda b, t: (row(b, t), 0)),
                   pl.BlockSpec((None, GLA_H, GLA_DK, GLA_DV), lambda b, t: (b, 0, 0, 0))],
        out_shape=[jax.ShapeDtypeStruct((BATCH * SEQ, GLA_VAL), BF16),
                   jax.ShapeDtypeStruct((BATCH, GLA_H, GLA_DK, GLA_DV), F32)],
        scratch_shapes=[pltpu.VMEM((GLA_H, GLA_DV, GLA_DK), F32),
                        pltpu.VMEM((GLA_VAL, tb), F32),
                        pltpu.VMEM((tb, GLA_VAL), F32)],
        compiler_params=_cparams(2),
        name="gla_prompt",
    )(z, z, z, z, lr, wgk_pad, b_gk.reshape(DEPTH, 1, GLA_KEY), gla_norm_w.reshape(DEPTH, 1, GLA_DV))


def _layernorm(x, w, b):
    mu = jnp.mean(x, axis=-1, keepdims=True)
    xc = x - mu
    var = jnp.mean(xc * xc, axis=-1, keepdims=True)
    return xc * lax.rsqrt(var + EPS) * w + b


def _gmlp_prompt_kernel(u_ref, v_ref, ws_ref, bst_ref, nw_ref, nb_ref, ob_ref, *, nsub):
    ri = lax.broadcasted_iota(jnp.int32, (GM_CHUNK, GM_CHUNK), 0)
    ci = lax.broadcasted_iota(jnp.int32, (GM_CHUNK, GM_CHUNK), 1)
    tril = ci <= ri
    for s in range(nsub):
        rs = slice(s * GM_CHUNK, (s + 1) * GM_CHUNK)
        u = _gelu(u_ref[rs, :])
        v = _layernorm(_gelu(v_ref[rs, :]), nw_ref[...], nb_ref[...])
        for g in range(GM_GROUPS):
            cs = slice(g * GM_GW, (g + 1) * GM_GW)
            wm = jnp.where(tril, ws_ref[g], 0.0)
            mixed = _dot(wm, v[:, cs]) + bst_ref[:, g:g + 1]
            ob_ref[rs, cs] = (u[:, cs] * mixed).astype(ob_ref.dtype)


def _gmlp_prompt(layer, z, gm_ws, gm_bs_t, gm_norm_w, gm_norm_b):
    nsub = 4
    tb = nsub * GM_CHUNK
    kern = functools.partial(_gmlp_prompt_kernel, nsub=nsub)
    return pl.pallas_call(
        kern,
        grid=(BATCH * SEQ // tb,),
        in_specs=[pl.BlockSpec((tb, GM_WIDTH), lambda i: (i, Z_UB // GM_WIDTH)),
                  pl.BlockSpec((tb, GM_WIDTH), lambda i: (i, Z_VB // GM_WIDTH)),
                  pl.BlockSpec((None, GM_GROUPS, GM_CHUNK, GM_CHUNK), lambda i: (layer, 0, 0, 0)),
                  pl.BlockSpec((None, GM_CHUNK, GM_GROUPS), lambda i: (layer, 0, 0)),
                  pl.BlockSpec((None, 1, GM_WIDTH), lambda i: (layer, 0, 0)),
                  pl.BlockSpec((None, 1, GM_WIDTH), lambda i: (layer, 0, 0))],
        out_specs=pl.BlockSpec((tb, GM_WIDTH), lambda i: (i, 0)),
        out_shape=jax.ShapeDtypeStruct((BATCH * SEQ, GM_WIDTH), BF16),
        compiler_params=_cparams(1),
        name="gmlp_prompt",
    )(z, z, gm_ws, gm_bs_t, gm_norm_w.reshape(DEPTH, 1, GM_WIDTH), gm_norm_b.reshape(DEPTH, 1, GM_WIDTH))


def _gmlp_sample_kernel(u_ref, v_ref, w0_ref, b0_ref, nw_ref, nb_ref, ob_ref, vn_ref):
    u = _gelu(u_ref[...])
    v = _layernorm(_gelu(v_ref[...]), nw_ref[...], nb_ref[...])
    vn_ref[...] = v
    ob_ref[...] = u * (w0_ref[...] * v + b0_ref[...])


def _gmlp_sample(layer, z, w0_row, b0_row, gm_norm_w, gm_norm_b):
    full = lambda i: (0, 0)
    lrow = lambda i: (layer, 0, 0)
    return pl.pallas_call(
        _gmlp_sample_kernel,
        grid=(1,),
        in_specs=[pl.BlockSpec((DEC_BATCH, GM_WIDTH), lambda i: (0, Z_UB // GM_WIDTH)),
                  pl.BlockSpec((DEC_BATCH, GM_WIDTH), lambda i: (0, Z_VB // GM_WIDTH)),
                  pl.BlockSpec((None, 1, GM_WIDTH), lrow),
                  pl.BlockSpec((None, 1, GM_WIDTH), lrow),
                  pl.BlockSpec((None, 1, GM_WIDTH), lrow),
                  pl.BlockSpec((None, 1, GM_WIDTH), lrow)],
        out_specs=[pl.BlockSpec((DEC_BATCH, GM_WIDTH), full), pl.BlockSpec((DEC_BATCH, GM_WIDTH), full)],
        out_shape=[jax.ShapeDtypeStruct((DEC_BATCH, GM_WIDTH), F32),
                   jax.ShapeDtypeStruct((DEC_BATCH, GM_WIDTH), F32)],
        compiler_params=_cparams(1),
        name="gmlp_sample",
    )(z, z, w0_row, b0_row, gm_norm_w.reshape(DEPTH, 1, GM_WIDTH), gm_norm_b.reshape(DEPTH, 1, GM_WIDTH))


def _alibi_slope(h):
    return float(2.0 ** (-8.0 * (h + 1) / SWA_HQ))


def _swa_lane_halves(x, half):
    lane = lax.broadcasted_iota(jnp.int32, x.shape, 1)
    own = jnp.where((lane >= half * SWA_HD) & (lane < (half + 1) * SWA_HD), x, 0.0)
    other = pltpu.roll(own, SWA_HD, 1)
    return (own, other) if half == 0 else (other, own)


def _swa_prompt_block(n, q_ref, kc_ref, kp_ref, vc_ref, vp_ref, sink_ref, oc_ref, s_s, p_s):
    w = WINDOW
    ri = lax.broadcasted_iota(jnp.int32, (w, 2 * w), 0)
    ci = lax.broadcasted_iota(jnp.int32, (w, 2 * w), 1)
    dist_i = w + ri - ci
    valid = (dist_i >= 0) & (dist_i < w) & ((ci >= w) | (n > 0))
    dist = dist_i.astype(F32)
    kcat = jnp.concatenate([kp_ref[...], kc_ref[...]], axis=0)
    vcat = jnp.concatenate([vp_ref[...], vc_ref[...]], axis=0)
    heads = []
    for kv in range(SWA_HKV):
        t, half = kv // 2, kv % 2
        k_lo, k_hi = _swa_lane_halves(kcat[:, t * LANE:(t + 1) * LANE], half)
        q2 = jnp.concatenate([q_ref[:, 2 * kv * LANE:(2 * kv + 1) * LANE],
                              q_ref[:, (2 * kv + 1) * LANE:(2 * kv + 2) * LANE]], axis=0)
        for par, kk in ((0, k_lo), (1, k_hi)):
            s = _dot_nt(q2, kk) * (SWA_HD ** -0.5)
            for e in range(2):
                h = SWA_G * kv + 2 * e + par
                seg = len(heads)
                heads.append(h)
                s_s[seg * w:(seg + 1) * w, :] = jnp.where(
                    valid, s[e * w:(e + 1) * w, :] - _alibi_slope(h) * dist, NEG_BIG)
    s = s_s[...]
    sink = jnp.concatenate([jnp.broadcast_to(sink_ref[h:h + 1, 0:1], (w, 1)) for h in heads], axis=0)
    m = jnp.maximum(jnp.max(s, axis=-1, keepdims=True), sink)
    p = jnp.exp(s - m)
    inv = 1.0 / (jnp.sum(p, axis=-1, keepdims=True) + jnp.exp(sink - m))
    p_s[...] = (p * inv).astype(p_s.dtype)
    for kv in range(SWA_HKV):
        t, half = kv // 2, kv % 2
        v_lo, v_hi = _swa_lane_halves(vcat[:, t * LANE:(t + 1) * LANE], half)
        r0 = SWA_G * kv * w
        o = _dot(p_s[r0:r0 + 2 * w, :], v_lo) + _dot(p_s[r0 + 2 * w:r0 + 4 * w, :], v_hi)
        oc_ref[:, 2 * kv * LANE:(2 * kv + 1) * LANE] = o[:w].astype(oc_ref.dtype)
        oc_ref[:, (2 * kv + 1) * LANE:(2 * kv + 2) * LANE] = o[w:].astype(oc_ref.dtype)


STATE_SLOTS = 3


def _state_copy(hbm, sbuf, sem, layer, step, rps, to_hbm):
    slot = lax.rem(step, STATE_SLOTS)
    rows = hbm.at[layer, pl.ds(step * rps, rps)]
    if to_hbm:
        return pltpu.make_async_copy(sbuf.at[slot], rows, sem.at[slot])
    return pltpu.make_async_copy(rows, sbuf.at[slot], sem.at[slot])


def _state_ring_begin(g, n_steps, copy_in, copy_out, x_s, oa_s):
    @pl.when(g == 0)
    def _():
        x_s[...] = jnp.zeros_like(x_s)
        oa_s[...] = jnp.zeros_like(oa_s)
        copy_in(0).start()

    @pl.when(g >= STATE_SLOTS - 1)
    def _():
        copy_out(g - (STATE_SLOTS - 1)).wait()

    @pl.when(g + 1 < n_steps)
    def _():
        copy_in(g + 1).start()

    copy_in(g).wait()


def _state_ring_end(g, n_steps, copy_out):
    copy_out(g).start()

    @pl.when(g == n_steps - 1)
    def _():
        for back in range(STATE_SLOTS - 2, -1, -1):
            copy_out(g - back).wait()


def _gla_decode_rows(g, rps, zq_ref, zk_ref, zv_ref, zga_ref, lr_ref, wgk_ref, bgk_ref, nw_ref,
                     oa_ref, sbuf, x_s, oa_s):
    slot = lax.rem(g, STATE_SLOTS)
    sub = lax.rem(g * rps, SUBLANE)
    up = lax.rem(SUBLANE - sub, SUBLANE)
    u = _dot(lr_ref[...], wgk_ref[...]) + bgk_ref[...]
    eg = pltpu.roll(jnp.exp(_log_sigmoid(u) * (1.0 / 16.0)), up, 0)
    zq = pltpu.roll(zq_ref[...], up, 0) * (GLA_DK ** -0.5)
    zk = pltpu.roll(zk_ref[...], up, 0)
    zv = pltpu.roll(zv_ref[...], up, 0)
    gate = _silu(pltpu.roll(zga_ref[...], up, 0))
    nw = nw_ref[...]
    row8 = lax.broadcasted_iota(jnp.int32, (SUBLANE, GLA_DV), 0)
    for h in range(GLA_H):
        ks = slice(h * GLA_DK, (h + 1) * GLA_DK)
        vs = slice(h * GLA_DV, (h + 1) * GLA_DV)
        x_s[0:SUBLANE, :] = zq[:, ks]
        x_s[SUBLANE:2 * SUBLANE, :] = zk[:, ks]
        x_s[2 * SUBLANE:3 * SUBLANE, :] = eg[:, ks]
        xt = x_s[...].T
        y8 = jnp.zeros((SUBLANE, GLA_DV), F32)
        for r in range(rps):
            qc = xt[:, r:r + 1]
            kc = xt[:, SUBLANE + r:SUBLANE + r + 1]
            gc = xt[:, 2 * SUBLANE + r:2 * SUBLANE + r + 1]
            s_new = gc * sbuf[slot, r, h] + kc * zv[r:r + 1, vs]
            sbuf[slot, r, h] = s_new
            o = jnp.sum(qc * s_new, axis=0, keepdims=True)
            y = o * lax.rsqrt(jnp.mean(o * o, axis=-1, keepdims=True) + EPS) * nw * gate[r:r + 1, vs]
            y8 = jnp.where(row8 == r, y, y8)
        y8 = pltpu.roll(y8, sub, 0)
        acc = jnp.where(sub == 0, y8, oa_s[:, vs] + y8)
        oa_s[:, vs] = acc
        oa_ref[:, vs] = acc


def _swa_gla_kernel(q_ref, kc_ref, kp_ref, vc_ref, vp_ref, sink_ref,
                    zq_ref, zk_ref, zv_ref, zga_ref, lr_ref, wgk_ref, bgk_ref, nw_ref, sin_hbm, *rest,
                    layer, rps, aliased):
    if aliased:
        rest = rest[1:]
    oc_ref, oa_ref, sout_hbm, s_s, p_s, sbuf, x_s, oa_s, in_sem, out_sem = rest
    n = pl.program_id(1)
    g = pl.program_id(0) * pl.num_programs(1) + n
    n_steps = pl.num_programs(0) * pl.num_programs(1)
    copy_in = functools.partial(_state_copy, sin_hbm, sbuf, in_sem, layer, rps=rps, to_hbm=False)
    copy_out = functools.partial(_state_copy, sout_hbm, sbuf, out_sem, layer, rps=rps, to_hbm=True)
    _state_ring_begin(g, n_steps, copy_in, copy_out, x_s, oa_s)
    _gla_decode_rows(g, rps, zq_ref, zk_ref, zv_ref, zga_ref, lr_ref, wgk_ref, bgk_ref, nw_ref,
                     oa_ref, sbuf, x_s, oa_s)
    _swa_prompt_block(n, q_ref, kc_ref, kp_ref, vc_ref, vp_ref, sink_ref, oc_ref, s_s, p_s)
    _state_ring_end(g, n_steps, copy_out)


def _swa_prompt_gla_sample(layer, zp, sinks_b, zs, lrs, wgk_pad, b_gk, gla_norm_w, state_gla, state_out):
    nb = SEQ // WINDOW
    n_steps = BATCH * nb
    rps = DEC_BATCH // n_steps
    assert rps * n_steps == DEC_BATCH and SUBLANE % rps == 0 and n_steps >= STATE_SLOTS
    row = lambda b, n: b * nb + n
    prev = lambda b, n: b * nb + jnp.maximum(n - 1, 0)
    srow = lambda b, n: (row(b, n) * rps) // SUBLANE
    in_specs = [pl.BlockSpec((WINDOW, SWA_Q), lambda b, n: (row(b, n), Z_QC // SWA_Q)),
                pl.BlockSpec((WINDOW, SWA_KV), lambda b, n: (row(b, n), Z_KC // SWA_KV)),
                pl.BlockSpec((WINDOW, SWA_KV), lambda b, n: (prev(b, n), Z_KC // SWA_KV)),
                pl.BlockSpec((WINDOW, SWA_KV), lambda b, n: (row(b, n), Z_VC // SWA_KV)),
                pl.BlockSpec((WINDOW, SWA_KV), lambda b, n: (prev(b, n), Z_VC // SWA_KV)),
                pl.BlockSpec((None, SWA_HQ, LANE), lambda b, n: (layer, 0, 0)),
                pl.BlockSpec((SUBLANE, GLA_KEY), lambda b, n: (srow(b, n), Z_QA // GLA_KEY)),
                pl.BlockSpec((SUBLANE, GLA_KEY), lambda b, n: (srow(b, n), Z_KA // GLA_KEY)),
                pl.BlockSpec((SUBLANE, GLA_VAL), lambda b, n: (srow(b, n), Z_VA // GLA_VAL)),
                pl.BlockSpec((SUBLANE, GLA_VAL), lambda b, n: (srow(b, n), Z_GA // GLA_VAL)),
                pl.BlockSpec((SUBLANE, LR_PAD), lambda b, n: (srow(b, n), 0)),
                pl.BlockSpec((None, LR_PAD, GLA_KEY), lambda b, n: (layer, 0, 0)),
                pl.BlockSpec((None, 1, GLA_KEY), lambda b, n: (layer, 0, 0)),
                pl.BlockSpec((None, 1, GLA_DV), lambda b, n: (layer, 0, 0)),
                pl.BlockSpec(memory_space=pl.ANY)]
    args = [zp, zp, zp, zp, zp, sinks_b, zs, zs, zs, zs, lrs, wgk_pad, b_gk.reshape(DEPTH, 1, GLA_KEY),
            gla_norm_w.reshape(DEPTH, 1, GLA_DV), state_gla]
    aliases = {}
    if state_out is not None:
        in_specs.append(pl.BlockSpec(memory_space=pl.ANY))
        args.append(state_out)
        aliases = {len(args) - 1: 2}
    kern = functools.partial(_swa_gla_kernel, layer=layer, rps=rps, aliased=state_out is not None)
    return pl.pallas_call(
        kern,
        grid=(BATCH, nb),
        in_specs=in_specs,
        out_specs=[pl.BlockSpec((WINDOW, SWA_Q), lambda b, n: (row(b, n), 0)),
                   pl.BlockSpec((SUBLANE, GLA_VAL), lambda b, n: (srow(b, n), 0)),
                   pl.BlockSpec(memory_space=pl.ANY)],
        out_shape=[jax.ShapeDtypeStruct((BATCH * SEQ, SWA_Q), BF16),
                   jax.ShapeDtypeStruct((DEC_BATCH, GLA_VAL), F32),
                   jax.ShapeDtypeStruct((DEPTH, DEC_BATCH, GLA_H, GLA_DK, GLA_DV), F32)],
        scratch_shapes=[pltpu.VMEM((SWA_HQ * WINDOW, 2 * WINDOW), F32),
                        pltpu.VMEM((SWA_HQ * WINDOW, 2 * WINDOW), BF16),
                        pltpu.VMEM((STATE_SLOTS, rps, GLA_H, GLA_DK, GLA_DV), F32),
                        pltpu.VMEM((LANE, GLA_DK), F32),
                        pltpu.VMEM((SUBLANE, GLA_VAL), F32),
                        pltpu.SemaphoreType.DMA((STATE_SLOTS,)),
                        pltpu.SemaphoreType.DMA((STATE_SLOTS,))],
        input_output_aliases=aliases,
        compiler_params=_cparams(2),
        name="swa_prompt_gla_sample",
    )(*args)


def _swa_sample_kernel(q_ref, kn_ref, vn_ref, kt_ref, vt_ref, sink_ref, slope_ref, o_ref, *, rb):
    wb = WINDOW
    j = lax.broadcasted_iota(jnp.int32, (SWA_HQ, wb), 1)
    dist = (wb - j).astype(F32)
    ok = j >= 1
    grp = lax.broadcasted_iota(jnp.int32, (SWA_HQ, 1), 0) // SWA_G
    slope = slope_ref[:, 0:1]
    sink = sink_ref[:, 0:1]

    def per_head(rows):
        out = jnp.broadcast_to(rows[0:1, :], (SWA_HQ, SWA_HD))
        for kv in range(1, SWA_HKV):
            out = jnp.where(grp == kv, rows[kv:kv + 1, :], out)
        return out

    s_rows, self_rows = [], []
    for r in range(rb):
        q = q_ref[r]
        s = _dot(q, kt_ref[r, 0])
        for kv in range(1, SWA_HKV):
            s = jnp.where(grp == kv, _dot(q, kt_ref[r, kv]), s)
        s_rows.append(jnp.where(ok, s * (SWA_HD ** -0.5) - slope * dist, NEG_BIG))
        self_rows.append(jnp.sum(_bf(q).astype(F32) * _bf(per_head(kn_ref[r])).astype(F32), axis=-1,
                                 keepdims=True) * (SWA_HD ** -0.5))
    s = jnp.concatenate(s_rows, axis=0)
    s_self = jnp.concatenate(self_rows, axis=0)
    sink = jnp.concatenate([sink] * rb, axis=0)
    m = jnp.maximum(jnp.maximum(jnp.max(s, axis=-1, keepdims=True), s_self), sink)
    p = jnp.exp(s - m)
    p_self = jnp.exp(s_self - m)
    inv = 1.0 / (jnp.sum(p, axis=-1, keepdims=True) + p_self + jnp.exp(sink - m))
    pn = _bf(p * inv)
    pn_self = _bf(p_self * inv).astype(F32)
    for r in range(rb):
        rows = slice(r * SWA_HQ, (r + 1) * SWA_HQ)
        o = _dot_nt(pn[rows, :], vt_ref[r, 0])
        for kv in range(1, SWA_HKV):
            o = jnp.where(grp == kv, _dot_nt(pn[rows, :], vt_ref[r, kv]), o)
        o_ref[r] = o + pn_self[rows, :] * _bf(per_head(vn_ref[r])).astype(F32)


def _swa_sample(layer, q3, kn3, vn3, cache_kt, cache_vt, sinks_b, slopes_b):
    rb = 8
    kern = functools.partial(_swa_sample_kernel, rb=rb)
    return pl.pallas_call(
        kern,
        grid=(DEC_BATCH // rb,),
        in_specs=[pl.BlockSpec((rb, SWA_HQ, SWA_HD), lambda i: (i, 0, 0)),
                  pl.BlockSpec((rb, SWA_HKV, SWA_HD), lambda i: (i, 0, 0)),
                  pl.BlockSpec((rb, SWA_HKV, SWA_HD), lambda i: (i, 0, 0)),
                  pl.BlockSpec((None, rb, SWA_HKV, SWA_HD, WINDOW), lambda i: (layer, i, 0, 0, 0)),
                  pl.BlockSpec((None, rb, SWA_HKV, SWA_HD, WINDOW), lambda i: (layer, i, 0, 0, 0)),
                  pl.BlockSpec((None, SWA_HQ, LANE), lambda i: (layer, 0, 0)),
                  pl.BlockSpec((SWA_HQ, LANE), lambda i: (0, 0))],
        out_specs=pl.BlockSpec((rb, SWA_HQ, SWA_HD), lambda i: (i, 0, 0)),
        out_shape=jax.ShapeDtypeStruct((DEC_BATCH, SWA_HQ, SWA_HD), F32),
        compiler_params=_cparams(1),
        name="swa_sample",
    )(q3, kn3, vn3, cache_kt, cache_vt, sinks_b, slopes_b)


def _repack_kernel(a_ref, b_ref, o_ref, *, n_plain):
    j = pl.program_id(1)

    @pl.when(j < n_plain)
    def _():
        o_ref[...] = a_ref[...].T.astype(o_ref.dtype)

    @pl.when(j >= n_plain)
    def _():
        o_ref[...] = jnp.concatenate([a_ref[GLA_RANK:, :], b_ref[...]], axis=0).T.astype(o_ref.dtype)


def _repack_w_in_t(w_in_t):
    tn = 512
    assert LR_COL % tn == 0 and Z_WIDTH % tn == 0 and tn % GLA_RANK == 0
    kern = functools.partial(_repack_kernel, n_plain=LR_COL // tn)
    return pl.pallas_call(
        kern,
        grid=(DEPTH, Z_WIDTH // tn),
        in_specs=[pl.BlockSpec((None, tn, D), lambda l, j: (l, j, 0)),
                  pl.BlockSpec((None, GLA_RANK, D), lambda l, j: (l, (j + 1) * (tn // GLA_RANK), 0))],
        out_specs=pl.BlockSpec((None, D, tn), lambda l, j: (l, 0, j)),
        out_shape=jax.ShapeDtypeStruct((DEPTH, D, Z_WIDTH), BF16),
        compiler_params=_cparams(2),
        name="repack_w_in",
    )(w_in_t, w_in_t)


def _layer(layer, xp, xs, mod, p, state_gla, cache_k, cache_v, state_out):
    hp, lrp = _prep(layer, xp, p["norm1_w"], mod, MOD_SC1, MOD_SH1, False, p["w_lr_t"])
    hs, lrs = _prep(layer, xs, p["norm1_w"], mod, MOD_SC1, MOD_SH1, True, p["w_lr_t"])
    (zp,), (zs,) = _fused_matmul("w_in", layer, [hp], [hs], [(0, p["w_in_r"], 0, False)], [], _epi_plain, [F32],
                                 Z_WIDTH, SEQ, 512, lhs_buffers=1)
    oa_p, s_p = _gla_prompt(layer, zp, lrp, p["wgk_pad"], p["b_gk"], p["gla_norm_w"])
    ob_p = _gmlp_prompt(layer, zp, p["gm_ws"], p["gm_bs_t"], p["gm_norm_w"], p["gm_norm_b"])
    oc_p, oa_s, state_out = _swa_prompt_gla_sample(layer, zp, p["sinks_b"], zs, lrs, p["wgk_pad"], p["b_gk"],
                                                   p["gla_norm_w"], state_gla, state_out)
    z4 = zp.reshape(BATCH, SEQ, Z_WIDTH)
    kp_rows = z4[:, SEQ - WINDOW:, Z_KC:Z_KC + SWA_KV].reshape(BATCH, WINDOW, SWA_HKV, SWA_HD)
    vp_rows = z4[:, SEQ - WINDOW:, Z_VC:Z_VC + SWA_KV].reshape(BATCH, WINDOW, SWA_HKV, SWA_HD)
    ob_s, v_gm = _gmlp_sample(layer, zs, p["gm_w0"], p["gm_b0"], p["gm_norm_w"], p["gm_norm_b"])
    q3 = zs[:, Z_QC:Z_QC + SWA_Q].reshape(DEC_BATCH, SWA_HQ, SWA_HD)
    kn3 = zs[:, Z_KC:Z_KC + SWA_KV].reshape(DEC_BATCH, SWA_HKV, SWA_HD)
    vn3 = zs[:, Z_VC:Z_VC + SWA_KV].reshape(DEC_BATCH, SWA_HKV, SWA_HD)
    oc_s = _swa_sample(layer, q3, kn3, vn3, cache_k, cache_v, p["sinks_b"], p["slopes_b"]).reshape(DEC_BATCH, SWA_Q)
    ks_rows = kn3.reshape(DEC_BATCH, 1, SWA_HKV, SWA_HD)
    vs_rows = vn3.reshape(DEC_BATCH, 1, SWA_HKV, SWA_HD)
    (mp,), (ms,) = _fused_matmul(
        "merge", layer, [oa_p, ob_p, oc_p], [oa_s, ob_s, oc_s],
        [(0, p["w_pa"], 0, False), (1, p["w_pb"], 0, False), (2, p["w_pc"], 0, False)],
        [("tile", zp, zs, Z_GATES), ("tile", zp, zs, Z_GATES + D), ("tile", zp, zs, Z_GATES + 2 * D)],
        _epi_merge, [BF16], D, 2048, 256, lhs_buffers=1)
    x1p, h2p, x1s, h2s = _wo_prep(layer, mp, ms, p["w_o_bf"], xp, xs, p["norm2_w"], mod)
    (hidp,), (hids,) = _fused_matmul(
        "ffn_in", layer, [h2p], [h2s], [(0, p["w_ffn_in"], 0, False), (0, p["w_ffn_in"], FFN_HIDDEN, False)],
        [], _epi_swiglu, [BF16], FFN_HIDDEN, 2048, 512)
    (x2p,), (x2s,) = _fused_matmul("ffn_out", layer, [hidp], [hids], [(0, p["w_ffn_out"], 0, False)],
                                   [("tile", x1p, x1s, 0), ("mod", mod, MOD_G2)], _epi_residual, [F32],
                                   D, 1024, 512)
    return x2p, x2s, s_p, state_out, kp_rows, vp_rows, ks_rows, vs_rows, v_gm


def kernel(x_prompt, x_sample, c_prompt, c_sample, state_gla, cache_swa_k, cache_swa_v, w_ada, b_ada, norm1_w,
           norm2_w, w_in, w_gk2, b_gk, gla_norm_w, gm_norm_w, gm_norm_b, gm_ws, gm_bs, swa_sinks, w_pa, w_pb,
           w_pc, w_o, w_ffn_in, w_ffn_out, final_norm_w):
    w_in_t = jnp.swapaxes(w_in, 1, 2)
    w_lr_t = jnp.pad(w_in_t[:, LR_COL:LR_COL + GLA_RANK, :], ((0, 0), (0, LR_PAD - GLA_RANK), (0, 0))).astype(BF16)
    p = {
        "norm1_w": norm1_w, "norm2_w": norm2_w, "w_in_r": _repack_w_in_t(w_in_t), "w_lr_t": w_lr_t,
        "wgk_pad": jnp.pad(w_gk2, ((0, 0), (0, LR_PAD - GLA_RANK), (0, 0))),
        "b_gk": b_gk, "gla_norm_w": gla_norm_w, "gm_norm_w": gm_norm_w, "gm_norm_b": gm_norm_b,
        "gm_ws": gm_ws, "gm_bs_t": jnp.swapaxes(gm_bs, 1, 2),
        "gm_w0": jnp.repeat(gm_ws[:, :, 0, 0], GM_GW, axis=1).reshape(DEPTH, 1, GM_WIDTH),
        "gm_b0": jnp.repeat(gm_bs[:, :, 0], GM_GW, axis=1).reshape(DEPTH, 1, GM_WIDTH),
        "sinks_b": jnp.broadcast_to(swa_sinks[:, :, None], (DEPTH, SWA_HQ, LANE)),
        "slopes_b": jnp.broadcast_to(
            jnp.asarray([_alibi_slope(h) for h in range(SWA_HQ)], F32)[:, None], (SWA_HQ, LANE)),
        "w_pa": w_pa, "w_pb": w_pb, "w_pc": w_pc, "w_o_bf": w_o.astype(BF16), "w_ffn_in": w_ffn_in,
        "w_ffn_out": w_ffn_out.astype(BF16),
    }
    c_all = jnp.concatenate([c_sample, c_prompt, jnp.zeros((MOD_ROWS - DEC_BATCH - BATCH, D), F32)], axis=0)
    mod = _ada(c_all, w_ada, b_ada)

    xp = x_prompt.reshape(BATCH * SEQ, D)
    xs = x_sample.reshape(DEC_BATCH, D)
    cache_k = jnp.transpose(cache_swa_k, (0, 1, 3, 4, 2))
    cache_v = jnp.transpose(cache_swa_v, (0, 1, 3, 4, 2))
    gla_p, kp, vp, ksm, vsm, gmv = [], [], [], [], [], []
    state_out = None
    for l in range(DEPTH):
        xp, xs, s_p, state_out, k_p, v_p, k_s, v_s, gv = _layer(l, xp, xs, mod, p, state_gla, cache_k, cache_v,
                                                                state_out)
        gla_p.append(s_p)
        kp.append(k_p)
        vp.append(v_p)
        ksm.append(k_s)
        vsm.append(v_s)
        gmv.append(gv.reshape(DEC_BATCH, 1, GM_WIDTH))
    y_prompt = _final_norm(xp, final_norm_w).reshape(BATCH, SEQ, D)
    y_sample = _final_norm(xs, final_norm_w).reshape(DEC_BATCH, 1, D)
    return (y_prompt, y_sample, jnp.stack(gla_p), state_out, jnp.stack(kp), jnp.stack(vp),
            jnp.stack(ksm), jnp.stack(vsm), jnp.stack(gmv))
```

```python
import functools

import jax
import jax.numpy as jnp
import numpy as np
from jax import lax
from jax.experimental import pallas as pl
from jax.experimental.pallas import tpu as pltpu

F32 = jnp.float32
BF16 = jnp.bfloat16

D = 2048
BATCH, SEQ = 2, 4096
DEPTH = 2
DEC_BATCH = 128
GLA_H, GLA_DK, GLA_DV = 4, 256, 512
GLA_KEY, GLA_VAL = GLA_H * GLA_DK, GLA_H * GLA_DV
GLA_RANK = 16
GLA_CHUNK = 16
GLA_SC = 128
GM_WIDTH, GM_GROUPS, GM_CHUNK = 1024, 4, 128
GM_GW = GM_WIDTH // GM_GROUPS
SWA_HQ, SWA_HKV, SWA_HD, WINDOW = 16, 4, 64, 128
SWA_G = SWA_HQ // SWA_HKV
SWA_Q, SWA_KV = SWA_HQ * SWA_HD, SWA_HKV * SWA_HD
FFN_HIDDEN = 5632
EPS = 1e-6
NEG_BIG = -1e30

Z_QA, Z_KA, Z_VA, Z_GA = 0, 1024, 2048, 4096
Z_UB, Z_VB = 6144, 7168
Z_QC, Z_KC, Z_VC = 8192, 9216, 9472
Z_GATES = 9728
Z_WIDTH = 15872
LR_COL = 6144
LANE = 128
SUBLANE = 8
LR_PAD = LANE

MOD_SH1, MOD_SC1, MOD_G1, MOD_SH2, MOD_SC2, MOD_G2 = range(6)
MOD_ROWS = DEC_BATCH + 8

VMEM_LIMIT = 58 * 1024 * 1024


def _cparams(n_axes):
    return pltpu.CompilerParams(dimension_semantics=("arbitrary",) * n_axes,
                                vmem_limit_bytes=VMEM_LIMIT)


def _bf(x):
    return x if x.dtype == BF16 else x.astype(BF16)


def _dot(a, b):
    return jnp.dot(_bf(a), _bf(b), preferred_element_type=F32)


def _dot_nt(a, b):
    return lax.dot_general(_bf(a), _bf(b), (((1,), (1,)), ((), ())), preferred_element_type=F32)


def _silu(x):
    return x * (1.0 / (1.0 + jnp.exp(-x)))


def _sigmoid(x):
    return 1.0 / (1.0 + jnp.exp(-x))


def _gelu(x):
    return 0.5 * x * (1.0 + jnp.tanh(np.sqrt(2.0 / np.pi).astype(np.float32) * (x + 0.044715 * (x * x * x))))


def _mod_spec_prompt(layer, chunk, tm, tn):
    cb, bpb = chunk * D // tn, SEQ // tm
    return pl.BlockSpec((None, None, 1, tn), lambda i, j: (layer, DEC_BATCH + i // bpb, 0, j + cb))


def _mod_spec_sample(layer, chunk, tn, jmap):
    cb = chunk * D // tn
    return pl.BlockSpec((None, DEC_BATCH, tn), lambda i, j: (layer, 0, jmap(i, j) + cb))


def _ada_kernel(c_ref, w_ref, b_ref, o_ref):
    o_ref[...] = _dot(_silu(c_ref[...]), w_ref[...]) + b_ref[...]


def _ada(c_all, w_ada, b_ada):
    tn = 1024
    return pl.pallas_call(
        _ada_kernel,
        grid=(DEPTH, 6 * D // tn),
        in_specs=[pl.BlockSpec((MOD_ROWS, D), lambda l, j: (0, 0)),
                  pl.BlockSpec((None, D, tn), lambda l, j: (l, 0, j)),
                  pl.BlockSpec((None, 1, tn), lambda l, j: (l, 0, j))],
        out_specs=pl.BlockSpec((None, MOD_ROWS, tn), lambda l, j: (l, 0, j)),
        out_shape=jax.ShapeDtypeStruct((DEPTH, MOD_ROWS, 6 * D), F32),
        compiler_params=_cparams(2),
        name="ada",
    )(c_all, w_ada, b_ada.reshape(DEPTH, 1, 6 * D))


def _prep_kernel(x_ref, nw_ref, sc_ref, sh_ref, *rest):
    x = x_ref[...]
    y = x * lax.rsqrt(jnp.mean(x * x, axis=-1, keepdims=True) + EPS) * nw_ref[...]
    h = (y * (1.0 + sc_ref[...]) + sh_ref[...]).astype(BF16)
    if len(rest) == 1:
        (o_ref,) = rest
    else:
        wlr_ref, o_ref, lr_ref = rest
        lr_ref[...] = _dot_nt(h, wlr_ref[...])
    o_ref[...] = h


def _prep(layer, x, norm_w, mod, sc_chunk, sh_chunk, sample, w_lr_t=None):
    rows = x.shape[0]
    if sample:
        tm, modop = rows, mod
        mod_specs = [_mod_spec_sample(layer, c, D, lambda i, j: j) for c in (sc_chunk, sh_chunk)]
    else:
        tm, modop = 512, mod.reshape(DEPTH, MOD_ROWS, 1, 6 * D)
        mod_specs = [_mod_spec_prompt(layer, c, tm, D) for c in (sc_chunk, sh_chunk)]
    args = [x, norm_w.reshape(DEPTH, 1, D), modop, modop]
    in_specs = [pl.BlockSpec((tm, D), lambda i, j: (i, 0)),
                pl.BlockSpec((None, 1, D), lambda i, j: (layer, 0, 0))] + mod_specs
    out_specs = [pl.BlockSpec((tm, D), lambda i, j: (i, 0))]
    out_shape = [jax.ShapeDtypeStruct((rows, D), BF16)]
    if w_lr_t is not None:
        args.append(w_lr_t)
        in_specs.append(pl.BlockSpec((None, LR_PAD, D), lambda i, j: (layer, 0, 0)))
        out_specs.append(pl.BlockSpec((tm, LR_PAD), lambda i, j: (i, 0)))
        out_shape.append(jax.ShapeDtypeStruct((rows, LR_PAD), F32))
    res = pl.pallas_call(
        _prep_kernel,
        grid=(rows // tm, 1),
        in_specs=in_specs,
        out_specs=out_specs,
        out_shape=out_shape,
        compiler_params=_cparams(2),
        name="prep",
    )(*args)
    return res if w_lr_t is not None else res[0]


def _final_norm_kernel(x_ref, nw_ref, o_ref):
    x = x_ref[...]
    o_ref[...] = x * lax.rsqrt(jnp.mean(x * x, axis=-1, keepdims=True) + EPS) * nw_ref[...]


def _final_norm(x, w):
    rows = x.shape[0]
    tm = min(rows, 512)
    return pl.pallas_call(
        _final_norm_kernel,
        grid=(rows // tm,),
        in_specs=[pl.BlockSpec((tm, D), lambda i: (i, 0)), pl.BlockSpec((1, D), lambda i: (0, 0))],
        out_specs=pl.BlockSpec((tm, D), lambda i: (i, 0)),
        out_shape=jax.ShapeDtypeStruct((rows, D), F32),
        compiler_params=_cparams(1),
        name="final_norm",
    )(x, w.reshape(1, D))


def _mm_kernel(*refs, n_a, term_a, term_t, n_extra, n_out, epilogue):
    sizes = (n_a, n_a, len(term_a), n_extra, n_extra, n_out, n_out)
    groups, pos = [], 0
    for n in sizes:
        groups.append(refs[pos:pos + n])
        pos += n
    a_p, a_s, w_refs, e_p, e_s, o_p, o_s = groups
    w_vals = [_bf(w[...]) for w in w_refs]

    def run(a_refs, e_refs, o_refs):
        a_vals = [_bf(a[...]) for a in a_refs]
        dots = [(_dot_nt if t else _dot)(a_vals[ai], w) for ai, t, w in zip(term_a, term_t, w_vals)]
        outs = epilogue(dots, [e[...] for e in e_refs])
        for o_ref, o in zip(o_refs, outs):
            o_ref[...] = o.astype(o_ref.dtype)

    run(a_p, e_p, o_p)

    @pl.when(pl.program_id(0) == 0)
    def _():
        run(a_s, e_s, o_s)


def _fused_matmul(name, layer, a_p, a_s, terms, extras, epilogue, out_dtypes, n_cols, tm, tn, lhs_buffers=2):
    rows_p, rows_s = a_p[0].shape[0], a_s[0].shape[0]
    nj = n_cols // tn
    grid = (rows_p // tm, nj)
    sj = lambda i, j: jnp.where(i == 0, j, nj - 1)
    args, in_specs = [], []
    for a in a_p:
        args.append(a)
        in_specs.append(pl.BlockSpec((tm, a.shape[1]), lambda i, j: (i, 0),
                                     pipeline_mode=pl.Buffered(lhs_buffers)))
    for a in a_s:
        args.append(a)
        in_specs.append(pl.BlockSpec((rows_s, a.shape[1]), lambda i, j: (0, 0)))
    for ai, w, col0, transposed in terms:
        assert col0 % tn == 0 and w.shape[-1 if transposed else -2] == a_p[ai].shape[1]
        args.append(w)
        if transposed:
            in_specs.append(pl.BlockSpec((None, tn, w.shape[-1]), lambda i, j, cb=col0 // tn: (layer, j + cb, 0)))
        else:
            in_specs.append(pl.BlockSpec((None, w.shape[-2], tn), lambda i, j, cb=col0 // tn: (layer, 0, j + cb)))
    s_args, s_specs = [], []
    for ex in extras:
        if ex[0] == "tile":
            _, arr_p, arr_s, col0 = ex
            assert col0 % tn == 0
            args.append(arr_p)
            in_specs.append(pl.BlockSpec((tm, tn), lambda i, j, cb=col0 // tn: (i, j + cb)))
            s_args.append(arr_s)
            s_specs.append(pl.BlockSpec((rows_s, tn), lambda i, j, cb=col0 // tn: (0, sj(i, j) + cb)))
        else:
            _, mod, chunk = ex
            args.append(mod.reshape(DEPTH, MOD_ROWS, 1, 6 * D))
            in_specs.append(_mod_spec_prompt(layer, chunk, tm, tn))
            s_args.append(mod)
            s_specs.append(_mod_spec_sample(layer, chunk, tn, sj))
    kern = functools.partial(_mm_kernel, n_a=len(a_p), term_a=tuple(t[0] for t in terms),
                             term_t=tuple(t[3] for t in terms), n_extra=len(extras), n_out=len(out_dtypes),
                             epilogue=epilogue)
    res = pl.pallas_call(
        kern,
        grid=grid,
        in_specs=in_specs + s_specs,
        out_specs=([pl.BlockSpec((tm, tn), lambda i, j: (i, j)) for _ in out_dtypes]
                   + [pl.BlockSpec((rows_s, tn), lambda i, j: (0, sj(i, j))) for _ in out_dtypes]),
        out_shape=([jax.ShapeDtypeStruct((rows_p, n_cols), dt) for dt in out_dtypes]
                   + [jax.ShapeDtypeStruct((rows_s, n_cols), dt) for dt in out_dtypes]),
        compiler_params=_cparams(2),
        name=name,
    )(*args, *s_args)
    return res[:len(out_dtypes)], res[len(out_dtypes):]


def _norm_mod(x, nw, sc, sh):
    y = x * lax.rsqrt(jnp.mean(x * x, axis=-1, keepdims=True) + EPS) * nw
    return (y * (1.0 + sc) + sh).astype(BF16)


def _wo_prep_kernel(mp_ref, ms_ref, w_ref, xp_ref, xs_ref, nw_ref, g1p_ref, scp_ref, shp_ref,
                    g1s_ref, scs_ref, shs_ref, x1p_ref, h2p_ref, x1s_ref, h2s_ref):
    w = w_ref[...]
    nw = nw_ref[...]
    x1 = xp_ref[...] + g1p_ref[...] * _dot(mp_ref[...], w)
    x1p_ref[...] = x1
    h2p_ref[...] = _norm_mod(x1, nw, scp_ref[...], shp_ref[...])

    @pl.when(pl.program_id(0) == 0)
    def _():
        x1s = xs_ref[...] + g1s_ref[...] * _dot(ms_ref[...], w)
        x1s_ref[...] = x1s
        h2s_ref[...] = _norm_mod(x1s, nw, scs_ref[...], shs_ref[...])


def _wo_prep(layer, mp, ms, w_o_bf, xp, xs, norm_w, mod):
    tm = 512
    rows_p, rows_s = mp.shape[0], ms.shape[0]
    mod4 = mod.reshape(DEPTH, MOD_ROWS, 1, 6 * D)
    whole = lambda i, j=0: (0, 0)
    pm = lambda c: pl.BlockSpec((None, None, 1, D), lambda i: (layer, DEC_BATCH + i // (SEQ // tm), 0, c))
    sm = lambda c: pl.BlockSpec((None, rows_s, D), lambda i: (layer, 0, c))
    return pl.pallas_call(
        _wo_prep_kernel,
        grid=(rows_p // tm,),
        in_specs=[pl.BlockSpec((tm, D), lambda i: (i, 0)),
                  pl.BlockSpec((rows_s, D), whole),
                  pl.BlockSpec((None, D, D), lambda i: (layer, 0, 0), pipeline_mode=pl.Buffered(1)),
                  pl.BlockSpec((tm, D), lambda i: (i, 0)),
                  pl.BlockSpec((rows_s, D), whole),
                  pl.BlockSpec((None, 1, D), lambda i: (layer, 0, 0)),
                  pm(MOD_G1), pm(MOD_SC2), pm(MOD_SH2), sm(MOD_G1), sm(MOD_SC2), sm(MOD_SH2)],
        out_specs=[pl.BlockSpec((tm, D), lambda i: (i, 0)),
                   pl.BlockSpec((tm, D), lambda i: (i, 0)),
                   pl.BlockSpec((rows_s, D), whole),
                   pl.BlockSpec((rows_s, D), whole)],
        out_shape=[jax.ShapeDtypeStruct((rows_p, D), F32), jax.ShapeDtypeStruct((rows_p, D), BF16),
                   jax.ShapeDtypeStruct((rows_s, D), F32), jax.ShapeDtypeStruct((rows_s, D), BF16)],
        compiler_params=_cparams(1),
        name="w_o_prep",
    )(mp, ms, w_o_bf, xp, xs, norm_w.reshape(DEPTH, 1, D), mod4, mod4, mod4, mod, mod, mod)


def _epi_merge(dots, ex):
    return [_sigmoid(ex[0]) * dots[0] + _sigmoid(ex[1]) * dots[1] + _sigmoid(ex[2]) * dots[2]]


def _epi_residual(dots, ex):
    return [ex[0] + ex[1] * dots[0]]


def _epi_swiglu(dots, ex):
    return [_silu(dots[0]) * dots[1]]


def _log_sigmoid(u):
    return -(jnp.maximum(-u, 0.0) + jnp.log1p(jnp.exp(-jnp.abs(u))))


def _gla_prompt_kernel(q_ref, k_ref, v_ref, ga_ref, lr_ref, wgk_ref, bgk_ref, nw_ref,
                       oa_ref, sfin_ref, st_s, vt_s, o_s, *, tb):
    t = pl.program_id(1)
    nsc = tb // GLA_SC
    nd = GLA_SC // GLA_CHUNK - 1

    @pl.when(t == 0)
    def _():
        st_s[...] = jnp.zeros_like(st_s)

    u = _dot(lr_ref[...], wgk_ref[...]) + bgk_ref[...]
    gk = _log_sigmoid(u) * (1.0 / 16.0)
    sub = lax.broadcasted_iota(jnp.int32, gk.shape, 0) % SUBLANE
    p8 = gk
    for s in (1, 2, 4):
        p8 = p8 + jnp.where(sub >= s, pltpu.roll(p8, s, 0), 0.0)
    nchunk = tb // GLA_CHUNK
    b_parts, bs_parts, tot = [], [], []
    acc = None
    for c in range(nchunk):
        r0 = c * GLA_CHUNK
        lo = p8[r0:r0 + SUBLANE, :]
        hi = p8[r0 + SUBLANE:r0 + GLA_CHUNK, :] + lo[SUBLANE - 1:SUBLANE, :]
        if c % (GLA_SC // GLA_CHUNK) == 0:
            b_parts += [lo, hi]
            bs_parts += [lo, hi]
            acc = hi[SUBLANE - 1:SUBLANE, :]
        else:
            b_parts += [lo, hi]
            bs_parts += [lo + acc, hi + acc]
            acc = acc + hi[SUBLANE - 1:SUBLANE, :]
        tot.append(hi[SUBLANE - 1:SUBLANE, :])
    b = jnp.concatenate(b_parts, axis=0)
    bs = jnp.concatenate(bs_parts, axis=0)

    def per_chunk(vals, shift):
        return jnp.concatenate([jnp.broadcast_to(vals[(c + shift) % nchunk], (GLA_CHUNK, GLA_KEY))
                                for c in range(nchunk)], axis=0)

    blb = per_chunk(tot, 0)
    etot = [jnp.exp(t) for t in tot]

    q = q_ref[...] * (GLA_DK ** -0.5)
    k = k_ref[...]
    qin = _bf(q * jnp.exp(b))
    kout = _bf(k * jnp.exp(-b))
    kd = k * jnp.exp(blb - b)
    qsc = _bf(q * jnp.exp(bs))
    vt_s[...] = v_ref[...].T

    ri = lax.broadcasted_iota(jnp.int32, (tb, tb), 0)
    ci = lax.broadcasted_iota(jnp.int32, (tb, tb), 1)
    delta = jnp.where(ri // GLA_SC == ci // GLA_SC, ri // GLA_CHUNK - ci // GLA_CHUNK, -1)
    m_intra = (delta == 0) & (ci <= ri)
    m_dist = [delta == d + 1 for d in range(nd)]
    kds = [_bf(kd)]
    for d in range(1, nd):
        kd = kd * per_chunk(etot, d)
        kds.append(_bf(kd))
    for h in range(GLA_H):
        ks = slice(h * GLA_DK, (h + 1) * GLA_DK)
        vs = slice(h * GLA_DV, (h + 1) * GLA_DV)
        a = jnp.where(m_intra, _dot_nt(qin[:, ks], kout[:, ks]), 0.0)
        for d in range(nd):
            a = jnp.where(m_dist[d], _dot_nt(qin[:, ks], kds[d][:, ks]), a)
        o_s[:, vs] = _dot(a, v_ref[:, vs])

    for sc in range(nsc):
        rows = slice(sc * GLA_SC, (sc + 1) * GLA_SC)
        last = bs[(sc + 1) * GLA_SC - 1:(sc + 1) * GLA_SC, :]
        k2 = _bf(k[rows, :] * jnp.exp(last - bs[rows, :]))
        elast = jnp.exp(last)
        for h in range(GLA_H):
            ks = slice(h * GLA_DK, (h + 1) * GLA_DK)
            vs = slice(h * GLA_DV, (h + 1) * GLA_DV)
            st = st_s[h]
            o_s[rows, vs] += _dot_nt(qsc[rows, ks], st)
            st_s[h] = st * elast[:, ks] + _dot(vt_s[vs, rows], k2[:, ks])

    nw = nw_ref[...]
    for h in range(GLA_H):
        vs = slice(h * GLA_DV, (h + 1) * GLA_DV)
        o = o_s[:, vs]
        y = o * lax.rsqrt(jnp.mean(o * o, axis=-1, keepdims=True) + EPS) * nw
        oa_ref[:, vs] = (y * _silu(ga_ref[:, vs])).astype(oa_ref.dtype)

    @pl.when(t == pl.num_programs(1) - 1)
    def _():
        for h in range(GLA_H):
            sfin_ref[h] = st_s[h].T


def _gla_prompt(layer, z, lr, wgk_pad, b_gk, gla_norm_w):
    tb = 256
    nt = SEQ // tb
    row = lambda b, t: b * nt + t
    kern = functools.partial(_gla_prompt_kernel, tb=tb)
    return pl.pallas_call(
        kern,
        grid=(BATCH, nt),
        in_specs=[pl.BlockSpec((tb, GLA_KEY), lambda b, t: (row(b, t), Z_QA // GLA_KEY)),
                  pl.BlockSpec((tb, GLA_KEY), lambda b, t: (row(b, t), Z_KA // GLA_KEY)),
                  pl.BlockSpec((tb, GLA_VAL), lambda b, t: (row(b, t), Z_VA // GLA_VAL)),
                  pl.BlockSpec((tb, GLA_VAL), lambda b, t: (row(b, t), Z_GA // GLA_VAL)),
                  pl.BlockSpec((tb, LR_PAD), lambda b, t: (row(b, t), 0)),
                  pl.BlockSpec((None, LR_PAD, GLA_KEY), lambda b, t: (layer, 0, 0)),
                  pl.BlockSpec((None, 1, GLA_KEY), lambda b, t: (layer, 0, 0)),
                  pl.BlockSpec((None, 1, GLA_DV), lambda b, t: (layer, 0, 0))],
        out_specs=[pl.BlockSpec((tb, GLA_VAL), lambda b, t: (row(b, t), 0)),
                   pl.BlockSpec((None, GLA_H, GLA_DK, GLA_DV), lambda b, t: (b, 0, 0, 0))],
        out_shape=[jax.ShapeDtypeStruct((BATCH * SEQ, GLA_VAL), BF16),
                   jax.ShapeDtypeStruct((BATCH, GLA_H, GLA_DK, GLA_DV), F32)],
        scratch_shapes=[pltpu.VMEM((GLA_H, GLA_DV, GLA_DK), F32),
                        pltpu.VMEM((GLA_VAL, tb), F32),
                        pltpu.VMEM((tb, GLA_VAL), F32)],
        compiler_params=_cparams(2),
        name="gla_prompt",
    )(z, z, z, z, lr, wgk_pad, b_gk.reshape(DEPTH, 1, GLA_KEY), gla_norm_w.reshape(DEPTH, 1, GLA_DV))


def _layernorm(x, w, b):
    mu = jnp.mean(x, axis=-1, keepdims=True)
    xc = x - mu
    var = jnp.mean(xc * xc, axis=-1, keepdims=True)
    return xc * lax.rsqrt(var + EPS) * w + b


def _gmlp_prompt_kernel(u_ref, v_ref, ws_ref, bst_ref, nw_ref, nb_ref, ob_ref, *, nsub):
    ri = lax.broadcasted_iota(jnp.int32, (GM_CHUNK, GM_CHUNK), 0)
    ci = lax.broadcasted_iota(jnp.int32, (GM_CHUNK, GM_CHUNK), 1)
    tril = ci <= ri
    for s in range(nsub):
        rs = slice(s * GM_CHUNK, (s + 1) * GM_CHUNK)
        u = _gelu(u_ref[rs, :])
        v = _layernorm(_gelu(v_ref[rs, :]), nw_ref[...], nb_ref[...])
        for g in range(GM_GROUPS):
            cs = slice(g * GM_GW, (g + 1) * GM_GW)
            wm = jnp.where(tril, ws_ref[g], 0.0)
            mixed = _dot(wm, v[:, cs]) + bst_ref[:, g:g + 1]
            ob_ref[rs, cs] = (u[:, cs] * mixed).astype(ob_ref.dtype)


def _gmlp_prompt(layer, z, gm_ws, gm_bs_t, gm_norm_w, gm_norm_b):
    nsub = 4
    tb = nsub * GM_CHUNK
    kern = functools.partial(_gmlp_prompt_kernel, nsub=nsub)
    return pl.pallas_call(
        kern,
        grid=(BATCH * SEQ // tb,),
        in_specs=[pl.BlockSpec((tb, GM_WIDTH), lambda i: (i, Z_UB // GM_WIDTH)),
                  pl.BlockSpec((tb, GM_WIDTH), lambda i: (i, Z_VB // GM_WIDTH)),
                  pl.BlockSpec((None, GM_GROUPS, GM_CHUNK, GM_CHUNK), lambda i: (layer, 0, 0, 0)),
                  pl.BlockSpec((None, GM_CHUNK, GM_GROUPS), lambda i: (layer, 0, 0)),
                  pl.BlockSpec((None, 1, GM_WIDTH), lambda i: (layer, 0, 0)),
                  pl.BlockSpec((None, 1, GM_WIDTH), lambda i: (layer, 0, 0))],
        out_specs=pl.BlockSpec((tb, GM_WIDTH), lambda i: (i, 0)),
        out_shape=jax.ShapeDtypeStruct((BATCH * SEQ, GM_WIDTH), BF16),
        compiler_params=_cparams(1),
        name="gmlp_prompt",
    )(z, z, gm_ws, gm_bs_t, gm_norm_w.reshape(DEPTH, 1, GM_WIDTH), gm_norm_b.reshape(DEPTH, 1, GM_WIDTH))


def _gmlp_sample_kernel(u_ref, v_ref, w0_ref, b0_ref, nw_ref, nb_ref, ob_ref, vn_ref):
    u = _gelu(u_ref[...])
    v = _layernorm(_gelu(v_ref[...]), nw_ref[...], nb_ref[...])
    vn_ref[...] = v
    ob_ref[...] = u * (w0_ref[...] * v + b0_ref[...])


def _gmlp_sample(layer, z, w0_row, b0_row, gm_norm_w, gm_norm_b):
    full = lambda i: (0, 0)
    lrow = lambda i: (layer, 0, 0)
    return pl.pallas_call(
        _gmlp_sample_kernel,
        grid=(1,),
        in_specs=[pl.BlockSpec((DEC_BATCH, GM_WIDTH), lambda i: (0, Z_UB // GM_WIDTH)),
                  pl.BlockSpec((DEC_BATCH, GM_WIDTH), lambda i: (0, Z_VB // GM_WIDTH)),
                  pl.BlockSpec((None, 1, GM_WIDTH), lrow),
                  pl.BlockSpec((None, 1, GM_WIDTH), lrow),
                  pl.BlockSpec((None, 1, GM_WIDTH), lrow),
                  pl.BlockSpec((None, 1, GM_WIDTH), lrow)],
        out_specs=[pl.BlockSpec((DEC_BATCH, GM_WIDTH), full), pl.BlockSpec((DEC_BATCH, GM_WIDTH), full)],
        out_shape=[jax.ShapeDtypeStruct((DEC_BATCH, GM_WIDTH), F32),
                   jax.ShapeDtypeStruct((DEC_BATCH, GM_WIDTH), F32)],
        compiler_params=_cparams(1),
        name="gmlp_sample",
    )(z, z, w0_row, b0_row, gm_norm_w.reshape(DEPTH, 1, GM_WIDTH), gm_norm_b.reshape(DEPTH, 1, GM_WIDTH))


def _alibi_slope(h):
    return float(2.0 ** (-8.0 * (h + 1) / SWA_HQ))


def _swa_lane_halves(x, half):
    lane = lax.broadcasted_iota(jnp.int32, x.shape, 1)
    own = jnp.where((lane >= half * SWA_HD) & (lane < (half + 1) * SWA_HD), x, 0.0)
    other = pltpu.roll(own, SWA_HD, 1)
    return (own, other) if half == 0 else (other, own)


def _swa_prompt_block(n, q_ref, kc_ref, kp_ref, vc_ref, vp_ref, sink_ref, oc_ref, s_s, p_s):
    w = WINDOW
    ri = lax.broadcasted_iota(jnp.int32, (w, 2 * w), 0)
    ci = lax.broadcasted_iota(jnp.int32, (w, 2 * w), 1)
    dist_i = w + ri - ci
    valid = (dist_i >= 0) & (dist_i < w) & ((ci >= w) | (n > 0))
    dist = dist_i.astype(F32)
    kcat = jnp.concatenate([kp_ref[...], kc_ref[...]], axis=0)
    vcat = jnp.concatenate([vp_ref[...], vc_ref[...]], axis=0)
    heads = []
    for kv in range(SWA_HKV):
        t, half = kv // 2, kv % 2
        k_lo, k_hi = _swa_lane_halves(kcat[:, t * LANE:(t + 1) * LANE], half)
        q2 = jnp.concatenate([q_ref[:, 2 * kv * LANE:(2 * kv + 1) * LANE],
                              q_ref[:, (2 * kv + 1) * LANE:(2 * kv + 2) * LANE]], axis=0)
        for par, kk in ((0, k_lo), (1, k_hi)):
            s = _dot_nt(q2, kk) * (SWA_HD ** -0.5)
            for e in range(2):
                h = SWA_G * kv + 2 * e + par
                seg = len(heads)
                heads.append(h)
                s_s[seg * w:(seg + 1) * w, :] = jnp.where(
                    valid, s[e * w:(e + 1) * w, :] - _alibi_slope(h) * dist, NEG_BIG)
    s = s_s[...]
    sink = jnp.concatenate([jnp.broadcast_to(sink_ref[h:h + 1, 0:1], (w, 1)) for h in heads], axis=0)
    m = jnp.maximum(jnp.max(s, axis=-1, keepdims=True), sink)
    p = jnp.exp(s - m)
    inv = 1.0 / (jnp.sum(p, axis=-1, keepdims=True) + jnp.exp(sink - m))
    p_s[...] = (p * inv).astype(p_s.dtype)
    for kv in range(SWA_HKV):
        t, half = kv // 2, kv % 2
        v_lo, v_hi = _swa_lane_halves(vcat[:, t * LANE:(t + 1) * LANE], half)
        r0 = SWA_G * kv * w
        o = _dot(p_s[r0:r0 + 2 * w, :], v_lo) + _dot(p_s[r0 + 2 * w:r0 + 4 * w, :], v_hi)
        oc_ref[:, 2 * kv * LANE:(2 * kv + 1) * LANE] = o[:w].astype(oc_ref.dtype)
        oc_ref[:, (2 * kv + 1) * LANE:(2 * kv + 2) * LANE] = o[w:].astype(oc_ref.dtype)


STATE_SLOTS = 3


def _state_copy(hbm, sbuf, sem, layer, step, rps, to_hbm):
    slot = lax.rem(step, STATE_SLOTS)
    rows = hbm.at[layer, pl.ds(step * rps, rps)]
    if to_hbm:
        return pltpu.make_async_copy(sbuf.at[slot], rows, sem.at[slot])
    return pltpu.make_async_copy(rows, sbuf.at[slot], sem.at[slot])


def _state_ring_begin(g, n_steps, copy_in, copy_out, x_s, oa_s):
    @pl.when(g == 0)
    def _():
        x_s[...] = jnp.zeros_like(x_s)
        oa_s[...] = jnp.zeros_like(oa_s)
        copy_in(0).start()

    @pl.when(g >= STATE_SLOTS - 1)
    def _():
        copy_out(g - (STATE_SLOTS - 1)).wait()

    @pl.when(g + 1 < n_steps)
    def _():
        copy_in(g + 1).start()

    copy_in(g).wait()


def _state_ring_end(g, n_steps, copy_out):
    copy_out(g).start()

    @pl.when(g == n_steps - 1)
    def _():
        for back in range(STATE_SLOTS - 2, -1, -1):
            copy_out(g - back).wait()


def _gla_decode_rows(g, rps, zq_ref, zk_ref, zv_ref, zga_ref, lr_ref, wgk_ref, bgk_ref, nw_ref,
                     oa_ref, sbuf, x_s, oa_s):
    slot = lax.rem(g, STATE_SLOTS)
    sub = lax.rem(g * rps, SUBLANE)
    up = lax.rem(SUBLANE - sub, SUBLANE)
    u = _dot(lr_ref[...], wgk_ref[...]) + bgk_ref[...]
    eg = pltpu.roll(jnp.exp(_log_sigmoid(u) * (1.0 / 16.0)), up, 0)
    zq = pltpu.roll(zq_ref[...], up, 0) * (GLA_DK ** -0.5)
    zk = pltpu.roll(zk_ref[...], up, 0)
    zv = pltpu.roll(zv_ref[...], up, 0)
    gate = _silu(pltpu.roll(zga_ref[...], up, 0))
    nw = nw_ref[...]
    row8 = lax.broadcasted_iota(jnp.int32, (SUBLANE, GLA_DV), 0)
    for h in range(GLA_H):
        ks = slice(h * GLA_DK, (h + 1) * GLA_DK)
        vs = slice(h * GLA_DV, (h + 1) * GLA_DV)
        x_s[0:SUBLANE, :] = zq[:, ks]
        x_s[SUBLANE:2 * SUBLANE, :] = zk[:, ks]
        x_s[2 * SUBLANE:3 * SUBLANE, :] = eg[:, ks]
        xt = x_s[...].T
        y8 = jnp.zeros((SUBLANE, GLA_DV), F32)
        for r in range(rps):
            qc = xt[:, r:r + 1]
            kc = xt[:, SUBLANE + r:SUBLANE + r + 1]
            gc = xt[:, 2 * SUBLANE + r:2 * SUBLANE + r + 1]
            s_new = gc * sbuf[slot, r, h] + kc * zv[r:r + 1, vs]
            sbuf[slot, r, h] = s_new
            o = jnp.sum(qc * s_new, axis=0, keepdims=True)
            y = o * lax.rsqrt(jnp.mean(o * o, axis=-1, keepdims=True) + EPS) * nw * gate[r:r + 1, vs]
            y8 = jnp.where(row8 == r, y, y8)
        y8 = pltpu.roll(y8, sub, 0)
        acc = jnp.where(sub == 0, y8, oa_s[:, vs] + y8)
        oa_s[:, vs] = acc
        oa_ref[:, vs] = acc


def _swa_gla_kernel(q_ref, kc_ref, kp_ref, vc_ref, vp_ref, sink_ref,
                    zq_ref, zk_ref, zv_ref, zga_ref, lr_ref, wgk_ref, bgk_ref, nw_ref, sin_hbm, *rest,
                    layer, rps, aliased):
    if aliased:
        rest = rest[1:]
    oc_ref, oa_ref, sout_hbm, s_s, p_s, sbuf, x_s, oa_s, in_sem, out_sem = rest
    n = pl.program_id(1)
    g = pl.program_id(0) * pl.num_programs(1) + n
    n_steps = pl.num_programs(0) * pl.num_programs(1)
    copy_in = functools.partial(_state_copy, sin_hbm, sbuf, in_sem, layer, rps=rps, to_hbm=False)
    copy_out = functools.partial(_state_copy, sout_hbm, sbuf, out_sem, layer, rps=rps, to_hbm=True)
    _state_ring_begin(g, n_steps, copy_in, copy_out, x_s, oa_s)
    _gla_decode_rows(g, rps, zq_ref, zk_ref, zv_ref, zga_ref, lr_ref, wgk_ref, bgk_ref, nw_ref,
                     oa_ref, sbuf, x_s, oa_s)
    _swa_prompt_block(n, q_ref, kc_ref, kp_ref, vc_ref, vp_ref, sink_ref, oc_ref, s_s, p_s)
    _state_ring_end(g, n_steps, copy_out)


def _swa_prompt_gla_sample(layer, zp, sinks_b, zs, lrs, wgk_pad, b_gk, gla_norm_w, state_gla, state_out):
    nb = SEQ // WINDOW
    n_steps = BATCH * nb
    rps = DEC_BATCH // n_steps
    assert rps * n_steps == DEC_BATCH and SUBLANE % rps == 0 and n_steps >= STATE_SLOTS
    row = lambda b, n: b * nb + n
    prev = lambda b, n: b * nb + jnp.maximum(n - 1, 0)
    srow = lambda b, n: (row(b, n) * rps) // SUBLANE
    in_specs = [pl.BlockSpec((WINDOW, SWA_Q), lambda b, n: (row(b, n), Z_QC // SWA_Q)),
                pl.BlockSpec((WINDOW, SWA_KV), lambda b, n: (row(b, n), Z_KC // SWA_KV)),
                pl.BlockSpec((WINDOW, SWA_KV), lambda b, n: (prev(b, n), Z_KC // SWA_KV)),
                pl.BlockSpec((WINDOW, SWA_KV), lambda b, n: (row(b, n), Z_VC // SWA_KV)),
                pl.BlockSpec((WINDOW, SWA_KV), lambda b, n: (prev(b, n), Z_VC // SWA_KV)),
                pl.BlockSpec((None, SWA_HQ, LANE), lambda b, n: (layer, 0, 0)),
                pl.BlockSpec((SUBLANE, GLA_KEY), lambda b, n: (srow(b, n), Z_QA // GLA_KEY)),
                pl.BlockSpec((SUBLANE, GLA_KEY), lambda b, n: (srow(b, n), Z_KA // GLA_KEY)),
                pl.BlockSpec((SUBLANE, GLA_VAL), lambda b, n: (srow(b, n), Z_VA // GLA_VAL)),
                pl.BlockSpec((SUBLANE, GLA_VAL), lambda b, n: (srow(b, n), Z_GA // GLA_VAL)),
                pl.BlockSpec((SUBLANE, LR_PAD), lambda b, n: (srow(b, n), 0)),
                pl.BlockSpec((None, LR_PAD, GLA_KEY), lambda b, n: (layer, 0, 0)),
                pl.BlockSpec((None, 1, GLA_KEY), lambda b, n: (layer, 0, 0)),
                pl.BlockSpec((None, 1, GLA_DV), lambda b, n: (layer, 0, 0)),
                pl.BlockSpec(memory_space=pl.ANY)]
    args = [zp, zp, zp, zp, zp, sinks_b, zs, zs, zs, zs, lrs, wgk_pad, b_gk.reshape(DEPTH, 1, GLA_KEY),
            gla_norm_w.reshape(DEPTH, 1, GLA_DV), state_gla]
    aliases = {}
    if state_out is not None:
        in_specs.append(pl.BlockSpec(memory_space=pl.ANY))
        args.append(state_out)
        aliases = {len(args) - 1: 2}
    kern = functools.partial(_swa_gla_kernel, layer=layer, rps=rps, aliased=state_out is not None)
    return pl.pallas_call(
        kern,
        grid=(BATCH, nb),
        in_specs=in_specs,
        out_specs=[pl.BlockSpec((WINDOW, SWA_Q), lambda b, n: (row(b, n), 0)),
                   pl.BlockSpec((SUBLANE, GLA_VAL), lambda b, n: (srow(b, n), 0)),
                   pl.BlockSpec(memory_space=pl.ANY)],
        out_shape=[jax.ShapeDtypeStruct((BATCH * SEQ, SWA_Q), BF16),
                   jax.ShapeDtypeStruct((DEC_BATCH, GLA_VAL), F32),
                   jax.ShapeDtypeStruct((DEPTH, DEC_BATCH, GLA_H, GLA_DK, GLA_DV), F32)],
        scratch_shapes=[pltpu.VMEM((SWA_HQ * WINDOW, 2 * WINDOW), F32),
                        pltpu.VMEM((SWA_HQ * WINDOW, 2 * WINDOW), BF16),
                        pltpu.VMEM((STATE_SLOTS, rps, GLA_H, GLA_DK, GLA_DV), F32),
                        pltpu.VMEM((LANE, GLA_DK), F32),
                        pltpu.VMEM((SUBLANE, GLA_VAL), F32),
                        pltpu.SemaphoreType.DMA((STATE_SLOTS,)),
                        pltpu.SemaphoreType.DMA((STATE_SLOTS,))],
        input_output_aliases=aliases,
        compiler_params=_cparams(2),
        name="swa_prompt_gla_sample",
    )(*args)


def _swa_sample_kernel(q_ref, kn_ref, vn_ref, kt_ref, vt_ref, sink_ref, slope_ref, o_ref, *, rb):
    wb = WINDOW
    j = lax.broadcasted_iota(jnp.int32, (SWA_HQ, wb), 1)
    dist = (wb - j).astype(F32)
    ok = j >= 1
    grp = lax.broadcasted_iota(jnp.int32, (SWA_HQ, 1), 0) // SWA_G
    slope = slope_ref[:, 0:1]
    sink = sink_ref[:, 0:1]

    def per_head(rows):
        out = jnp.broadcast_to(rows[0:1, :], (SWA_HQ, SWA_HD))
        for kv in range(1, SWA_HKV):
            out = jnp.where(grp == kv, rows[kv:kv + 1, :], out)
        return out

    s_rows, self_rows = [], []
    for r in range(rb):
        q = q_ref[r]
        s = _dot(q, kt_ref[r, 0])
        for kv in range(1, SWA_HKV):
            s = jnp.where(grp == kv, _dot(q, kt_ref[r, kv]), s)
        s_rows.append(jnp.where(ok, s * (SWA_HD ** -0.5) - slope * dist, NEG_BIG))
        self_rows.append(jnp.sum(_bf(q).astype(F32) * _bf(per_head(kn_ref[r])).astype(F32), axis=-1,
                                 keepdims=True) * (SWA_HD ** -0.5))
    s = jnp.concatenate(s_rows, axis=0)
    s_self = jnp.concatenate(self_rows, axis=0)
    sink = jnp.concatenate([sink] * rb, axis=0)
    m = jnp.maximum(jnp.maximum(jnp.max(s, axis=-1, keepdims=True), s_self), sink)
    p = jnp.exp(s - m)
    p_self = jnp.exp(s_self - m)
    inv = 1.0 / (jnp.sum(p, axis=-1, keepdims=True) + p_self + jnp.exp(sink - m))
    pn = _bf(p * inv)
    pn_self = _bf(p_self * inv).astype(F32)
    for r in range(rb):
        rows = slice(r * SWA_HQ, (r + 1) * SWA_HQ)
        o = _dot_nt(pn[rows, :], vt_ref[r, 0])
        for kv in range(1, SWA_HKV):
            o = jnp.where(grp == kv, _dot_nt(pn[rows, :], vt_ref[r, kv]), o)
        o_ref[r] = o + pn_self[rows, :] * _bf(per_head(vn_ref[r])).astype(F32)


def _swa_sample(layer, q3, kn3, vn3, cache_kt, cache_vt, sinks_b, slopes_b):
    rb = 8
    kern = functools.partial(_swa_sample_kernel, rb=rb)
    return pl.pallas_call(
        kern,
        grid=(DEC_BATCH // rb,),
        in_specs=[pl.BlockSpec((rb, SWA_HQ, SWA_HD), lambda i: (i, 0, 0)),
                  pl.BlockSpec((rb, SWA_HKV, SWA_HD), lambda i: (i, 0, 0)),
                  pl.BlockSpec((rb, SWA_HKV, SWA_HD), lambda i: (i, 0, 0)),
                  pl.BlockSpec((None, rb, SWA_HKV, SWA_HD, WINDOW), lambda i: (layer, i, 0, 0, 0)),
                  pl.BlockSpec((None, rb, SWA_HKV, SWA_HD, WINDOW), lambda i: (layer, i, 0, 0, 0)),
                  pl.BlockSpec((None, SWA_HQ, LANE), lambda i: (layer, 0, 0)),
                  pl.BlockSpec((SWA_HQ, LANE), lambda i: (0, 0))],
        out_specs=pl.BlockSpec((rb, SWA_HQ, SWA_HD), lambda i: (i, 0, 0)),
        out_shape=jax.ShapeDtypeStruct((DEC_BATCH, SWA_HQ, SWA_HD), F32),
        compiler_params=_cparams(1),
        name="swa_sample",
    )(q3, kn3, vn3, cache_kt, cache_vt, sinks_b, slopes_b)


def _w_in_kernel(hp_ref, hs_ref, wa_ref, wb_ref, zp_ref, zs_ref, *, n_plain):
    j = pl.program_id(1)
    a = wa_ref[...]
    shifted = jnp.concatenate([a[GLA_RANK:, :], wb_ref[...]], axis=0)
    w = _bf(jnp.where(j >= n_plain, shifted, a))
    zp_ref[...] = _dot_nt(hp_ref[...], w)

    @pl.when(pl.program_id(0) == 0)
    def _():
        zs_ref[...] = _dot_nt(hs_ref[...], w)


def _w_in(layer, hp, hs, w_in_t):
    tm, tn = SEQ, 512
    assert LR_COL % tn == 0 and Z_WIDTH % tn == 0 and tn % GLA_RANK == 0
    rows_p, rows_s = hp.shape[0], hs.shape[0]
    nj = Z_WIDTH // tn
    sj = lambda i, j: jnp.where(i == 0, j, nj - 1)
    return pl.pallas_call(
        functools.partial(_w_in_kernel, n_plain=LR_COL // tn),
        grid=(rows_p // tm, nj),
        in_specs=[pl.BlockSpec((tm, D), lambda i, j: (i, 0), pipeline_mode=pl.Buffered(1)),
                  pl.BlockSpec((rows_s, D), lambda i, j: (0, 0)),
                  pl.BlockSpec((None, tn, D), lambda i, j: (layer, j, 0)),
                  pl.BlockSpec((None, GLA_RANK, D), lambda i, j: (layer, (j + 1) * (tn // GLA_RANK), 0))],
        out_specs=[pl.BlockSpec((tm, tn), lambda i, j: (i, j)),
                   pl.BlockSpec((rows_s, tn), lambda i, j: (0, sj(i, j)))],
        out_shape=[jax.ShapeDtypeStruct((rows_p, Z_WIDTH), F32), jax.ShapeDtypeStruct((rows_s, Z_WIDTH), F32)],
        compiler_params=_cparams(2),
        name="w_in",
    )(hp, hs, w_in_t, w_in_t)


def _layer(layer, xp, xs, mod, p, state_gla, cache_k, cache_v, state_out):
    hp, lrp = _prep(layer, xp, p["norm1_w"], mod, MOD_SC1, MOD_SH1, False, p["w_lr_t"])
    hs, lrs = _prep(layer, xs, p["norm1_w"], mod, MOD_SC1, MOD_SH1, True, p["w_lr_t"])
    zp, zs = _w_in(layer, hp, hs, p["w_in_t"])
    oa_p, s_p = _gla_prompt(layer, zp, lrp, p["wgk_pad"], p["b_gk"], p["gla_norm_w"])
    ob_p = _gmlp_prompt(layer, zp, p["gm_ws"], p["gm_bs_t"], p["gm_norm_w"], p["gm_norm_b"])
    oc_p, oa_s, state_out = _swa_prompt_gla_sample(layer, zp, p["sinks_b"], zs, lrs, p["wgk_pad"], p["b_gk"],
                                                   p["gla_norm_w"], state_gla, state_out)
    z4 = zp.reshape(BATCH, SEQ, Z_WIDTH)
    kp_rows = z4[:, SEQ - WINDOW:, Z_KC:Z_KC + SWA_KV].reshape(BATCH, WINDOW, SWA_HKV, SWA_HD)
    vp_rows = z4[:, SEQ - WINDOW:, Z_VC:Z_VC + SWA_KV].reshape(BATCH, WINDOW, SWA_HKV, SWA_HD)
    ob_s, v_gm = _gmlp_sample(layer, zs, p["gm_w0"], p["gm_b0"], p["gm_norm_w"], p["gm_norm_b"])
    q3 = zs[:, Z_QC:Z_QC + SWA_Q].reshape(DEC_BATCH, SWA_HQ, SWA_HD)
    kn3 = zs[:, Z_KC:Z_KC + SWA_KV].reshape(DEC_BATCH, SWA_HKV, SWA_HD)
    vn3 = zs[:, Z_VC:Z_VC + SWA_KV].reshape(DEC_BATCH, SWA_HKV, SWA_HD)
    oc_s = _swa_sample(layer, q3, kn3, vn3, cache_k, cache_v, p["sinks_b"], p["slopes_b"]).reshape(DEC_BATCH, SWA_Q)
    ks_rows = kn3.reshape(DEC_BATCH, 1, SWA_HKV, SWA_HD)
    vs_rows = vn3.reshape(DEC_BATCH, 1, SWA_HKV, SWA_HD)
    (mp,), (ms,) = _fused_matmul(
        "merge", layer, [oa_p, ob_p, oc_p], [oa_s, ob_s, oc_s],
        [(0, p["w_pa"], 0, False), (1, p["w_pb"], 0, False), (2, p["w_pc"], 0, False)],
        [("tile", zp, zs, Z_GATES), ("tile", zp, zs, Z_GATES + D), ("tile", zp, zs, Z_GATES + 2 * D)],
        _epi_merge, [BF16], D, 1024, 512)
    x1p, h2p, x1s, h2s = _wo_prep(layer, mp, ms, p["w_o_bf"], xp, xs, p["norm2_w"], mod)
    (hidp,), (hids,) = _fused_matmul(
        "ffn_in", layer, [h2p], [h2s], [(0, p["w_ffn_in"], 0, False), (0, p["w_ffn_in"], FFN_HIDDEN, False)],
        [], _epi_swiglu, [BF16], FFN_HIDDEN, 2048, 512)
    (x2p,), (x2s,) = _fused_matmul("ffn_out", layer, [hidp], [hids], [(0, p["w_ffn_out"], 0, False)],
                                   [("tile", x1p, x1s, 0), ("mod", mod, MOD_G2)], _epi_residual, [F32],
                                   D, 1024, 512)
    return x2p, x2s, s_p, state_out, kp_rows, vp_rows, ks_rows, vs_rows, v_gm


def kernel(x_prompt, x_sample, c_prompt, c_sample, state_gla, cache_swa_k, cache_swa_v, w_ada, b_ada, norm1_w,
           norm2_w, w_in, w_gk2, b_gk, gla_norm_w, gm_norm_w, gm_norm_b, gm_ws, gm_bs, swa_sinks, w_pa, w_pb,
           w_pc, w_o, w_ffn_in, w_ffn_out, final_norm_w):
    w_in_t = jnp.swapaxes(w_in, 1, 2)
    w_lr_t = jnp.pad(w_in_t[:, LR_COL:LR_COL + GLA_RANK, :], ((0, 0), (0, LR_PAD - GLA_RANK), (0, 0))).astype(BF16)
    p = {
        "norm1_w": norm1_w, "norm2_w": norm2_w, "w_in_t": w_in_t, "w_lr_t": w_lr_t,
        "wgk_pad": jnp.pad(w_gk2, ((0, 0), (0, LR_PAD - GLA_RANK), (0, 0))),
        "b_gk": b_gk, "gla_norm_w": gla_norm_w, "gm_norm_w": gm_norm_w, "gm_norm_b": gm_norm_b,
        "gm_ws": gm_ws, "gm_bs_t": jnp.swapaxes(gm_bs, 1, 2),
        "gm_w0": jnp.repeat(gm_ws[:, :, 0, 0], GM_GW, axis=1).reshape(DEPTH, 1, GM_WIDTH),
        "gm_b0": jnp.repeat(gm_bs[:, :, 0], GM_GW, axis=1).reshape(DEPTH, 1, GM_WIDTH),
        "sinks_b": jnp.broadcast_to(swa_sinks[:, :, None], (DEPTH, SWA_HQ, LANE)),
        "slopes_b": jnp.broadcast_to(
            jnp.asarray([_alibi_slope(h) for h in range(SWA_HQ)], F32)[:, None], (SWA_HQ, LANE)),
        "w_pa": w_pa, "w_pb": w_pb, "w_pc": w_pc, "w_o_bf": w_o.astype(BF16), "w_ffn_in": w_ffn_in,
        "w_ffn_out": w_ffn_out.astype(BF16),
    }
    c_all = jnp.concatenate([c_sample, c_prompt, jnp.zeros((MOD_ROWS - DEC_BATCH - BATCH, D), F32)], axis=0)
    mod = _ada(c_all, w_ada, b_ada)

    xp = x_prompt.reshape(BATCH * SEQ, D)
    xs = x_sample.reshape(DEC_BATCH, D)
    cache_k = jnp.transpose(cache_swa_k, (0, 1, 3, 4, 2))
    cache_v = jnp.transpose(cache_swa_v, (0, 1, 3, 4, 2))
    gla_p, kp, vp, ksm, vsm, gmv = [], [], [], [], [], []
    state_out = None
    for l in range(DEPTH):
        xp, xs, s_p, state_out, k_p, v_p, k_s, v_s, gv = _layer(l, xp, xs, mod, p, state_gla, cache_k, cache_v,
                                                                state_out)
        gla_p.append(s_p)
        kp.append(k_p)
        vp.append(v_p)
        ksm.append(k_s)
        vsm.append(v_s)
        gmv.append(gv.reshape(DEC_BATCH, 1, GM_WIDTH))
    y_prompt = _final_norm(xp, final_norm_w).reshape(BATCH, SEQ, D)
    y_sample = _final_norm(xs, final_norm_w).reshape(DEC_BATCH, 1, D)
    return (y_prompt, y_sample, jnp.stack(gla_p), state_out, jnp.stack(kp), jnp.stack(vp),
            jnp.stack(ksm), jnp.stack(vsm), jnp.stack(gmv))
```

```python
import functools

import jax
import jax.numpy as jnp
import numpy as np
from jax import lax
from jax.experimental import pallas as pl
from jax.experimental.pallas import tpu as pltpu

F32 = jnp.float32
BF16 = jnp.bfloat16

D = 2048
BATCH, SEQ = 2, 4096
DEPTH = 2
DEC_BATCH = 128
GLA_H, GLA_DK, GLA_DV = 4, 256, 512
GLA_KEY, GLA_VAL = GLA_H * GLA_DK, GLA_H * GLA_DV
GLA_RANK = 16
GLA_CHUNK = 16
GLA_SC = 128
GM_WIDTH, GM_GROUPS, GM_CHUNK = 1024, 4, 128
GM_GW = GM_WIDTH // GM_GROUPS
SWA_HQ, SWA_HKV, SWA_HD, WINDOW = 16, 4, 64, 128
SWA_G = SWA_HQ // SWA_HKV
SWA_Q, SWA_KV = SWA_HQ * SWA_HD, SWA_HKV * SWA_HD
FFN_HIDDEN = 5632
EPS = 1e-6
NEG_BIG = -1e30

Z_QA, Z_KA, Z_VA, Z_GA = 0, 1024, 2048, 4096
Z_UB, Z_VB = 6144, 7168
Z_QC, Z_KC, Z_VC = 8192, 9216, 9472
Z_GATES = 9728
Z_WIDTH = 15872
LR_COL = 6144
LANE = 128
SUBLANE = 8
LR_PAD = LANE

MOD_SH1, MOD_SC1, MOD_G1, MOD_SH2, MOD_SC2, MOD_G2 = range(6)
MOD_ROWS = DEC_BATCH + 8

VMEM_LIMIT = 58 * 1024 * 1024


def _cparams(n_axes):
    return pltpu.CompilerParams(dimension_semantics=("arbitrary",) * n_axes,
                                vmem_limit_bytes=VMEM_LIMIT)


def _bf(x):
    return x if x.dtype == BF16 else x.astype(BF16)


def _dot(a, b):
    return jnp.dot(_bf(a), _bf(b), preferred_element_type=F32)


def _dot_nt(a, b):
    return lax.dot_general(_bf(a), _bf(b), (((1,), (1,)), ((), ())), preferred_element_type=F32)


def _silu(x):
    return x * (1.0 / (1.0 + jnp.exp(-x)))


def _sigmoid(x):
    return 1.0 / (1.0 + jnp.exp(-x))


def _gelu(x):
    return 0.5 * x * (1.0 + jnp.tanh(np.sqrt(2.0 / np.pi).astype(np.float32) * (x + 0.044715 * (x * x * x))))


def _mod_spec_prompt(layer, chunk, tm, tn):
    cb, bpb = chunk * D // tn, SEQ // tm
    return pl.BlockSpec((None, None, 1, tn), lambda i, j: (layer, DEC_BATCH + i // bpb, 0, j + cb))


def _mod_spec_sample(layer, chunk, tn, jmap):
    cb = chunk * D // tn
    return pl.BlockSpec((None, DEC_BATCH, tn), lambda i, j: (layer, 0, jmap(i, j) + cb))


def _ada_kernel(c_ref, w_ref, b_ref, o_ref):
    o_ref[...] = _dot(_silu(c_ref[...]), w_ref[...]) + b_ref[...]


def _ada(c_all, w_ada, b_ada):
    tn = 1024
    return pl.pallas_call(
        _ada_kernel,
        grid=(DEPTH, 6 * D // tn),
        in_specs=[pl.BlockSpec((MOD_ROWS, D), lambda l, j: (0, 0)),
                  pl.BlockSpec((None, D, tn), lambda l, j: (l, 0, j)),
                  pl.BlockSpec((None, 1, tn), lambda l, j: (l, 0, j))],
        out_specs=pl.BlockSpec((None, MOD_ROWS, tn), lambda l, j: (l, 0, j)),
        out_shape=jax.ShapeDtypeStruct((DEPTH, MOD_ROWS, 6 * D), F32),
        compiler_params=_cparams(2),
        name="ada",
    )(c_all, w_ada, b_ada.reshape(DEPTH, 1, 6 * D))


def _prep_kernel(x_ref, nw_ref, sc_ref, sh_ref, *rest):
    x = x_ref[...]
    y = x * lax.rsqrt(jnp.mean(x * x, axis=-1, keepdims=True) + EPS) * nw_ref[...]
    h = (y * (1.0 + sc_ref[...]) + sh_ref[...]).astype(BF16)
    if len(rest) == 1:
        (o_ref,) = rest
    else:
        wlr_ref, o_ref, lr_ref = rest
        lr_ref[...] = _dot_nt(h, wlr_ref[...])
    o_ref[...] = h


def _prep(layer, x, norm_w, mod, sc_chunk, sh_chunk, sample, w_lr_t=None):
    rows = x.shape[0]
    if sample:
        tm, modop = rows, mod
        mod_specs = [_mod_spec_sample(layer, c, D, lambda i, j: j) for c in (sc_chunk, sh_chunk)]
    else:
        tm, modop = 1024, mod.reshape(DEPTH, MOD_ROWS, 1, 6 * D)
        mod_specs = [_mod_spec_prompt(layer, c, tm, D) for c in (sc_chunk, sh_chunk)]
    args = [x, norm_w.reshape(DEPTH, 1, D), modop, modop]
    in_specs = [pl.BlockSpec((tm, D), lambda i, j: (i, 0)),
                pl.BlockSpec((None, 1, D), lambda i, j: (layer, 0, 0))] + mod_specs
    out_specs = [pl.BlockSpec((tm, D), lambda i, j: (i, 0))]
    out_shape = [jax.ShapeDtypeStruct((rows, D), BF16)]
    if w_lr_t is not None:
        args.append(w_lr_t)
        in_specs.append(pl.BlockSpec((None, LR_PAD, D), lambda i, j: (layer, 0, 0)))
        out_specs.append(pl.BlockSpec((tm, LR_PAD), lambda i, j: (i, 0)))
        out_shape.append(jax.ShapeDtypeStruct((rows, LR_PAD), F32))
    res = pl.pallas_call(
        _prep_kernel,
        grid=(rows // tm, 1),
        in_specs=in_specs,
        out_specs=out_specs,
        out_shape=out_shape,
        compiler_params=_cparams(2),
        name="prep",
    )(*args)
    return res if w_lr_t is not None else res[0]


def _final_norm_kernel(x_ref, nw_ref, o_ref):
    x = x_ref[...]
    o_ref[...] = x * lax.rsqrt(jnp.mean(x * x, axis=-1, keepdims=True) + EPS) * nw_ref[...]


def _final_norm(x, w):
    rows = x.shape[0]
    tm = min(rows, 512)
    return pl.pallas_call(
        _final_norm_kernel,
        grid=(rows // tm,),
        in_specs=[pl.BlockSpec((tm, D), lambda i: (i, 0)), pl.BlockSpec((1, D), lambda i: (0, 0))],
        out_specs=pl.BlockSpec((tm, D), lambda i: (i, 0)),
        out_shape=jax.ShapeDtypeStruct((rows, D), F32),
        compiler_params=_cparams(1),
        name="final_norm",
    )(x, w.reshape(1, D))


def _mm_kernel(*refs, n_a, term_a, n_extra, n_out, epilogue):
    sizes = (n_a, n_a, len(term_a), n_extra, n_extra, n_out, n_out)
    groups, pos = [], 0
    for n in sizes:
        groups.append(refs[pos:pos + n])
        pos += n
    a_p, a_s, w_refs, e_p, e_s, o_p, o_s = groups
    w_vals = [_bf(w[...]) for w in w_refs]

    def run(a_refs, e_refs, o_refs):
        a_vals = [_bf(a[...]) for a in a_refs]
        dots = [_dot(a_vals[ai], w) for ai, w in zip(term_a, w_vals)]
        outs = epilogue(dots, [e[...] for e in e_refs])
        for o_ref, o in zip(o_refs, outs):
            o_ref[...] = o.astype(o_ref.dtype)

    run(a_p, e_p, o_p)

    @pl.when(pl.program_id(0) == 0)
    def _():
        run(a_s, e_s, o_s)


def _fused_matmul(name, layer, a_p, a_s, terms, extras, epilogue, out_dtypes, n_cols, tm, tn, lhs_buffers=2):
    rows_p, rows_s = a_p[0].shape[0], a_s[0].shape[0]
    nj = n_cols // tn
    grid = (rows_p // tm, nj)
    sj = lambda i, j: jnp.where(i == 0, j, nj - 1)
    args, in_specs = [], []
    for a in a_p:
        args.append(a)
        in_specs.append(pl.BlockSpec((tm, a.shape[1]), lambda i, j: (i, 0),
                                     pipeline_mode=pl.Buffered(lhs_buffers)))
    for a in a_s:
        args.append(a)
        in_specs.append(pl.BlockSpec((rows_s, a.shape[1]), lambda i, j: (0, 0)))
    for ai, w, col0 in terms:
        assert col0 % tn == 0 and w.shape[-2] == a_p[ai].shape[1]
        args.append(w)
        in_specs.append(pl.BlockSpec((None, w.shape[-2], tn), lambda i, j, cb=col0 // tn: (layer, 0, j + cb)))
    s_args, s_specs = [], []
    for ex in extras:
        if ex[0] == "tile":
            _, arr_p, arr_s, col0 = ex
            assert col0 % tn == 0
            args.append(arr_p)
            in_specs.append(pl.BlockSpec((tm, tn), lambda i, j, cb=col0 // tn: (i, j + cb)))
            s_args.append(arr_s)
            s_specs.append(pl.BlockSpec((rows_s, tn), lambda i, j, cb=col0 // tn: (0, sj(i, j) + cb)))
        else:
            _, mod, chunk = ex
            args.append(mod.reshape(DEPTH, MOD_ROWS, 1, 6 * D))
            in_specs.append(_mod_spec_prompt(layer, chunk, tm, tn))
            s_args.append(mod)
            s_specs.append(_mod_spec_sample(layer, chunk, tn, sj))
    kern = functools.partial(_mm_kernel, n_a=len(a_p), term_a=tuple(t[0] for t in terms),
                             n_extra=len(extras), n_out=len(out_dtypes), epilogue=epilogue)
    res = pl.pallas_call(
        kern,
        grid=grid,
        in_specs=in_specs + s_specs,
        out_specs=([pl.BlockSpec((tm, tn), lambda i, j: (i, j)) for _ in out_dtypes]
                   + [pl.BlockSpec((rows_s, tn), lambda i, j: (0, sj(i, j))) for _ in out_dtypes]),
        out_shape=([jax.ShapeDtypeStruct((rows_p, n_cols), dt) for dt in out_dtypes]
                   + [jax.ShapeDtypeStruct((rows_s, n_cols), dt) for dt in out_dtypes]),
        compiler_params=_cparams(2),
        name=name,
    )(*args, *s_args)
    return res[:len(out_dtypes)], res[len(out_dtypes):]


def _norm_mod(x, nw, sc, sh):
    y = x * lax.rsqrt(jnp.mean(x * x, axis=-1, keepdims=True) + EPS) * nw
    return (y * (1.0 + sc) + sh).astype(BF16)


def _wo_prep_kernel(mp_ref, ms_ref, w_ref, xp_ref, xs_ref, nw_ref, g1p_ref, scp_ref, shp_ref,
                    g1s_ref, scs_ref, shs_ref, x1p_ref, h2p_ref, x1s_ref, h2s_ref):
    w = w_ref[...]
    nw = nw_ref[...]
    x1 = xp_ref[...] + g1p_ref[...] * _dot(mp_ref[...], w)
    x1p_ref[...] = x1
    h2p_ref[...] = _norm_mod(x1, nw, scp_ref[...], shp_ref[...])

    @pl.when(pl.program_id(0) == 0)
    def _():
        x1s = xs_ref[...] + g1s_ref[...] * _dot(ms_ref[...], w)
        x1s_ref[...] = x1s
        h2s_ref[...] = _norm_mod(x1s, nw, scs_ref[...], shs_ref[...])


def _wo_prep(layer, mp, ms, w_o_bf, xp, xs, norm_w, mod):
    tm = 512
    rows_p, rows_s = mp.shape[0], ms.shape[0]
    mod4 = mod.reshape(DEPTH, MOD_ROWS, 1, 6 * D)
    whole = lambda i, j=0: (0, 0)
    pm = lambda c: pl.BlockSpec((None, None, 1, D), lambda i: (layer, DEC_BATCH + i // (SEQ // tm), 0, c))
    sm = lambda c: pl.BlockSpec((None, rows_s, D), lambda i: (layer, 0, c))
    return pl.pallas_call(
        _wo_prep_kernel,
        grid=(rows_p // tm,),
        in_specs=[pl.BlockSpec((tm, D), lambda i: (i, 0)),
                  pl.BlockSpec((rows_s, D), whole),
                  pl.BlockSpec((None, D, D), lambda i: (layer, 0, 0), pipeline_mode=pl.Buffered(1)),
                  pl.BlockSpec((tm, D), lambda i: (i, 0)),
                  pl.BlockSpec((rows_s, D), whole),
                  pl.BlockSpec((None, 1, D), lambda i: (layer, 0, 0)),
                  pm(MOD_G1), pm(MOD_SC2), pm(MOD_SH2), sm(MOD_G1), sm(MOD_SC2), sm(MOD_SH2)],
        out_specs=[pl.BlockSpec((tm, D), lambda i: (i, 0)),
                   pl.BlockSpec((tm, D), lambda i: (i, 0)),
                   pl.BlockSpec((rows_s, D), whole),
                   pl.BlockSpec((rows_s, D), whole)],
        out_shape=[jax.ShapeDtypeStruct((rows_p, D), F32), jax.ShapeDtypeStruct((rows_p, D), BF16),
                   jax.ShapeDtypeStruct((rows_s, D), F32), jax.ShapeDtypeStruct((rows_s, D), BF16)],
        compiler_params=_cparams(1),
        name="w_o_prep",
    )(mp, ms, w_o_bf, xp, xs, norm_w.reshape(DEPTH, 1, D), mod4, mod4, mod4, mod, mod, mod)


def _epi_merge(dots, ex):
    return [_sigmoid(ex[0]) * dots[0] + _sigmoid(ex[1]) * dots[1] + _sigmoid(ex[2]) * dots[2]]


def _epi_residual(dots, ex):
    return [ex[0] + ex[1] * dots[0]]


def _epi_swiglu(dots, ex):
    return [_silu(dots[0]) * dots[1]]


def _log_sigmoid(u):
    return -(jnp.maximum(-u, 0.0) + jnp.log1p(jnp.exp(-jnp.abs(u))))


def _gla_prompt_kernel(q_ref, k_ref, v_ref, ga_ref, lr_ref, wgk_ref, bgk_ref, nw_ref,
                       oa_ref, sfin_ref, st_s, vt_s, o_s, *, tb):
    t = pl.program_id(1)
    nsc = tb // GLA_SC
    nd = GLA_SC // GLA_CHUNK - 1

    @pl.when(t == 0)
    def _():
        st_s[...] = jnp.zeros_like(st_s)

    u = _dot(lr_ref[...], wgk_ref[...]) + bgk_ref[...]
    gk = _log_sigmoid(u) * (1.0 / 16.0)
    sub = lax.broadcasted_iota(jnp.int32, gk.shape, 0) % SUBLANE
    p8 = gk
    for s in (1, 2, 4):
        p8 = p8 + jnp.where(sub >= s, pltpu.roll(p8, s, 0), 0.0)
    nchunk = tb // GLA_CHUNK
    b_parts, bs_parts, tot = [], [], []
    acc = None
    for c in range(nchunk):
        r0 = c * GLA_CHUNK
        lo = p8[r0:r0 + SUBLANE, :]
        hi = p8[r0 + SUBLANE:r0 + GLA_CHUNK, :] + lo[SUBLANE - 1:SUBLANE, :]
        if c % (GLA_SC // GLA_CHUNK) == 0:
            b_parts += [lo, hi]
            bs_parts += [lo, hi]
            acc = hi[SUBLANE - 1:SUBLANE, :]
        else:
            b_parts += [lo, hi]
            bs_parts += [lo + acc, hi + acc]
            acc = acc + hi[SUBLANE - 1:SUBLANE, :]
        tot.append(hi[SUBLANE - 1:SUBLANE, :])
    b = jnp.concatenate(b_parts, axis=0)
    bs = jnp.concatenate(bs_parts, axis=0)

    def per_chunk(vals, shift):
        return jnp.concatenate([jnp.broadcast_to(vals[(c + shift) % nchunk], (GLA_CHUNK, GLA_KEY))
                                for c in range(nchunk)], axis=0)

    blb = per_chunk(tot, 0)
    etot = [jnp.exp(t) for t in tot]

    q = q_ref[...] * (GLA_DK ** -0.5)
    k = k_ref[...]
    qin = _bf(q * jnp.exp(b))
    kout = _bf(k * jnp.exp(-b))
    kd = k * jnp.exp(blb - b)
    qsc = _bf(q * jnp.exp(bs))
    vt_s[...] = v_ref[...].T

    ri = lax.broadcasted_iota(jnp.int32, (tb, tb), 0)
    ci = lax.broadcasted_iota(jnp.int32, (tb, tb), 1)
    delta = jnp.where(ri // GLA_SC == ci // GLA_SC, ri // GLA_CHUNK - ci // GLA_CHUNK, -1)
    m_intra = (delta == 0) & (ci <= ri)
    m_dist = [delta == d + 1 for d in range(nd)]
    kds = [_bf(kd)]
    for d in range(1, nd):
        kd = kd * per_chunk(etot, d)
        kds.append(_bf(kd))
    for h in range(GLA_H):
        ks = slice(h * GLA_DK, (h + 1) * GLA_DK)
        vs = slice(h * GLA_DV, (h + 1) * GLA_DV)
        a = jnp.where(m_intra, _dot_nt(qin[:, ks], kout[:, ks]), 0.0)
        for d in range(nd):
            a = jnp.where(m_dist[d], _dot_nt(qin[:, ks], kds[d][:, ks]), a)
        o_s[:, vs] = _dot(a, v_ref[:, vs])

    for sc in range(nsc):
        rows = slice(sc * GLA_SC, (sc + 1) * GLA_SC)
        last = bs[(sc + 1) * GLA_SC - 1:(sc + 1) * GLA_SC, :]
        k2 = _bf(k[rows, :] * jnp.exp(last - bs[rows, :]))
        elast = jnp.exp(last)
        for h in range(GLA_H):
            ks = slice(h * GLA_DK, (h + 1) * GLA_DK)
            vs = slice(h * GLA_DV, (h + 1) * GLA_DV)
            st = st_s[h]
            o_s[rows, vs] += _dot_nt(qsc[rows, ks], st)
            st_s[h] = st * elast[:, ks] + _dot(vt_s[vs, rows], k2[:, ks])

    nw = nw_ref[...]
    for h in range(GLA_H):
        vs = slice(h * GLA_DV, (h + 1) * GLA_DV)
        o = o_s[:, vs]
        y = o * lax.rsqrt(jnp.mean(o * o, axis=-1, keepdims=True) + EPS) * nw
        oa_ref[:, vs] = (y * _silu(ga_ref[:, vs])).astype(oa_ref.dtype)

    @pl.when(t == pl.num_programs(1) - 1)
    def _():
        for h in range(GLA_H):
            sfin_ref[h] = st_s[h].T


def _gla_prompt(layer, z, lr, wgk_pad, b_gk, gla_norm_w):
    tb = 256
    nt = SEQ // tb
    row = lambda b, t: b * nt + t
    kern = functools.partial(_gla_prompt_kernel, tb=tb)
    return pl.pallas_call(
        kern,
        grid=(BATCH, nt),
        in_specs=[pl.BlockSpec((tb, GLA_KEY), lambda b, t: (row(b, t), Z_QA // GLA_KEY)),
                  pl.BlockSpec((tb, GLA_KEY), lambda b, t: (row(b, t), Z_KA // GLA_KEY)),
                  pl.BlockSpec((tb, GLA_VAL), lambda b, t: (row(b, t), Z_VA // GLA_VAL)),
                  pl.BlockSpec((tb, GLA_VAL), lambda b, t: (row(b, t), Z_GA // GLA_VAL)),
                  pl.BlockSpec((tb, LR_PAD), lambda b, t: (row(b, t), 0)),
                  pl.BlockSpec((None, LR_PAD, GLA_KEY), lambda b, t: (layer, 0, 0)),
                  pl.BlockSpec((None, 1, GLA_KEY), lambda b, t: (layer, 0, 0)),
                  pl.BlockSpec((None, 1, GLA_DV), lambda b, t: (layer, 0, 0))],
        out_specs=[pl.BlockSpec((tb, GLA_VAL), lambda b, t: (row(b, t), 0)),
                   pl.BlockSpec((None, GLA_H, GLA_DK, GLA_DV), lambda b, t: (b, 0, 0, 0))],
        out_shape=[jax.ShapeDtypeStruct((BATCH * SEQ, GLA_VAL), BF16),
                   jax.ShapeDtypeStruct((BATCH, GLA_H, GLA_DK, GLA_DV), F32)],
        scratch_shapes=[pltpu.VMEM((GLA_H, GLA_DV, GLA_DK), F32),
                        pltpu.VMEM((GLA_VAL, tb), F32),
                        pltpu.VMEM((tb, GLA_VAL), F32)],
        compiler_params=_cparams(2),
        name="gla_prompt",
    )(z, z, z, z, lr, wgk_pad, b_gk.reshape(DEPTH, 1, GLA_KEY), gla_norm_w.reshape(DEPTH, 1, GLA_DV))


def _layernorm(x, w, b):
    mu = jnp.mean(x, axis=-1, keepdims=True)
    xc = x - mu
    var = jnp.mean(xc * xc, axis=-1, keepdims=True)
    return xc * lax.rsqrt(var + EPS) * w + b


def _gmlp_prompt_kernel(u_ref, v_ref, ws_ref, bst_ref, nw_ref, nb_ref, ob_ref, *, nsub):
    ri = lax.broadcasted_iota(jnp.int32, (GM_CHUNK, GM_CHUNK), 0)
    ci = lax.broadcasted_iota(jnp.int32, (GM_CHUNK, GM_CHUNK), 1)
    tril = ci <= ri
    for s in range(nsub):
        rs = slice(s * GM_CHUNK, (s + 1) * GM_CHUNK)
        u = _gelu(u_ref[rs, :])
        v = _layernorm(_gelu(v_ref[rs, :]), nw_ref[...], nb_ref[...])
        for g in range(GM_GROUPS):
            cs = slice(g * GM_GW, (g + 1) * GM_GW)
            wm = jnp.where(tril, ws_ref[g], 0.0)
            mixed = _dot(wm, v[:, cs]) + bst_ref[:, g:g + 1]
            ob_ref[rs, cs] = (u[:, cs] * mixed).astype(ob_ref.dtype)


def _gmlp_prompt(layer, z, gm_ws, gm_bs_t, gm_norm_w, gm_norm_b):
    nsub = 4
    tb = nsub * GM_CHUNK
    kern = functools.partial(_gmlp_prompt_kernel, nsub=nsub)
    return pl.pallas_call(
        kern,
        grid=(BATCH * SEQ // tb,),
        in_specs=[pl.BlockSpec((tb, GM_WIDTH), lambda i: (i, Z_UB // GM_WIDTH)),
                  pl.BlockSpec((tb, GM_WIDTH), lambda i: (i, Z_VB // GM_WIDTH)),
                  pl.BlockSpec((None, GM_GROUPS, GM_CHUNK, GM_CHUNK), lambda i: (layer, 0, 0, 0)),
                  pl.BlockSpec((None, GM_CHUNK, GM_GROUPS), lambda i: (layer, 0, 0)),
                  pl.BlockSpec((None, 1, GM_WIDTH), lambda i: (layer, 0, 0)),
                  pl.BlockSpec((None, 1, GM_WIDTH), lambda i: (layer, 0, 0))],
        out_specs=pl.BlockSpec((tb, GM_WIDTH), lambda i: (i, 0)),
        out_shape=jax.ShapeDtypeStruct((BATCH * SEQ, GM_WIDTH), BF16),
        compiler_params=_cparams(1),
        name="gmlp_prompt",
    )(z, z, gm_ws, gm_bs_t, gm_norm_w.reshape(DEPTH, 1, GM_WIDTH), gm_norm_b.reshape(DEPTH, 1, GM_WIDTH))


def _gmlp_sample_kernel(u_ref, v_ref, w0_ref, b0_ref, nw_ref, nb_ref, ob_ref, vn_ref):
    u = _gelu(u_ref[...])
    v = _layernorm(_gelu(v_ref[...]), nw_ref[...], nb_ref[...])
    vn_ref[...] = v
    ob_ref[...] = u * (w0_ref[...] * v + b0_ref[...])


def _gmlp_sample(layer, z, w0_row, b0_row, gm_norm_w, gm_norm_b):
    full = lambda i: (0, 0)
    lrow = lambda i: (layer, 0, 0)
    return pl.pallas_call(
        _gmlp_sample_kernel,
        grid=(1,),
        in_specs=[pl.BlockSpec((DEC_BATCH, GM_WIDTH), lambda i: (0, Z_UB // GM_WIDTH)),
                  pl.BlockSpec((DEC_BATCH, GM_WIDTH), lambda i: (0, Z_VB // GM_WIDTH)),
                  pl.BlockSpec((None, 1, GM_WIDTH), lrow),
                  pl.BlockSpec((None, 1, GM_WIDTH), lrow),
                  pl.BlockSpec((None, 1, GM_WIDTH), lrow),
                  pl.BlockSpec((None, 1, GM_WIDTH), lrow)],
        out_specs=[pl.BlockSpec((DEC_BATCH, GM_WIDTH), full), pl.BlockSpec((DEC_BATCH, GM_WIDTH), full)],
        out_shape=[jax.ShapeDtypeStruct((DEC_BATCH, GM_WIDTH), F32),
                   jax.ShapeDtypeStruct((DEC_BATCH, GM_WIDTH), F32)],
        compiler_params=_cparams(1),
        name="gmlp_sample",
    )(z, z, w0_row, b0_row, gm_norm_w.reshape(DEPTH, 1, GM_WIDTH), gm_norm_b.reshape(DEPTH, 1, GM_WIDTH))


def _alibi_slope(h):
    return float(2.0 ** (-8.0 * (h + 1) / SWA_HQ))


def _swa_lane_halves(x, half):
    lane = lax.broadcasted_iota(jnp.int32, x.shape, 1)
    own = jnp.where((lane >= half * SWA_HD) & (lane < (half + 1) * SWA_HD), x, 0.0)
    other = pltpu.roll(own, SWA_HD, 1)
    return (own, other) if half == 0 else (other, own)


def _swa_prompt_block(n, q_ref, kc_ref, kp_ref, vc_ref, vp_ref, sink_ref, oc_ref, s_s, p_s):
    w = WINDOW
    ri = lax.broadcasted_iota(jnp.int32, (w, 2 * w), 0)
    ci = lax.broadcasted_iota(jnp.int32, (w, 2 * w), 1)
    dist_i = w + ri - ci
    valid = (dist_i >= 0) & (dist_i < w) & ((ci >= w) | (n > 0))
    dist = dist_i.astype(F32)
    kcat = jnp.concatenate([kp_ref[...], kc_ref[...]], axis=0)
    vcat = jnp.concatenate([vp_ref[...], vc_ref[...]], axis=0)
    heads = []
    for kv in range(SWA_HKV):
        t, half = kv // 2, kv % 2
        k_lo, k_hi = _swa_lane_halves(kcat[:, t * LANE:(t + 1) * LANE], half)
        q2 = jnp.concatenate([q_ref[:, 2 * kv * LANE:(2 * kv + 1) * LANE],
                              q_ref[:, (2 * kv + 1) * LANE:(2 * kv + 2) * LANE]], axis=0)
        for par, kk in ((0, k_lo), (1, k_hi)):
            s = _dot_nt(q2, kk) * (SWA_HD ** -0.5)
            for e in range(2):
                h = SWA_G * kv + 2 * e + par
                seg = len(heads)
                heads.append(h)
                s_s[seg * w:(seg + 1) * w, :] = jnp.where(
                    valid, s[e * w:(e + 1) * w, :] - _alibi_slope(h) * dist, NEG_BIG)
    s = s_s[...]
    sink = jnp.concatenate([jnp.broadcast_to(sink_ref[h:h + 1, 0:1], (w, 1)) for h in heads], axis=0)
    m = jnp.maximum(jnp.max(s, axis=-1, keepdims=True), sink)
    p = jnp.exp(s - m)
    inv = 1.0 / (jnp.sum(p, axis=-1, keepdims=True) + jnp.exp(sink - m))
    p_s[...] = (p * inv).astype(p_s.dtype)
    for kv in range(SWA_HKV):
        t, half = kv // 2, kv % 2
        v_lo, v_hi = _swa_lane_halves(vcat[:, t * LANE:(t + 1) * LANE], half)
        r0 = SWA_G * kv * w
        o = _dot(p_s[r0:r0 + 2 * w, :], v_lo) + _dot(p_s[r0 + 2 * w:r0 + 4 * w, :], v_hi)
        oc_ref[:, 2 * kv * LANE:(2 * kv + 1) * LANE] = o[:w].astype(oc_ref.dtype)
        oc_ref[:, (2 * kv + 1) * LANE:(2 * kv + 2) * LANE] = o[w:].astype(oc_ref.dtype)


STATE_SLOTS = 3


def _state_copy(hbm, sbuf, sem, layer, step, rps, to_hbm):
    slot = lax.rem(step, STATE_SLOTS)
    rows = hbm.at[layer, pl.ds(step * rps, rps)]
    if to_hbm:
        return pltpu.make_async_copy(sbuf.at[slot], rows, sem.at[slot])
    return pltpu.make_async_copy(rows, sbuf.at[slot], sem.at[slot])


def _state_ring_begin(g, n_steps, copy_in, copy_out, x_s, oa_s):
    @pl.when(g == 0)
    def _():
        x_s[...] = jnp.zeros_like(x_s)
        oa_s[...] = jnp.zeros_like(oa_s)
        copy_in(0).start()

    @pl.when(g >= STATE_SLOTS - 1)
    def _():
        copy_out(g - (STATE_SLOTS - 1)).wait()

    @pl.when(g + 1 < n_steps)
    def _():
        copy_in(g + 1).start()

    copy_in(g).wait()


def _state_ring_end(g, n_steps, copy_out):
    copy_out(g).start()

    @pl.when(g == n_steps - 1)
    def _():
        for back in range(STATE_SLOTS - 2, -1, -1):
            copy_out(g - back).wait()


def _gla_decode_rows(g, rps, zq_ref, zk_ref, zv_ref, zga_ref, lr_ref, wgk_ref, bgk_ref, nw_ref,
                     oa_ref, sbuf, x_s, oa_s):
    slot = lax.rem(g, STATE_SLOTS)
    sub = lax.rem(g * rps, SUBLANE)
    up = lax.rem(SUBLANE - sub, SUBLANE)
    u = _dot(lr_ref[...], wgk_ref[...]) + bgk_ref[...]
    eg = pltpu.roll(jnp.exp(_log_sigmoid(u) * (1.0 / 16.0)), up, 0)
    zq = pltpu.roll(zq_ref[...], up, 0) * (GLA_DK ** -0.5)
    zk = pltpu.roll(zk_ref[...], up, 0)
    zv = pltpu.roll(zv_ref[...], up, 0)
    gate = _silu(pltpu.roll(zga_ref[...], up, 0))
    nw = nw_ref[...]
    row8 = lax.broadcasted_iota(jnp.int32, (SUBLANE, GLA_DV), 0)
    for h in range(GLA_H):
        ks = slice(h * GLA_DK, (h + 1) * GLA_DK)
        vs = slice(h * GLA_DV, (h + 1) * GLA_DV)
        x_s[0:SUBLANE, :] = zq[:, ks]
        x_s[SUBLANE:2 * SUBLANE, :] = zk[:, ks]
        x_s[2 * SUBLANE:3 * SUBLANE, :] = eg[:, ks]
        xt = x_s[...].T
        y8 = jnp.zeros((SUBLANE, GLA_DV), F32)
        for r in range(rps):
            qc = xt[:, r:r + 1]
            kc = xt[:, SUBLANE + r:SUBLANE + r + 1]
            gc = xt[:, 2 * SUBLANE + r:2 * SUBLANE + r + 1]
            s_new = gc * sbuf[slot, r, h] + kc * zv[r:r + 1, vs]
            sbuf[slot, r, h] = s_new
            o = jnp.sum(qc * s_new, axis=0, keepdims=True)
            y = o * lax.rsqrt(jnp.mean(o * o, axis=-1, keepdims=True) + EPS) * nw * gate[r:r + 1, vs]
            y8 = jnp.where(row8 == r, y, y8)
        y8 = pltpu.roll(y8, sub, 0)
        acc = jnp.where(sub == 0, y8, oa_s[:, vs] + y8)
        oa_s[:, vs] = acc
        oa_ref[:, vs] = acc


def _swa_gla_kernel(q_ref, kc_ref, kp_ref, vc_ref, vp_ref, sink_ref,
                    zq_ref, zk_ref, zv_ref, zga_ref, lr_ref, wgk_ref, bgk_ref, nw_ref, sin_hbm, *rest,
                    layer, rps, aliased):
    if aliased:
        rest = rest[1:]
    oc_ref, oa_ref, sout_hbm, s_s, p_s, sbuf, x_s, oa_s, in_sem, out_sem = rest
    n = pl.program_id(1)
    g = pl.program_id(0) * pl.num_programs(1) + n
    n_steps = pl.num_programs(0) * pl.num_programs(1)
    copy_in = functools.partial(_state_copy, sin_hbm, sbuf, in_sem, layer, rps=rps, to_hbm=False)
    copy_out = functools.partial(_state_copy, sout_hbm, sbuf, out_sem, layer, rps=rps, to_hbm=True)
    _state_ring_begin(g, n_steps, copy_in, copy_out, x_s, oa_s)
    _gla_decode_rows(g, rps, zq_ref, zk_ref, zv_ref, zga_ref, lr_ref, wgk_ref, bgk_ref, nw_ref,
                     oa_ref, sbuf, x_s, oa_s)
    _swa_prompt_block(n, q_ref, kc_ref, kp_ref, vc_ref, vp_ref, sink_ref, oc_ref, s_s, p_s)
    _state_ring_end(g, n_steps, copy_out)


def _swa_prompt_gla_sample(layer, zp, sinks_b, zs, lrs, wgk_pad, b_gk, gla_norm_w, state_gla, state_out):
    nb = SEQ // WINDOW
    n_steps = BATCH * nb
    rps = DEC_BATCH // n_steps
    assert rps * n_steps == DEC_BATCH and SUBLANE % rps == 0 and n_steps >= STATE_SLOTS
    row = lambda b, n: b * nb + n
    prev = lambda b, n: b * nb + jnp.maximum(n - 1, 0)
    srow = lambda b, n: (row(b, n) * rps) // SUBLANE
    in_specs = [pl.BlockSpec((WINDOW, SWA_Q), lambda b, n: (row(b, n), Z_QC // SWA_Q)),
                pl.BlockSpec((WINDOW, SWA_KV), lambda b, n: (row(b, n), Z_KC // SWA_KV)),
                pl.BlockSpec((WINDOW, SWA_KV), lambda b, n: (prev(b, n), Z_KC // SWA_KV)),
                pl.BlockSpec((WINDOW, SWA_KV), lambda b, n: (row(b, n), Z_VC // SWA_KV)),
                pl.BlockSpec((WINDOW, SWA_KV), lambda b, n: (prev(b, n), Z_VC // SWA_KV)),
                pl.BlockSpec((None, SWA_HQ, LANE), lambda b, n: (layer, 0, 0)),
                pl.BlockSpec((SUBLANE, GLA_KEY), lambda b, n: (srow(b, n), Z_QA // GLA_KEY)),
                pl.BlockSpec((SUBLANE, GLA_KEY), lambda b, n: (srow(b, n), Z_KA // GLA_KEY)),
                pl.BlockSpec((SUBLANE, GLA_VAL), lambda b, n: (srow(b, n), Z_VA // GLA_VAL)),
                pl.BlockSpec((SUBLANE, GLA_VAL), lambda b, n: (srow(b, n), Z_GA // GLA_VAL)),
                pl.BlockSpec((SUBLANE, LR_PAD), lambda b, n: (srow(b, n), 0)),
                pl.BlockSpec((None, LR_PAD, GLA_KEY), lambda b, n: (layer, 0, 0)),
                pl.BlockSpec((None, 1, GLA_KEY), lambda b, n: (layer, 0, 0)),
                pl.BlockSpec((None, 1, GLA_DV), lambda b, n: (layer, 0, 0)),
                pl.BlockSpec(memory_space=pl.ANY)]
    args = [zp, zp, zp, zp, zp, sinks_b, zs, zs, zs, zs, lrs, wgk_pad, b_gk.reshape(DEPTH, 1, GLA_KEY),
            gla_norm_w.reshape(DEPTH, 1, GLA_DV), state_gla]
    aliases = {}
    if state_out is not None:
        in_specs.append(pl.BlockSpec(memory_space=pl.ANY))
        args.append(state_out)
        aliases = {len(args) - 1: 2}
    kern = functools.partial(_swa_gla_kernel, layer=layer, rps=rps, aliased=state_out is not None)
    return pl.pallas_call(
        kern,
        grid=(BATCH, nb),
        in_specs=in_specs,
        out_specs=[pl.BlockSpec((WINDOW, SWA_Q), lambda b, n: (row(b, n), 0)),
                   pl.BlockSpec((SUBLANE, GLA_VAL), lambda b, n: (srow(b, n), 0)),
                   pl.BlockSpec(memory_space=pl.ANY)],
        out_shape=[jax.ShapeDtypeStruct((BATCH * SEQ, SWA_Q), BF16),
                   jax.ShapeDtypeStruct((DEC_BATCH, GLA_VAL), F32),
                   jax.ShapeDtypeStruct((DEPTH, DEC_BATCH, GLA_H, GLA_DK, GLA_DV), F32)],
        scratch_shapes=[pltpu.VMEM((SWA_HQ * WINDOW, 2 * WINDOW), F32),
                        pltpu.VMEM((SWA_HQ * WINDOW, 2 * WINDOW), BF16),
                        pltpu.VMEM((STATE_SLOTS, rps, GLA_H, GLA_DK, GLA_DV), F32),
                        pltpu.VMEM((LANE, GLA_DK), F32),
                        pltpu.VMEM((SUBLANE, GLA_VAL), F32),
                        pltpu.SemaphoreType.DMA((STATE_SLOTS,)),
                        pltpu.SemaphoreType.DMA((STATE_SLOTS,))],
        input_output_aliases=aliases,
        compiler_params=_cparams(2),
        name="swa_prompt_gla_sample",
    )(*args)


def _swa_sample_kernel(q_ref, kn_ref, vn_ref, kt_ref, vt_ref, sink_ref, slope_ref, o_ref, *, rb):
    wb = WINDOW
    j = lax.broadcasted_iota(jnp.int32, (SWA_HQ, wb), 1)
    dist = (wb - j).astype(F32)
    ok = j >= 1
    grp = lax.broadcasted_iota(jnp.int32, (SWA_HQ, 1), 0) // SWA_G
    slope = slope_ref[:, 0:1]
    sink = sink_ref[:, 0:1]

    def per_head(rows):
        out = jnp.broadcast_to(rows[0:1, :], (SWA_HQ, SWA_HD))
        for kv in range(1, SWA_HKV):
            out = jnp.where(grp == kv, rows[kv:kv + 1, :], out)
        return out

    s_rows, self_rows = [], []
    for r in range(rb):
        q = q_ref[r]
        s = _dot(q, kt_ref[r, 0])
        for kv in range(1, SWA_HKV):
            s = jnp.where(grp == kv, _dot(q, kt_ref[r, kv]), s)
        s_rows.append(jnp.where(ok, s * (SWA_HD ** -0.5) - slope * dist, NEG_BIG))
        self_rows.append(jnp.sum(_bf(q).astype(F32) * _bf(per_head(kn_ref[r])).astype(F32), axis=-1,
                                 keepdims=True) * (SWA_HD ** -0.5))
    s = jnp.concatenate(s_rows, axis=0)
    s_self = jnp.concatenate(self_rows, axis=0)
    sink = jnp.concatenate([sink] * rb, axis=0)
    m = jnp.maximum(jnp.maximum(jnp.max(s, axis=-1, keepdims=True), s_self), sink)
    p = jnp.exp(s - m)
    p_self = jnp.exp(s_self - m)
    inv = 1.0 / (jnp.sum(p, axis=-1, keepdims=True) + p_self + jnp.exp(sink - m))
    pn = _bf(p * inv)
    pn_self = _bf(p_self * inv).astype(F32)
    for r in range(rb):
        rows = slice(r * SWA_HQ, (r + 1) * SWA_HQ)
        o = _dot_nt(pn[rows, :], vt_ref[r, 0])
        for kv in range(1, SWA_HKV):
            o = jnp.where(grp == kv, _dot_nt(pn[rows, :], vt_ref[r, kv]), o)
        o_ref[r] = o + pn_self[rows, :] * _bf(per_head(vn_ref[r])).astype(F32)


def _swa_sample(layer, q3, kn3, vn3, cache_kt, cache_vt, sinks_b, slopes_b):
    rb = 8
    kern = functools.partial(_swa_sample_kernel, rb=rb)
    return pl.pallas_call(
        kern,
        grid=(DEC_BATCH // rb,),
        in_specs=[pl.BlockSpec((rb, SWA_HQ, SWA_HD), lambda i: (i, 0, 0)),
                  pl.BlockSpec((rb, SWA_HKV, SWA_HD), lambda i: (i, 0, 0)),
                  pl.BlockSpec((rb, SWA_HKV, SWA_HD), lambda i: (i, 0, 0)),
                  pl.BlockSpec((None, rb, SWA_HKV, SWA_HD, WINDOW), lambda i: (layer, i, 0, 0, 0)),
                  pl.BlockSpec((None, rb, SWA_HKV, SWA_HD, WINDOW), lambda i: (layer, i, 0, 0, 0)),
                  pl.BlockSpec((None, SWA_HQ, LANE), lambda i: (layer, 0, 0)),
                  pl.BlockSpec((SWA_HQ, LANE), lambda i: (0, 0))],
        out_specs=pl.BlockSpec((rb, SWA_HQ, SWA_HD), lambda i: (i, 0, 0)),
        out_shape=jax.ShapeDtypeStruct((DEC_BATCH, SWA_HQ, SWA_HD), F32),
        compiler_params=_cparams(1),
        name="swa_sample",
    )(q3, kn3, vn3, cache_kt, cache_vt, sinks_b, slopes_b)


def _w_in_kernel(hp_ref, hs_ref, wa_ref, wb_ref, zp_ref, zs_ref, *, n_plain):
    j = pl.program_id(1)
    a = wa_ref[...]
    shifted = jnp.concatenate([a[GLA_RANK:, :], wb_ref[...]], axis=0)
    w = _bf(jnp.where(j >= n_plain, shifted, a))
    zp_ref[...] = _dot_nt(hp_ref[...], w)

    @pl.when(pl.program_id(0) == 0)
    def _():
        zs_ref[...] = _dot_nt(hs_ref[...], w)


def _w_in(layer, hp, hs, w_in_t):
    tm, tn = SEQ, 512
    assert LR_COL % tn == 0 and Z_WIDTH % tn == 0 and tn % GLA_RANK == 0
    rows_p, rows_s = hp.shape[0], hs.shape[0]
    nj = Z_WIDTH // tn
    sj = lambda i, j: jnp.where(i == 0, j, nj - 1)
    return pl.pallas_call(
        functools.partial(_w_in_kernel, n_plain=LR_COL // tn),
        grid=(rows_p // tm, nj),
        in_specs=[pl.BlockSpec((tm, D), lambda i, j: (i, 0), pipeline_mode=pl.Buffered(1)),
                  pl.BlockSpec((rows_s, D), lambda i, j: (0, 0)),
                  pl.BlockSpec((None, tn, D), lambda i, j: (layer, j, 0)),
                  pl.BlockSpec((None, GLA_RANK, D), lambda i, j: (layer, (j + 1) * (tn // GLA_RANK), 0))],
        out_specs=[pl.BlockSpec((tm, tn), lambda i, j: (i, j)),
                   pl.BlockSpec((rows_s, tn), lambda i, j: (0, sj(i, j)))],
        out_shape=[jax.ShapeDtypeStruct((rows_p, Z_WIDTH), F32), jax.ShapeDtypeStruct((rows_s, Z_WIDTH), F32)],
        compiler_params=_cparams(2),
        name="w_in",
    )(hp, hs, w_in_t, w_in_t)


def _layer(layer, xp, xs, mod, p, state_gla, cache_k, cache_v, state_out):
    hp, lrp = _prep(layer, xp, p["norm1_w"], mod, MOD_SC1, MOD_SH1, False, p["w_lr_t"])
    hs, lrs = _prep(layer, xs, p["norm1_w"], mod, MOD_SC1, MOD_SH1, True, p["w_lr_t"])
    zp, zs = _w_in(layer, hp, hs, p["w_in_t"])
    oa_p, s_p = _gla_prompt(layer, zp, lrp, p["wgk_pad"], p["b_gk"], p["gla_norm_w"])
    ob_p = _gmlp_prompt(layer, zp, p["gm_ws"], p["gm_bs_t"], p["gm_norm_w"], p["gm_norm_b"])
    oc_p, oa_s, state_out = _swa_prompt_gla_sample(layer, zp, p["sinks_b"], zs, lrs, p["wgk_pad"], p["b_gk"],
                                                   p["gla_norm_w"], state_gla, state_out)
    z4 = zp.reshape(BATCH, SEQ, Z_WIDTH)
    kp_rows = z4[:, SEQ - WINDOW:, Z_KC:Z_KC + SWA_KV].reshape(BATCH, WINDOW, SWA_HKV, SWA_HD)
    vp_rows = z4[:, SEQ - WINDOW:, Z_VC:Z_VC + SWA_KV].reshape(BATCH, WINDOW, SWA_HKV, SWA_HD)
    ob_s, v_gm = _gmlp_sample(layer, zs, p["gm_w0"], p["gm_b0"], p["gm_norm_w"], p["gm_norm_b"])
    q3 = zs[:, Z_QC:Z_QC + SWA_Q].reshape(DEC_BATCH, SWA_HQ, SWA_HD)
    kn3 = zs[:, Z_KC:Z_KC + SWA_KV].reshape(DEC_BATCH, SWA_HKV, SWA_HD)
    vn3 = zs[:, Z_VC:Z_VC + SWA_KV].reshape(DEC_BATCH, SWA_HKV, SWA_HD)
    oc_s = _swa_sample(layer, q3, kn3, vn3, cache_k, cache_v, p["sinks_b"], p["slopes_b"]).reshape(DEC_BATCH, SWA_Q)
    ks_rows = kn3.reshape(DEC_BATCH, 1, SWA_HKV, SWA_HD)
    vs_rows = vn3.reshape(DEC_BATCH, 1, SWA_HKV, SWA_HD)
    (mp,), (ms,) = _fused_matmul(
        "merge", layer, [oa_p, ob_p, oc_p], [oa_s, ob_s, oc_s],
        [(0, p["w_pa"], 0), (1, p["w_pb"], 0), (2, p["w_pc"], 0)],
        [("tile", zp, zs, Z_GATES), ("tile", zp, zs, Z_GATES + D), ("tile", zp, zs, Z_GATES + 2 * D)],
        _epi_merge, [BF16], D, 1024, 512)
    x1p, h2p, x1s, h2s = _wo_prep(layer, mp, ms, p["w_o_bf"], xp, xs, p["norm2_w"], mod)
    (hidp,), (hids,) = _fused_matmul(
        "ffn_in", layer, [h2p], [h2s], [(0, p["w_ffn_in"], 0), (0, p["w_ffn_in"], FFN_HIDDEN)],
        [], _epi_swiglu, [BF16], FFN_HIDDEN, 2048, 512)
    (x2p,), (x2s,) = _fused_matmul("ffn_out", layer, [hidp], [hids], [(0, p["w_ffn_out"], 0)],
                                   [("tile", x1p, x1s, 0), ("mod", mod, MOD_G2)], _epi_residual, [F32],
                                   D, 1024, 512)
    return x2p, x2s, s_p, state_out, kp_rows, vp_rows, ks_rows, vs_rows, v_gm


def kernel(x_prompt, x_sample, c_prompt, c_sample, state_gla, cache_swa_k, cache_swa_v, w_ada, b_ada, norm1_w,
           norm2_w, w_in, w_gk2, b_gk, gla_norm_w, gm_norm_w, gm_norm_b, gm_ws, gm_bs, swa_sinks, w_pa, w_pb,
           w_pc, w_o, w_ffn_in, w_ffn_out, final_norm_w):
    w_in_t = jnp.swapaxes(w_in, 1, 2)
    w_lr_t = jnp.pad(w_in_t[:, LR_COL:LR_COL + GLA_RANK, :], ((0, 0), (0, LR_PAD - GLA_RANK), (0, 0))).astype(BF16)
    p = {
        "norm1_w": norm1_w, "norm2_w": norm2_w, "w_in_t": w_in_t, "w_lr_t": w_lr_t,
        "wgk_pad": jnp.pad(w_gk2, ((0, 0), (0, LR_PAD - GLA_RANK), (0, 0))),
        "b_gk": b_gk, "gla_norm_w": gla_norm_w, "gm_norm_w": gm_norm_w, "gm_norm_b": gm_norm_b,
        "gm_ws": gm_ws, "gm_bs_t": jnp.swapaxes(gm_bs, 1, 2),
        "gm_w0": jnp.repeat(gm_ws[:, :, 0, 0], GM_GW, axis=1).reshape(DEPTH, 1, GM_WIDTH),
        "gm_b0": jnp.repeat(gm_bs[:, :, 0], GM_GW, axis=1).reshape(DEPTH, 1, GM_WIDTH),
        "sinks_b": jnp.broadcast_to(swa_sinks[:, :, None], (DEPTH, SWA_HQ, LANE)),
        "slopes_b": jnp.broadcast_to(
            jnp.asarray([_alibi_slope(h) for h in range(SWA_HQ)], F32)[:, None], (SWA_HQ, LANE)),
        "w_pa": w_pa, "w_pb": w_pb, "w_pc": w_pc, "w_o_bf": w_o.astype(BF16), "w_ffn_in": w_ffn_in,
        "w_ffn_out": w_ffn_out.astype(BF16),
    }
    c_all = jnp.concatenate([c_sample, c_prompt, jnp.zeros((MOD_ROWS - DEC_BATCH - BATCH, D), F32)], axis=0)
    mod = _ada(c_all, w_ada, b_ada)

    xp = x_prompt.reshape(BATCH * SEQ, D)
    xs = x_sample.reshape(DEC_BATCH, D)
    cache_k = jnp.transpose(cache_swa_k, (0, 1, 3, 4, 2))
    cache_v = jnp.transpose(cache_swa_v, (0, 1, 3, 4, 2))
    gla_p, kp, vp, ksm, vsm, gmv = [], [], [], [], [], []
    state_out = None
    for l in range(DEPTH):
        xp, xs, s_p, state_out, k_p, v_p, k_s, v_s, gv = _layer(l, xp, xs, mod, p, state_gla, cache_k, cache_v,
                                                                state_out)
        gla_p.append(s_p)
        kp.append(k_p)
        vp.append(v_p)
        ksm.append(k_s)
        vsm.append(v_s)
        gmv.append(gv.reshape(DEC_BATCH, 1, GM_WIDTH))
    y_prompt = _final_norm(xp, final_norm_w).reshape(BATCH, SEQ, D)
    y_sample = _final_norm(xs, final_norm_w).reshape(DEC_BATCH, 1, D)
    return (y_prompt, y_sample, jnp.stack(gla_p), state_out, jnp.stack(kp), jnp.stack(vp),
            jnp.stack(ksm), jnp.stack(vsm), jnp.stack(gmv))
```

```python
import functools

import jax
import jax.numpy as jnp
import numpy as np
from jax import lax
from jax.experimental import pallas as pl
from jax.experimental.pallas import tpu as pltpu

F32 = jnp.float32
BF16 = jnp.bfloat16

D = 2048
BATCH, SEQ = 2, 4096
DEPTH = 2
DEC_BATCH = 128
GLA_H, GLA_DK, GLA_DV = 4, 256, 512
GLA_KEY, GLA_VAL = GLA_H * GLA_DK, GLA_H * GLA_DV
GLA_RANK = 16
GLA_CHUNK = 16
GLA_SC = 128
GM_WIDTH, GM_GROUPS, GM_CHUNK = 1024, 4, 128
GM_GW = GM_WIDTH // GM_GROUPS
SWA_HQ, SWA_HKV, SWA_HD, WINDOW = 16, 4, 64, 128
SWA_G = SWA_HQ // SWA_HKV
SWA_Q, SWA_KV = SWA_HQ * SWA_HD, SWA_HKV * SWA_HD
FFN_HIDDEN = 5632
EPS = 1e-6
NEG_BIG = -1e30

Z_QA, Z_KA, Z_VA, Z_GA = 0, 1024, 2048, 4096
Z_UB, Z_VB = 6144, 7168
Z_QC, Z_KC, Z_VC = 8192, 9216, 9472
Z_GATES = 9728
Z_WIDTH = 15872
LR_COL = 6144
LANE = 128
SUBLANE = 8
LR_PAD = LANE

MOD_SH1, MOD_SC1, MOD_G1, MOD_SH2, MOD_SC2, MOD_G2 = range(6)
MOD_ROWS = DEC_BATCH + 8

VMEM_LIMIT = 58 * 1024 * 1024


def _cparams(n_axes):
    return pltpu.CompilerParams(dimension_semantics=("arbitrary",) * n_axes,
                                vmem_limit_bytes=VMEM_LIMIT)


def _bf(x):
    return x if x.dtype == BF16 else x.astype(BF16)


def _dot(a, b):
    return jnp.dot(_bf(a), _bf(b), preferred_element_type=F32)


def _dot_nt(a, b):
    return lax.dot_general(_bf(a), _bf(b), (((1,), (1,)), ((), ())), preferred_element_type=F32)


def _silu(x):
    return x * (1.0 / (1.0 + jnp.exp(-x)))


def _sigmoid(x):
    return 1.0 / (1.0 + jnp.exp(-x))


def _gelu(x):
    return 0.5 * x * (1.0 + jnp.tanh(np.sqrt(2.0 / np.pi).astype(np.float32) * (x + 0.044715 * (x * x * x))))


def _mod_spec_prompt(layer, chunk, tm, tn):
    cb, bpb = chunk * D // tn, SEQ // tm
    return pl.BlockSpec((None, None, 1, tn), lambda i, j: (layer, DEC_BATCH + i // bpb, 0, j + cb))


def _mod_spec_sample(layer, chunk, tn, jmap):
    cb = chunk * D // tn
    return pl.BlockSpec((None, DEC_BATCH, tn), lambda i, j: (layer, 0, jmap(i, j) + cb))


def _ada_kernel(c_ref, w_ref, b_ref, o_ref):
    o_ref[...] = _dot(_silu(c_ref[...]), w_ref[...]) + b_ref[...]


def _ada(c_all, w_ada, b_ada):
    tn = 2048
    return pl.pallas_call(
        _ada_kernel,
        grid=(DEPTH, 6 * D // tn),
        in_specs=[pl.BlockSpec((MOD_ROWS, D), lambda l, j: (0, 0)),
                  pl.BlockSpec((None, D, tn), lambda l, j: (l, 0, j)),
                  pl.BlockSpec((None, 1, tn), lambda l, j: (l, 0, j))],
        out_specs=pl.BlockSpec((None, MOD_ROWS, tn), lambda l, j: (l, 0, j)),
        out_shape=jax.ShapeDtypeStruct((DEPTH, MOD_ROWS, 6 * D), F32),
        compiler_params=_cparams(2),
        name="ada",
    )(c_all, w_ada, b_ada.reshape(DEPTH, 1, 6 * D))


def _prep_kernel(x_ref, nw_ref, sc_ref, sh_ref, *rest):
    x = x_ref[...]
    y = x * lax.rsqrt(jnp.mean(x * x, axis=-1, keepdims=True) + EPS) * nw_ref[...]
    h = (y * (1.0 + sc_ref[...]) + sh_ref[...]).astype(BF16)
    if len(rest) == 1:
        (o_ref,) = rest
    else:
        wlr_ref, o_ref, lr_ref = rest
        lr_ref[...] = _dot_nt(h, wlr_ref[...])
    o_ref[...] = h


def _prep(layer, x, norm_w, mod, sc_chunk, sh_chunk, sample, w_lr_t=None):
    rows = x.shape[0]
    if sample:
        tm, modop = rows, mod
        mod_specs = [_mod_spec_sample(layer, c, D, lambda i, j: j) for c in (sc_chunk, sh_chunk)]
    else:
        tm, modop = 1024, mod.reshape(DEPTH, MOD_ROWS, 1, 6 * D)
        mod_specs = [_mod_spec_prompt(layer, c, tm, D) for c in (sc_chunk, sh_chunk)]
    args = [x, norm_w.reshape(DEPTH, 1, D), modop, modop]
    in_specs = [pl.BlockSpec((tm, D), lambda i, j: (i, 0)),
                pl.BlockSpec((None, 1, D), lambda i, j: (layer, 0, 0))] + mod_specs
    out_specs = [pl.BlockSpec((tm, D), lambda i, j: (i, 0))]
    out_shape = [jax.ShapeDtypeStruct((rows, D), BF16)]
    if w_lr_t is not None:
        args.append(w_lr_t)
        in_specs.append(pl.BlockSpec((None, LR_PAD, D), lambda i, j: (layer, 0, 0)))
        out_specs.append(pl.BlockSpec((tm, LR_PAD), lambda i, j: (i, 0)))
        out_shape.append(jax.ShapeDtypeStruct((rows, LR_PAD), F32))
    res = pl.pallas_call(
        _prep_kernel,
        grid=(rows // tm, 1),
        in_specs=in_specs,
        out_specs=out_specs,
        out_shape=out_shape,
        compiler_params=_cparams(2),
        name="prep",
    )(*args)
    return res if w_lr_t is not None else res[0]


def _final_norm_kernel(x_ref, nw_ref, o_ref):
    x = x_ref[...]
    o_ref[...] = x * lax.rsqrt(jnp.mean(x * x, axis=-1, keepdims=True) + EPS) * nw_ref[...]


def _final_norm(x, w):
    rows = x.shape[0]
    tm = min(rows, 512)
    return pl.pallas_call(
        _final_norm_kernel,
        grid=(rows // tm,),
        in_specs=[pl.BlockSpec((tm, D), lambda i: (i, 0)), pl.BlockSpec((1, D), lambda i: (0, 0))],
        out_specs=pl.BlockSpec((tm, D), lambda i: (i, 0)),
        out_shape=jax.ShapeDtypeStruct((rows, D), F32),
        compiler_params=_cparams(1),
        name="final_norm",
    )(x, w.reshape(1, D))


def _mm_kernel(*refs, n_a, term_a, n_extra, n_out, epilogue):
    sizes = (n_a, n_a, len(term_a), n_extra, n_extra, n_out, n_out)
    groups, pos = [], 0
    for n in sizes:
        groups.append(refs[pos:pos + n])
        pos += n
    a_p, a_s, w_refs, e_p, e_s, o_p, o_s = groups
    w_vals = [_bf(w[...]) for w in w_refs]

    def run(a_refs, e_refs, o_refs):
        a_vals = [_bf(a[...]) for a in a_refs]
        dots = [_dot(a_vals[ai], w) for ai, w in zip(term_a, w_vals)]
        outs = epilogue(dots, [e[...] for e in e_refs])
        for o_ref, o in zip(o_refs, outs):
            o_ref[...] = o.astype(o_ref.dtype)

    run(a_p, e_p, o_p)

    @pl.when(pl.program_id(0) == 0)
    def _():
        run(a_s, e_s, o_s)


def _fused_matmul(name, layer, a_p, a_s, terms, extras, epilogue, out_dtypes, n_cols, tm, tn, lhs_buffers=2):
    rows_p, rows_s = a_p[0].shape[0], a_s[0].shape[0]
    nj = n_cols // tn
    grid = (rows_p // tm, nj)
    sj = lambda i, j: jnp.where(i == 0, j, nj - 1)
    args, in_specs = [], []
    for a in a_p:
        args.append(a)
        in_specs.append(pl.BlockSpec((tm, a.shape[1]), lambda i, j: (i, 0),
                                     pipeline_mode=pl.Buffered(lhs_buffers)))
    for a in a_s:
        args.append(a)
        in_specs.append(pl.BlockSpec((rows_s, a.shape[1]), lambda i, j: (0, 0)))
    for ai, w, col0 in terms:
        assert col0 % tn == 0 and w.shape[-2] == a_p[ai].shape[1]
        args.append(w)
        in_specs.append(pl.BlockSpec((None, w.shape[-2], tn), lambda i, j, cb=col0 // tn: (layer, 0, j + cb)))
    s_args, s_specs = [], []
    for ex in extras:
        if ex[0] == "tile":
            _, arr_p, arr_s, col0 = ex
            assert col0 % tn == 0
            args.append(arr_p)
            in_specs.append(pl.BlockSpec((tm, tn), lambda i, j, cb=col0 // tn: (i, j + cb)))
            s_args.append(arr_s)
            s_specs.append(pl.BlockSpec((rows_s, tn), lambda i, j, cb=col0 // tn: (0, sj(i, j) + cb)))
        else:
            _, mod, chunk = ex
            args.append(mod.reshape(DEPTH, MOD_ROWS, 1, 6 * D))
            in_specs.append(_mod_spec_prompt(layer, chunk, tm, tn))
            s_args.append(mod)
            s_specs.append(_mod_spec_sample(layer, chunk, tn, sj))
    kern = functools.partial(_mm_kernel, n_a=len(a_p), term_a=tuple(t[0] for t in terms),
                             n_extra=len(extras), n_out=len(out_dtypes), epilogue=epilogue)
    res = pl.pallas_call(
        kern,
        grid=grid,
        in_specs=in_specs + s_specs,
        out_specs=([pl.BlockSpec((tm, tn), lambda i, j: (i, j)) for _ in out_dtypes]
                   + [pl.BlockSpec((rows_s, tn), lambda i, j: (0, sj(i, j))) for _ in out_dtypes]),
        out_shape=([jax.ShapeDtypeStruct((rows_p, n_cols), dt) for dt in out_dtypes]
                   + [jax.ShapeDtypeStruct((rows_s, n_cols), dt) for dt in out_dtypes]),
        compiler_params=_cparams(2),
        name=name,
    )(*args, *s_args)
    return res[:len(out_dtypes)], res[len(out_dtypes):]


def _norm_mod(x, nw, sc, sh):
    y = x * lax.rsqrt(jnp.mean(x * x, axis=-1, keepdims=True) + EPS) * nw
    return (y * (1.0 + sc) + sh).astype(BF16)


def _wo_prep_kernel(mp_ref, ms_ref, w_ref, xp_ref, xs_ref, nw_ref, g1p_ref, scp_ref, shp_ref,
                    g1s_ref, scs_ref, shs_ref, x1p_ref, h2p_ref, x1s_ref, h2s_ref):
    w = w_ref[...]
    nw = nw_ref[...]
    x1 = xp_ref[...] + g1p_ref[...] * _dot(mp_ref[...], w)
    x1p_ref[...] = x1
    h2p_ref[...] = _norm_mod(x1, nw, scp_ref[...], shp_ref[...])

    @pl.when(pl.program_id(0) == 0)
    def _():
        x1s = xs_ref[...] + g1s_ref[...] * _dot(ms_ref[...], w)
        x1s_ref[...] = x1s
        h2s_ref[...] = _norm_mod(x1s, nw, scs_ref[...], shs_ref[...])


def _wo_prep(layer, mp, ms, w_o_bf, xp, xs, norm_w, mod):
    tm = 512
    rows_p, rows_s = mp.shape[0], ms.shape[0]
    mod4 = mod.reshape(DEPTH, MOD_ROWS, 1, 6 * D)
    whole = lambda i, j=0: (0, 0)
    pm = lambda c: pl.BlockSpec((None, None, 1, D), lambda i: (layer, DEC_BATCH + i // (SEQ // tm), 0, c))
    sm = lambda c: pl.BlockSpec((None, rows_s, D), lambda i: (layer, 0, c))
    return pl.pallas_call(
        _wo_prep_kernel,
        grid=(rows_p // tm,),
        in_specs=[pl.BlockSpec((tm, D), lambda i: (i, 0)),
                  pl.BlockSpec((rows_s, D), whole),
                  pl.BlockSpec((None, D, D), lambda i: (layer, 0, 0), pipeline_mode=pl.Buffered(1)),
                  pl.BlockSpec((tm, D), lambda i: (i, 0)),
                  pl.BlockSpec((rows_s, D), whole),
                  pl.BlockSpec((None, 1, D), lambda i: (layer, 0, 0)),
                  pm(MOD_G1), pm(MOD_SC2), pm(MOD_SH2), sm(MOD_G1), sm(MOD_SC2), sm(MOD_SH2)],
        out_specs=[pl.BlockSpec((tm, D), lambda i: (i, 0)),
                   pl.BlockSpec((tm, D), lambda i: (i, 0)),
                   pl.BlockSpec((rows_s, D), whole),
                   pl.BlockSpec((rows_s, D), whole)],
        out_shape=[jax.ShapeDtypeStruct((rows_p, D), F32), jax.ShapeDtypeStruct((rows_p, D), BF16),
                   jax.ShapeDtypeStruct((rows_s, D), F32), jax.ShapeDtypeStruct((rows_s, D), BF16)],
        compiler_params=_cparams(1),
        name="w_o_prep",
    )(mp, ms, w_o_bf, xp, xs, norm_w.reshape(DEPTH, 1, D), mod4, mod4, mod4, mod, mod, mod)


def _epi_merge(dots, ex):
    return [_sigmoid(ex[0]) * dots[0] + _sigmoid(ex[1]) * dots[1] + _sigmoid(ex[2]) * dots[2]]


def _epi_residual(dots, ex):
    return [ex[0] + ex[1] * dots[0]]


def _epi_swiglu(dots, ex):
    return [_silu(dots[0]) * dots[1]]


def _log_sigmoid(u):
    return -(jnp.maximum(-u, 0.0) + jnp.log1p(jnp.exp(-jnp.abs(u))))


def _gla_prompt_kernel(q_ref, k_ref, v_ref, ga_ref, lr_ref, wgk_ref, bgk_ref, nw_ref,
                       oa_ref, sfin_ref, st_s, vt_s, o_s, *, tb):
    t = pl.program_id(1)
    nsc = tb // GLA_SC
    nd = GLA_SC // GLA_CHUNK - 1

    @pl.when(t == 0)
    def _():
        st_s[...] = jnp.zeros_like(st_s)

    u = _dot(lr_ref[...], wgk_ref[...]) + bgk_ref[...]
    gk = _log_sigmoid(u) * (1.0 / 16.0)
    sub = lax.broadcasted_iota(jnp.int32, gk.shape, 0) % SUBLANE
    p8 = gk
    for s in (1, 2, 4):
        p8 = p8 + jnp.where(sub >= s, pltpu.roll(p8, s, 0), 0.0)
    nchunk = tb // GLA_CHUNK
    b_parts, bs_parts, tot = [], [], []
    acc = None
    for c in range(nchunk):
        r0 = c * GLA_CHUNK
        lo = p8[r0:r0 + SUBLANE, :]
        hi = p8[r0 + SUBLANE:r0 + GLA_CHUNK, :] + lo[SUBLANE - 1:SUBLANE, :]
        if c % (GLA_SC // GLA_CHUNK) == 0:
            b_parts += [lo, hi]
            bs_parts += [lo, hi]
            acc = hi[SUBLANE - 1:SUBLANE, :]
        else:
            b_parts += [lo, hi]
            bs_parts += [lo + acc, hi + acc]
            acc = acc + hi[SUBLANE - 1:SUBLANE, :]
        tot.append(hi[SUBLANE - 1:SUBLANE, :])
    b = jnp.concatenate(b_parts, axis=0)
    bs = jnp.concatenate(bs_parts, axis=0)

    def per_chunk(vals, shift):
        return jnp.concatenate([jnp.broadcast_to(vals[(c + shift) % nchunk], (GLA_CHUNK, GLA_KEY))
                                for c in range(nchunk)], axis=0)

    blb = per_chunk(tot, 0)
    etot = [jnp.exp(t) for t in tot]

    q = q_ref[...] * (GLA_DK ** -0.5)
    k = k_ref[...]
    qin = _bf(q * jnp.exp(b))
    kout = _bf(k * jnp.exp(-b))
    kd = k * jnp.exp(blb - b)
    qsc = _bf(q * jnp.exp(bs))
    vt_s[...] = v_ref[...].T

    ri = lax.broadcasted_iota(jnp.int32, (tb, tb), 0)
    ci = lax.broadcasted_iota(jnp.int32, (tb, tb), 1)
    delta = jnp.where(ri // GLA_SC == ci // GLA_SC, ri // GLA_CHUNK - ci // GLA_CHUNK, -1)
    m_intra = (delta == 0) & (ci <= ri)
    m_dist = [delta == d + 1 for d in range(nd)]
    kds = [_bf(kd)]
    for d in range(1, nd):
        kd = kd * per_chunk(etot, d)
        kds.append(_bf(kd))
    for h in range(GLA_H):
        ks = slice(h * GLA_DK, (h + 1) * GLA_DK)
        vs = slice(h * GLA_DV, (h + 1) * GLA_DV)
        a = jnp.where(m_intra, _dot_nt(qin[:, ks], kout[:, ks]), 0.0)
        for d in range(nd):
            a = jnp.where(m_dist[d], _dot_nt(qin[:, ks], kds[d][:, ks]), a)
        o_s[:, vs] = _dot(a, v_ref[:, vs])

    for sc in range(nsc):
        rows = slice(sc * GLA_SC, (sc + 1) * GLA_SC)
        last = bs[(sc + 1) * GLA_SC - 1:(sc + 1) * GLA_SC, :]
        k2 = _bf(k[rows, :] * jnp.exp(last - bs[rows, :]))
        elast = jnp.exp(last)
        for h in range(GLA_H):
            ks = slice(h * GLA_DK, (h + 1) * GLA_DK)
            vs = slice(h * GLA_DV, (h + 1) * GLA_DV)
            st = st_s[h]
            o_s[rows, vs] += _dot_nt(qsc[rows, ks], st)
            st_s[h] = st * elast[:, ks] + _dot(vt_s[vs, rows], k2[:, ks])

    nw = nw_ref[...]
    for h in range(GLA_H):
        vs = slice(h * GLA_DV, (h + 1) * GLA_DV)
        o = o_s[:, vs]
        y = o * lax.rsqrt(jnp.mean(o * o, axis=-1, keepdims=True) + EPS) * nw
        oa_ref[:, vs] = (y * _silu(ga_ref[:, vs])).astype(oa_ref.dtype)

    @pl.when(t == pl.num_programs(1) - 1)
    def _():
        for h in range(GLA_H):
            sfin_ref[h] = st_s[h].T


def _gla_prompt(layer, z, lr, wgk_pad, b_gk, gla_norm_w):
    tb = 256
    nt = SEQ // tb
    row = lambda b, t: b * nt + t
    kern = functools.partial(_gla_prompt_kernel, tb=tb)
    return pl.pallas_call(
        kern,
        grid=(BATCH, nt),
        in_specs=[pl.BlockSpec((tb, GLA_KEY), lambda b, t: (row(b, t), Z_QA // GLA_KEY)),
                  pl.BlockSpec((tb, GLA_KEY), lambda b, t: (row(b, t), Z_KA // GLA_KEY)),
                  pl.BlockSpec((tb, GLA_VAL), lambda b, t: (row(b, t), Z_VA // GLA_VAL)),
                  pl.BlockSpec((tb, GLA_VAL), lambda b, t: (row(b, t), Z_GA // GLA_VAL)),
                  pl.BlockSpec((tb, LR_PAD), lambda b, t: (row(b, t), 0)),
                  pl.BlockSpec((None, LR_PAD, GLA_KEY), lambda b, t: (layer, 0, 0)),
                  pl.BlockSpec((None, 1, GLA_KEY), lambda b, t: (layer, 0, 0)),
                  pl.BlockSpec((None, 1, GLA_DV), lambda b, t: (layer, 0, 0))],
        out_specs=[pl.BlockSpec((tb, GLA_VAL), lambda b, t: (row(b, t), 0)),
                   pl.BlockSpec((None, GLA_H, GLA_DK, GLA_DV), lambda b, t: (b, 0, 0, 0))],
        out_shape=[jax.ShapeDtypeStruct((BATCH * SEQ, GLA_VAL), BF16),
                   jax.ShapeDtypeStruct((BATCH, GLA_H, GLA_DK, GLA_DV), F32)],
        scratch_shapes=[pltpu.VMEM((GLA_H, GLA_DV, GLA_DK), F32),
                        pltpu.VMEM((GLA_VAL, tb), F32),
                        pltpu.VMEM((tb, GLA_VAL), F32)],
        compiler_params=_cparams(2),
        name="gla_prompt",
    )(z, z, z, z, lr, wgk_pad, b_gk.reshape(DEPTH, 1, GLA_KEY), gla_norm_w.reshape(DEPTH, 1, GLA_DV))


def _layernorm(x, w, b):
    mu = jnp.mean(x, axis=-1, keepdims=True)
    xc = x - mu
    var = jnp.mean(xc * xc, axis=-1, keepdims=True)
    return xc * lax.rsqrt(var + EPS) * w + b


def _gmlp_prompt_kernel(u_ref, v_ref, ws_ref, bst_ref, nw_ref, nb_ref, ob_ref, *, nsub):
    ri = lax.broadcasted_iota(jnp.int32, (GM_CHUNK, GM_CHUNK), 0)
    ci = lax.broadcasted_iota(jnp.int32, (GM_CHUNK, GM_CHUNK), 1)
    tril = ci <= ri
    for s in range(nsub):
        rs = slice(s * GM_CHUNK, (s + 1) * GM_CHUNK)
        u = _gelu(u_ref[rs, :])
        v = _layernorm(_gelu(v_ref[rs, :]), nw_ref[...], nb_ref[...])
        for g in range(GM_GROUPS):
            cs = slice(g * GM_GW, (g + 1) * GM_GW)
            wm = jnp.where(tril, ws_ref[g], 0.0)
            mixed = _dot(wm, v[:, cs]) + bst_ref[:, g:g + 1]
            ob_ref[rs, cs] = (u[:, cs] * mixed).astype(ob_ref.dtype)


def _gmlp_prompt(layer, z, gm_ws, gm_bs_t, gm_norm_w, gm_norm_b):
    nsub = 8
    tb = nsub * GM_CHUNK
    kern = functools.partial(_gmlp_prompt_kernel, nsub=nsub)
    return pl.pallas_call(
        kern,
        grid=(BATCH * SEQ // tb,),
        in_specs=[pl.BlockSpec((tb, GM_WIDTH), lambda i: (i, Z_UB // GM_WIDTH)),
                  pl.BlockSpec((tb, GM_WIDTH), lambda i: (i, Z_VB // GM_WIDTH)),
                  pl.BlockSpec((None, GM_GROUPS, GM_CHUNK, GM_CHUNK), lambda i: (layer, 0, 0, 0)),
                  pl.BlockSpec((None, GM_CHUNK, GM_GROUPS), lambda i: (layer, 0, 0)),
                  pl.BlockSpec((None, 1, GM_WIDTH), lambda i: (layer, 0, 0)),
                  pl.BlockSpec((None, 1, GM_WIDTH), lambda i: (layer, 0, 0))],
        out_specs=pl.BlockSpec((tb, GM_WIDTH), lambda i: (i, 0)),
        out_shape=jax.ShapeDtypeStruct((BATCH * SEQ, GM_WIDTH), BF16),
        compiler_params=_cparams(1),
        name="gmlp_prompt",
    )(z, z, gm_ws, gm_bs_t, gm_norm_w.reshape(DEPTH, 1, GM_WIDTH), gm_norm_b.reshape(DEPTH, 1, GM_WIDTH))


def _gmlp_sample_kernel(u_ref, v_ref, w0_ref, b0_ref, nw_ref, nb_ref, ob_ref, vn_ref):
    u = _gelu(u_ref[...])
    v = _layernorm(_gelu(v_ref[...]), nw_ref[...], nb_ref[...])
    vn_ref[...] = v
    ob_ref[...] = u * (w0_ref[...] * v + b0_ref[...])


def _gmlp_sample(layer, z, w0_row, b0_row, gm_norm_w, gm_norm_b):
    full = lambda i: (0, 0)
    lrow = lambda i: (layer, 0, 0)
    return pl.pallas_call(
        _gmlp_sample_kernel,
        grid=(1,),
        in_specs=[pl.BlockSpec((DEC_BATCH, GM_WIDTH), lambda i: (0, Z_UB // GM_WIDTH)),
                  pl.BlockSpec((DEC_BATCH, GM_WIDTH), lambda i: (0, Z_VB // GM_WIDTH)),
                  pl.BlockSpec((None, 1, GM_WIDTH), lrow),
                  pl.BlockSpec((None, 1, GM_WIDTH), lrow),
                  pl.BlockSpec((None, 1, GM_WIDTH), lrow),
                  pl.BlockSpec((None, 1, GM_WIDTH), lrow)],
        out_specs=[pl.BlockSpec((DEC_BATCH, GM_WIDTH), full), pl.BlockSpec((DEC_BATCH, GM_WIDTH), full)],
        out_shape=[jax.ShapeDtypeStruct((DEC_BATCH, GM_WIDTH), F32),
                   jax.ShapeDtypeStruct((DEC_BATCH, GM_WIDTH), F32)],
        compiler_params=_cparams(1),
        name="gmlp_sample",
    )(z, z, w0_row, b0_row, gm_norm_w.reshape(DEPTH, 1, GM_WIDTH), gm_norm_b.reshape(DEPTH, 1, GM_WIDTH))


def _alibi_slope(h):
    return float(2.0 ** (-8.0 * (h + 1) / SWA_HQ))


def _swa_lane_halves(x, half):
    lane = lax.broadcasted_iota(jnp.int32, x.shape, 1)
    own = jnp.where((lane >= half * SWA_HD) & (lane < (half + 1) * SWA_HD), x, 0.0)
    other = pltpu.roll(own, SWA_HD, 1)
    return (own, other) if half == 0 else (other, own)


def _swa_prompt_block(n, q_ref, kc_ref, kp_ref, vc_ref, vp_ref, sink_ref, oc_ref, s_s, p_s):
    w = WINDOW
    ri = lax.broadcasted_iota(jnp.int32, (w, 2 * w), 0)
    ci = lax.broadcasted_iota(jnp.int32, (w, 2 * w), 1)
    dist_i = w + ri - ci
    valid = (dist_i >= 0) & (dist_i < w) & ((ci >= w) | (n > 0))
    dist = dist_i.astype(F32)
    kcat = jnp.concatenate([kp_ref[...], kc_ref[...]], axis=0)
    vcat = jnp.concatenate([vp_ref[...], vc_ref[...]], axis=0)
    heads = []
    for kv in range(SWA_HKV):
        t, half = kv // 2, kv % 2
        k_lo, k_hi = _swa_lane_halves(kcat[:, t * LANE:(t + 1) * LANE], half)
        q2 = jnp.concatenate([q_ref[:, 2 * kv * LANE:(2 * kv + 1) * LANE],
                              q_ref[:, (2 * kv + 1) * LANE:(2 * kv + 2) * LANE]], axis=0)
        for par, kk in ((0, k_lo), (1, k_hi)):
            s = _dot_nt(q2, kk) * (SWA_HD ** -0.5)
            for e in range(2):
                h = SWA_G * kv + 2 * e + par
                seg = len(heads)
                heads.append(h)
                s_s[seg * w:(seg + 1) * w, :] = jnp.where(
                    valid, s[e * w:(e + 1) * w, :] - _alibi_slope(h) * dist, NEG_BIG)
    s = s_s[...]
    sink = jnp.concatenate([jnp.broadcast_to(sink_ref[h:h + 1, 0:1], (w, 1)) for h in heads], axis=0)
    m = jnp.maximum(jnp.max(s, axis=-1, keepdims=True), sink)
    p = jnp.exp(s - m)
    inv = 1.0 / (jnp.sum(p, axis=-1, keepdims=True) + jnp.exp(sink - m))
    p_s[...] = (p * inv).astype(p_s.dtype)
    for kv in range(SWA_HKV):
        t, half = kv // 2, kv % 2
        v_lo, v_hi = _swa_lane_halves(vcat[:, t * LANE:(t + 1) * LANE], half)
        r0 = SWA_G * kv * w
        o = _dot(p_s[r0:r0 + 2 * w, :], v_lo) + _dot(p_s[r0 + 2 * w:r0 + 4 * w, :], v_hi)
        oc_ref[:, 2 * kv * LANE:(2 * kv + 1) * LANE] = o[:w].astype(oc_ref.dtype)
        oc_ref[:, (2 * kv + 1) * LANE:(2 * kv + 2) * LANE] = o[w:].astype(oc_ref.dtype)


STATE_SLOTS = 3


def _state_copy(hbm, sbuf, sem, layer, step, rps, to_hbm):
    slot = lax.rem(step, STATE_SLOTS)
    rows = hbm.at[layer, pl.ds(step * rps, rps)]
    if to_hbm:
        return pltpu.make_async_copy(sbuf.at[slot], rows, sem.at[slot])
    return pltpu.make_async_copy(rows, sbuf.at[slot], sem.at[slot])


def _state_ring_begin(g, n_steps, copy_in, copy_out, x_s, oa_s):
    @pl.when(g == 0)
    def _():
        x_s[...] = jnp.zeros_like(x_s)
        oa_s[...] = jnp.zeros_like(oa_s)
        copy_in(0).start()

    @pl.when(g >= STATE_SLOTS - 1)
    def _():
        copy_out(g - (STATE_SLOTS - 1)).wait()

    @pl.when(g + 1 < n_steps)
    def _():
        copy_in(g + 1).start()

    copy_in(g).wait()


def _state_ring_end(g, n_steps, copy_out):
    copy_out(g).start()

    @pl.when(g == n_steps - 1)
    def _():
        for back in range(STATE_SLOTS - 2, -1, -1):
            copy_out(g - back).wait()


def _gla_decode_rows(g, rps, zq_ref, zk_ref, zv_ref, zga_ref, lr_ref, wgk_ref, bgk_ref, nw_ref,
                     oa_ref, sbuf, x_s, oa_s):
    slot = lax.rem(g, STATE_SLOTS)
    sub = lax.rem(g * rps, SUBLANE)
    up = lax.rem(SUBLANE - sub, SUBLANE)
    u = _dot(lr_ref[...], wgk_ref[...]) + bgk_ref[...]
    eg = pltpu.roll(jnp.exp(_log_sigmoid(u) * (1.0 / 16.0)), up, 0)
    zq = pltpu.roll(zq_ref[...], up, 0) * (GLA_DK ** -0.5)
    zk = pltpu.roll(zk_ref[...], up, 0)
    zv = pltpu.roll(zv_ref[...], up, 0)
    gate = _silu(pltpu.roll(zga_ref[...], up, 0))
    nw = nw_ref[...]
    row8 = lax.broadcasted_iota(jnp.int32, (SUBLANE, GLA_DV), 0)
    for h in range(GLA_H):
        ks = slice(h * GLA_DK, (h + 1) * GLA_DK)
        vs = slice(h * GLA_DV, (h + 1) * GLA_DV)
        x_s[0:SUBLANE, :] = zq[:, ks]
        x_s[SUBLANE:2 * SUBLANE, :] = zk[:, ks]
        x_s[2 * SUBLANE:3 * SUBLANE, :] = eg[:, ks]
        xt = x_s[...].T
        y8 = jnp.zeros((SUBLANE, GLA_DV), F32)
        for r in range(rps):
            qc = xt[:, r:r + 1]
            kc = xt[:, SUBLANE + r:SUBLANE + r + 1]
            gc = xt[:, 2 * SUBLANE + r:2 * SUBLANE + r + 1]
            s_new = gc * sbuf[slot, r, h] + kc * zv[r:r + 1, vs]
            sbuf[slot, r, h] = s_new
            o = jnp.sum(qc * s_new, axis=0, keepdims=True)
            y = o * lax.rsqrt(jnp.mean(o * o, axis=-1, keepdims=True) + EPS) * nw * gate[r:r + 1, vs]
            y8 = jnp.where(row8 == r, y, y8)
        y8 = pltpu.roll(y8, sub, 0)
        acc = jnp.where(sub == 0, y8, oa_s[:, vs] + y8)
        oa_s[:, vs] = acc
        oa_ref[:, vs] = acc


def _swa_gla_kernel(q_ref, kc_ref, kp_ref, vc_ref, vp_ref, sink_ref,
                    zq_ref, zk_ref, zv_ref, zga_ref, lr_ref, wgk_ref, bgk_ref, nw_ref, sin_hbm, *rest,
                    layer, rps, aliased):
    if aliased:
        rest = rest[1:]
    oc_ref, oa_ref, sout_hbm, s_s, p_s, sbuf, x_s, oa_s, in_sem, out_sem = rest
    n = pl.program_id(1)
    g = pl.program_id(0) * pl.num_programs(1) + n
    n_steps = pl.num_programs(0) * pl.num_programs(1)
    copy_in = functools.partial(_state_copy, sin_hbm, sbuf, in_sem, layer, rps=rps, to_hbm=False)
    copy_out = functools.partial(_state_copy, sout_hbm, sbuf, out_sem, layer, rps=rps, to_hbm=True)
    _state_ring_begin(g, n_steps, copy_in, copy_out, x_s, oa_s)
    _gla_decode_rows(g, rps, zq_ref, zk_ref, zv_ref, zga_ref, lr_ref, wgk_ref, bgk_ref, nw_ref,
                     oa_ref, sbuf, x_s, oa_s)
    _swa_prompt_block(n, q_ref, kc_ref, kp_ref, vc_ref, vp_ref, sink_ref, oc_ref, s_s, p_s)
    _state_ring_end(g, n_steps, copy_out)


def _swa_prompt_gla_sample(layer, zp, sinks_b, zs, lrs, wgk_pad, b_gk, gla_norm_w, state_gla, state_out):
    nb = SEQ // WINDOW
    n_steps = BATCH * nb
    rps = DEC_BATCH // n_steps
    assert rps * n_steps == DEC_BATCH and SUBLANE % rps == 0 and n_steps >= STATE_SLOTS
    row = lambda b, n: b * nb + n
    prev = lambda b, n: b * nb + jnp.maximum(n - 1, 0)
    srow = lambda b, n: (row(b, n) * rps) // SUBLANE
    in_specs = [pl.BlockSpec((WINDOW, SWA_Q), lambda b, n: (row(b, n), Z_QC // SWA_Q)),
                pl.BlockSpec((WINDOW, SWA_KV), lambda b, n: (row(b, n), Z_KC // SWA_KV)),
                pl.BlockSpec((WINDOW, SWA_KV), lambda b, n: (prev(b, n), Z_KC // SWA_KV)),
                pl.BlockSpec((WINDOW, SWA_KV), lambda b, n: (row(b, n), Z_VC // SWA_KV)),
                pl.BlockSpec((WINDOW, SWA_KV), lambda b, n: (prev(b, n), Z_VC // SWA_KV)),
                pl.BlockSpec((None, SWA_HQ, LANE), lambda b, n: (layer, 0, 0)),
                pl.BlockSpec((SUBLANE, GLA_KEY), lambda b, n: (srow(b, n), Z_QA // GLA_KEY)),
                pl.BlockSpec((SUBLANE, GLA_KEY), lambda b, n: (srow(b, n), Z_KA // GLA_KEY)),
                pl.BlockSpec((SUBLANE, GLA_VAL), lambda b, n: (srow(b, n), Z_VA // GLA_VAL)),
                pl.BlockSpec((SUBLANE, GLA_VAL), lambda b, n: (srow(b, n), Z_GA // GLA_VAL)),
                pl.BlockSpec((SUBLANE, LR_PAD), lambda b, n: (srow(b, n), 0)),
                pl.BlockSpec((None, LR_PAD, GLA_KEY), lambda b, n: (layer, 0, 0)),
                pl.BlockSpec((None, 1, GLA_KEY), lambda b, n: (layer, 0, 0)),
                pl.BlockSpec((None, 1, GLA_DV), lambda b, n: (layer, 0, 0)),
                pl.BlockSpec(memory_space=pl.ANY)]
    args = [zp, zp, zp, zp, zp, sinks_b, zs, zs, zs, zs, lrs, wgk_pad, b_gk.reshape(DEPTH, 1, GLA_KEY),
            gla_norm_w.reshape(DEPTH, 1, GLA_DV), state_gla]
    aliases = {}
    if state_out is not None:
        in_specs.append(pl.BlockSpec(memory_space=pl.ANY))
        args.append(state_out)
        aliases = {len(args) - 1: 2}
    kern = functools.partial(_swa_gla_kernel, layer=layer, rps=rps, aliased=state_out is not None)
    return pl.pallas_call(
        kern,
        grid=(BATCH, nb),
        in_specs=in_specs,
        out_specs=[pl.BlockSpec((WINDOW, SWA_Q), lambda b, n: (row(b, n), 0)),
                   pl.BlockSpec((SUBLANE, GLA_VAL), lambda b, n: (srow(b, n), 0)),
                   pl.BlockSpec(memory_space=pl.ANY)],
        out_shape=[jax.ShapeDtypeStruct((BATCH * SEQ, SWA_Q), BF16),
                   jax.ShapeDtypeStruct((DEC_BATCH, GLA_VAL), F32),
                   jax.ShapeDtypeStruct((DEPTH, DEC_BATCH, GLA_H, GLA_DK, GLA_DV), F32)],
        scratch_shapes=[pltpu.VMEM((SWA_HQ * WINDOW, 2 * WINDOW), F32),
                        pltpu.VMEM((SWA_HQ * WINDOW, 2 * WINDOW), BF16),
                        pltpu.VMEM((STATE_SLOTS, rps, GLA_H, GLA_DK, GLA_DV), F32),
                        pltpu.VMEM((LANE, GLA_DK), F32),
                        pltpu.VMEM((SUBLANE, GLA_VAL), F32),
                        pltpu.SemaphoreType.DMA((STATE_SLOTS,)),
                        pltpu.SemaphoreType.DMA((STATE_SLOTS,))],
        input_output_aliases=aliases,
        compiler_params=_cparams(2),
        name="swa_prompt_gla_sample",
    )(*args)


def _swa_sample_kernel(q_ref, kn_ref, vn_ref, kt_ref, vt_ref, sink_ref, slope_ref, o_ref, *, rb):
    wb = WINDOW
    j = lax.broadcasted_iota(jnp.int32, (SWA_HQ, wb), 1)
    dist = (wb - j).astype(F32)
    ok = j >= 1
    grp = lax.broadcasted_iota(jnp.int32, (SWA_HQ, 1), 0) // SWA_G
    slope = slope_ref[:, 0:1]
    sink = sink_ref[:, 0:1]

    def per_head(rows):
        out = jnp.broadcast_to(rows[0:1, :], (SWA_HQ, SWA_HD))
        for kv in range(1, SWA_HKV):
            out = jnp.where(grp == kv, rows[kv:kv + 1, :], out)
        return out

    s_rows, self_rows = [], []
    for r in range(rb):
        q = q_ref[r]
        s = _dot(q, kt_ref[r, 0])
        for kv in range(1, SWA_HKV):
            s = jnp.where(grp == kv, _dot(q, kt_ref[r, kv]), s)
        s_rows.append(jnp.where(ok, s * (SWA_HD ** -0.5) - slope * dist, NEG_BIG))
        self_rows.append(jnp.sum(_bf(q).astype(F32) * _bf(per_head(kn_ref[r])).astype(F32), axis=-1,
                                 keepdims=True) * (SWA_HD ** -0.5))
    s = jnp.concatenate(s_rows, axis=0)
    s_self = jnp.concatenate(self_rows, axis=0)
    sink = jnp.concatenate([sink] * rb, axis=0)
    m = jnp.maximum(jnp.maximum(jnp.max(s, axis=-1, keepdims=True), s_self), sink)
    p = jnp.exp(s - m)
    p_self = jnp.exp(s_self - m)
    inv = 1.0 / (jnp.sum(p, axis=-1, keepdims=True) + p_self + jnp.exp(sink - m))
    pn = _bf(p * inv)
    pn_self = _bf(p_self * inv).astype(F32)
    for r in range(rb):
        rows = slice(r * SWA_HQ, (r + 1) * SWA_HQ)
        o = _dot_nt(pn[rows, :], vt_ref[r, 0])
        for kv in range(1, SWA_HKV):
            o = jnp.where(grp == kv, _dot_nt(pn[rows, :], vt_ref[r, kv]), o)
        o_ref[r] = o + pn_self[rows, :] * _bf(per_head(vn_ref[r])).astype(F32)


def _swa_sample(layer, q3, kn3, vn3, cache_kt, cache_vt, sinks_b, slopes_b):
    rb = 16
    kern = functools.partial(_swa_sample_kernel, rb=rb)
    return pl.pallas_call(
        kern,
        grid=(DEC_BATCH // rb,),
        in_specs=[pl.BlockSpec((rb, SWA_HQ, SWA_HD), lambda i: (i, 0, 0)),
                  pl.BlockSpec((rb, SWA_HKV, SWA_HD), lambda i: (i, 0, 0)),
                  pl.BlockSpec((rb, SWA_HKV, SWA_HD), lambda i: (i, 0, 0)),
                  pl.BlockSpec((None, rb, SWA_HKV, SWA_HD, WINDOW), lambda i: (layer, i, 0, 0, 0)),
                  pl.BlockSpec((None, rb, SWA_HKV, SWA_HD, WINDOW), lambda i: (layer, i, 0, 0, 0)),
                  pl.BlockSpec((None, SWA_HQ, LANE), lambda i: (layer, 0, 0)),
                  pl.BlockSpec((SWA_HQ, LANE), lambda i: (0, 0))],
        out_specs=pl.BlockSpec((rb, SWA_HQ, SWA_HD), lambda i: (i, 0, 0)),
        out_shape=jax.ShapeDtypeStruct((DEC_BATCH, SWA_HQ, SWA_HD), F32),
        compiler_params=_cparams(1),
        name="swa_sample",
    )(q3, kn3, vn3, cache_kt, cache_vt, sinks_b, slopes_b)


def _w_in_kernel(hp_ref, hs_ref, wa_ref, wb_ref, zp_ref, zs_ref, *, n_plain):
    j = pl.program_id(1)
    a = wa_ref[...]
    shifted = jnp.concatenate([a[GLA_RANK:, :], wb_ref[...]], axis=0)
    w = _bf(jnp.where(j >= n_plain, shifted, a))
    zp_ref[...] = _dot_nt(hp_ref[...], w)

    @pl.when(pl.program_id(0) == 0)
    def _():
        zs_ref[...] = _dot_nt(hs_ref[...], w)


def _w_in(layer, hp, hs, w_in_t):
    tm, tn = SEQ, 512
    assert LR_COL % tn == 0 and Z_WIDTH % tn == 0 and tn % GLA_RANK == 0
    rows_p, rows_s = hp.shape[0], hs.shape[0]
    nj = Z_WIDTH // tn
    sj = lambda i, j: jnp.where(i == 0, j, nj - 1)
    return pl.pallas_call(
        functools.partial(_w_in_kernel, n_plain=LR_COL // tn),
        grid=(rows_p // tm, nj),
        in_specs=[pl.BlockSpec((tm, D), lambda i, j: (i, 0), pipeline_mode=pl.Buffered(1)),
                  pl.BlockSpec((rows_s, D), lambda i, j: (0, 0)),
                  pl.BlockSpec((None, tn, D), lambda i, j: (layer, j, 0)),
                  pl.BlockSpec((None, GLA_RANK, D), lambda i, j: (layer, (j + 1) * (tn // GLA_RANK), 0))],
        out_specs=[pl.BlockSpec((tm, tn), lambda i, j: (i, j)),
                   pl.BlockSpec((rows_s, tn), lambda i, j: (0, sj(i, j)))],
        out_shape=[jax.ShapeDtypeStruct((rows_p, Z_WIDTH), F32), jax.ShapeDtypeStruct((rows_s, Z_WIDTH), F32)],
        compiler_params=_cparams(2),
        name="w_in",
    )(hp, hs, w_in_t, w_in_t)


def _layer(layer, xp, xs, mod, p, state_gla, cache_k, cache_v, state_out):
    hp, lrp = _prep(layer, xp, p["norm1_w"], mod, MOD_SC1, MOD_SH1, False, p["w_lr_t"])
    hs, lrs = _prep(layer, xs, p["norm1_w"], mod, MOD_SC1, MOD_SH1, True, p["w_lr_t"])
    zp, zs = _w_in(layer, hp, hs, p["w_in_t"])
    oa_p, s_p = _gla_prompt(layer, zp, lrp, p["wgk_pad"], p["b_gk"], p["gla_norm_w"])
    ob_p = _gmlp_prompt(layer, zp, p["gm_ws"], p["gm_bs_t"], p["gm_norm_w"], p["gm_norm_b"])
    oc_p, oa_s, state_out = _swa_prompt_gla_sample(layer, zp, p["sinks_b"], zs, lrs, p["wgk_pad"], p["b_gk"],
                                                   p["gla_norm_w"], state_gla, state_out)
    z4 = zp.reshape(BATCH, SEQ, Z_WIDTH)
    kp_rows = z4[:, SEQ - WINDOW:, Z_KC:Z_KC + SWA_KV].reshape(BATCH, WINDOW, SWA_HKV, SWA_HD)
    vp_rows = z4[:, SEQ - WINDOW:, Z_VC:Z_VC + SWA_KV].reshape(BATCH, WINDOW, SWA_HKV, SWA_HD)
    ob_s, v_gm = _gmlp_sample(layer, zs, p["gm_w0"], p["gm_b0"], p["gm_norm_w"], p["gm_norm_b"])
    q3 = zs[:, Z_QC:Z_QC + SWA_Q].reshape(DEC_BATCH, SWA_HQ, SWA_HD)
    kn3 = zs[:, Z_KC:Z_KC + SWA_KV].reshape(DEC_BATCH, SWA_HKV, SWA_HD)
    vn3 = zs[:, Z_VC:Z_VC + SWA_KV].reshape(DEC_BATCH, SWA_HKV, SWA_HD)
    oc_s = _swa_sample(layer, q3, kn3, vn3, cache_k, cache_v, p["sinks_b"], p["slopes_b"]).reshape(DEC_BATCH, SWA_Q)
    ks_rows = kn3.reshape(DEC_BATCH, 1, SWA_HKV, SWA_HD)
    vs_rows = vn3.reshape(DEC_BATCH, 1, SWA_HKV, SWA_HD)
    (mp,), (ms,) = _fused_matmul(
        "merge", layer, [oa_p, ob_p, oc_p], [oa_s, ob_s, oc_s],
        [(0, p["w_pa"], 0), (1, p["w_pb"], 0), (2, p["w_pc"], 0)],
        [("tile", zp, zs, Z_GATES), ("tile", zp, zs, Z_GATES + D), ("tile", zp, zs, Z_GATES + 2 * D)],
        _epi_merge, [BF16], D, 1024, 512)
    x1p, h2p, x1s, h2s = _wo_prep(layer, mp, ms, p["w_o_bf"], xp, xs, p["norm2_w"], mod)
    (hidp,), (hids,) = _fused_matmul(
        "ffn_in", layer, [h2p], [h2s], [(0, p["w_ffn_in"], 0), (0, p["w_ffn_in"], FFN_HIDDEN)],
        [], _epi_swiglu, [BF16], FFN_HIDDEN, 2048, 512)
    (x2p,), (x2s,) = _fused_matmul("ffn_out", layer, [hidp], [hids], [(0, p["w_ffn_out"], 0)],
                                   [("tile", x1p, x1s, 0), ("mod", mod, MOD_G2)], _epi_residual, [F32],
                                   D, 1024, 512)
    return x2p, x2s, s_p, state_out, kp_rows, vp_rows, ks_rows, vs_rows, v_gm


def kernel(x_prompt, x_sample, c_prompt, c_sample, state_gla, cache_swa_k, cache_swa_v, w_ada, b_ada, norm1_w,
           norm2_w, w_in, w_gk2, b_gk, gla_norm_w, gm_norm_w, gm_norm_b, gm_ws, gm_bs, swa_sinks, w_pa, w_pb,
           w_pc, w_o, w_ffn_in, w_ffn_out, final_norm_w):
    w_in_t = jnp.swapaxes(w_in, 1, 2)
    w_lr_t = jnp.pad(w_in_t[:, LR_COL:LR_COL + GLA_RANK, :], ((0, 0), (0, LR_PAD - GLA_RANK), (0, 0))).astype(BF16)
    p = {
        "norm1_w": norm1_w, "norm2_w": norm2_w, "w_in_t": w_in_t, "w_lr_t": w_lr_t,
        "wgk_pad": jnp.pad(w_gk2, ((0, 0), (0, LR_PAD - GLA_RANK), (0, 0))),
        "b_gk": b_gk, "gla_norm_w": gla_norm_w, "gm_norm_w": gm_norm_w, "gm_norm_b": gm_norm_b,
        "gm_ws": gm_ws, "gm_bs_t": jnp.swapaxes(gm_bs, 1, 2),
        "gm_w0": jnp.repeat(gm_ws[:, :, 0, 0], GM_GW, axis=1).reshape(DEPTH, 1, GM_WIDTH),
        "gm_b0": jnp.repeat(gm_bs[:, :, 0], GM_GW, axis=1).reshape(DEPTH, 1, GM_WIDTH),
        "sinks_b": jnp.broadcast_to(swa_sinks[:, :, None], (DEPTH, SWA_HQ, LANE)),
        "slopes_b": jnp.broadcast_to(
            jnp.asarray([_alibi_slope(h) for h in range(SWA_HQ)], F32)[:, None], (SWA_HQ, LANE)),
        "w_pa": w_pa, "w_pb": w_pb, "w_pc": w_pc, "w_o_bf": w_o.astype(BF16), "w_ffn_in": w_ffn_in,
        "w_ffn_out": w_ffn_out.astype(BF16),
    }
    c_all = jnp.concatenate([c_sample, c_prompt, jnp.zeros((MOD_ROWS - DEC_BATCH - BATCH, D), F32)], axis=0)
    mod = _ada(c_all, w_ada, b_ada)

    xp = x_prompt.reshape(BATCH * SEQ, D)
    xs = x_sample.reshape(DEC_BATCH, D)
    cache_k = jnp.transpose(cache_swa_k, (0, 1, 3, 4, 2))
    cache_v = jnp.transpose(cache_swa_v, (0, 1, 3, 4, 2))
    gla_p, kp, vp, ksm, vsm, gmv = [], [], [], [], [], []
    state_out = None
    for l in range(DEPTH):
        xp, xs, s_p, state_out, k_p, v_p, k_s, v_s, gv = _layer(l, xp, xs, mod, p, state_gla, cache_k, cache_v,
                                                                state_out)
        gla_p.append(s_p)
        kp.append(k_p)
        vp.append(v_p)
        ksm.append(k_s)
        vsm.append(v_s)
        gmv.append(gv.reshape(DEC_BATCH, 1, GM_WIDTH))
    y_prompt = _final_norm(xp, final_norm_w).reshape(BATCH, SEQ, D)
    y_sample = _final_norm(xs, final_norm_w).reshape(DEC_BATCH, 1, D)
    return (y_prompt, y_sample, jnp.stack(gla_p), state_out, jnp.stack(kp), jnp.stack(vp),
            jnp.stack(ksm), jnp.stack(vsm), jnp.stack(gmv))
```
